```python
import jax
import jax.numpy as jnp
from jax import lax
import numpy as np

D_MODEL = 1024
BATCH = 8
SEQ = 2048
DEPTH = 4
DEC_BATCH = 32
DEC_SEQ = 4
PAST_LEN = 8192
PAGE_SIZE = 128

N_MIXERS = 3
N_LAYERS_A = (DEPTH + 2) // 3
N_LAYERS_B = (DEPTH + 1) // 3
N_LAYERS_C = DEPTH // 3

D_CONV = D_MODEL
CONV_A_WIDTH = 3

N_HEADS = 16
HEAD_DIM = D_MODEL // N_HEADS
N_KV = 2
HPG = N_HEADS // N_KV
D_ATT = N_HEADS * HEAD_DIM
CMP_BLOCK = 32
SLC_BLOCK = 64
CMP_PER_SLC = SLC_BLOCK // CMP_BLOCK
TOP_N = 16
N_LOCAL = 2
WINDOW = 512
Q_BLOCK = 32
NEG_INF = -1e30
FORCE = 1e9

D_RNN = D_MODEL
N_RG_BLOCKS = 4
RG_BLOCK = D_RNN // N_RG_BLOCKS
CONV_C_WIDTH = 4
RG_C = 8.0

ALPHA = (2 * DEPTH) ** 0.25
BETA = (8 * DEPTH) ** -0.25
LN_EPS = 1e-5

kernel_name = 'hybrid_conv_nsa_rglru_deepnorm_step'


def layer_norm(x, g, b):
    xf = x.astype(jnp.float32)
    mu = jnp.mean(xf, axis=-1, keepdims=True)
    var = jnp.mean(jnp.square(xf - mu), axis=-1, keepdims=True)
    return ((xf - mu) * lax.rsqrt(var + LN_EPS) * g.astype(jnp.float32) + b.astype(jnp.float32)).astype(x.dtype)


def causal_dwconv(u, buf, w):
    width = w.shape[0]
    t_len = u.shape[1]
    ext = jnp.concatenate([buf.astype(u.dtype), u], axis=1)
    out = ext[:, 0:t_len] * w[0]
    for k in range(1, width):
        out = out + ext[:, k:k + t_len] * w[k]
    return out, ext[:, ext.shape[1] - (width - 1):]


def masked_softmax(s, valid):
    s = jnp.where(valid, s, NEG_INF)
    return jax.nn.softmax(s, axis=-1) * valid


def alibi_slopes():
    h = jnp.arange(1, N_HEADS + 1, dtype=jnp.float32)
    return (2.0 ** (-8.0 * h / N_HEADS)).reshape(N_KV, HPG)


def short_conv_mixer(x, buf, w_in, conv_w, w_out):
    h, b_gate, c_gate, z = jnp.split(x @ w_in, 4, axis=-1)
    conv, new_buf = causal_dwconv(c_gate * h, buf, conv_w)
    return (jax.nn.silu(z) * b_gate * conv) @ w_out, new_buf


def rglru_mixer(x, h0, buf, w_in, conv_w, conv_b, w_a, b_a, w_x, b_x, lam, w_out):
    bsz, t_len, _ = x.shape
    u, z = jnp.split(x @ w_in, 2, axis=-1)
    uc, new_buf = causal_dwconv(u, buf, conv_w)
    uc = uc + conv_b
    ub = uc.reshape(bsz, t_len, N_RG_BLOCKS, RG_BLOCK)
    r = jax.nn.sigmoid(jnp.einsum('btni,nij->btnj', ub, w_a).reshape(bsz, t_len, D_RNN) + b_a)
    i = jax.nn.sigmoid(jnp.einsum('btni,nij->btnj', ub, w_x).reshape(bsz, t_len, D_RNN) + b_x)
    log_a = (-RG_C * jax.nn.softplus(-lam.astype(jnp.float32))) * r.astype(jnp.float32)
    a = jnp.exp(log_a)
    b = jnp.sqrt(-jnp.expm1(2.0 * log_a)) * (i * uc).astype(jnp.float32)
    b = b.at[:, 0].add(a[:, 0] * h0.astype(jnp.float32))

    def combine(left, right):
        return left[0] * right[0], right[0] * left[1] + right[1]

    _, h = lax.associative_scan(combine, (a, b), axis=1)
    y = (jax.nn.silu(z) * h.astype(x.dtype)) @ w_out
    return y, h[:, -1].astype(x.dtype), new_buf


def nsa_mixer(x, past_rows, win_buf, w_in, w_cmp, w_out):
    bsz, t_len, _ = x.shape
    p_len = past_rows.shape[1]
    l_len = p_len + t_len
    n_kvcols = 6 * N_KV * HEAD_DIM
    proj = x @ w_in
    q = proj[..., :D_ATT].reshape(bsz, t_len, N_KV, HPG, HEAD_DIM) * (HEAD_DIM ** -0.5)
    kv = proj[..., D_ATT:D_ATT + n_kvcols].reshape(bsz, t_len, 6, N_KV, HEAD_DIM)
    gates = jax.nn.sigmoid(proj[..., D_ATT + n_kvcols:D_ATT + n_kvcols + 3 * N_HEADS]).reshape(bsz, t_len, 3, N_KV, HPG)
    z = proj[..., D_ATT + n_kvcols + 3 * N_HEADS:]
    rows = kv[:, :, :4]
    win_new = kv[:, :, 4:]
    dt = rows.dtype
    full = jnp.concatenate([past_rows.astype(dt), rows], axis=1)

    n_cmp = l_len // CMP_BLOCK
    blocks = full[:, :n_cmp * CMP_BLOCK, :2].reshape(bsz, n_cmp, CMP_BLOCK, 2, N_KV, HEAD_DIM)
    kv_c = jnp.einsum('bnjcgd,cjgd->bncgd', blocks, w_cmp)
    k_c, v_c = kv_c[:, :, 0], kv_c[:, :, 1]
    cmp_end = jnp.arange(n_cmp) * CMP_BLOCK + (CMP_BLOCK - 1)

    n_slc = -(-l_len // SLC_BLOCK)
    slc = jnp.pad(full[:, :, 2:], ((0, 0), (0, n_slc * SLC_BLOCK - l_len), (0, 0), (0, 0), (0, 0)))
    slc = slc.reshape(bsz, n_slc, SLC_BLOCK, 2, N_KV, HEAD_DIM)
    k_sl = slc[:, :, :, 0].transpose(0, 3, 1, 2, 4)
    v_sl = slc[:, :, :, 1].transpose(0, 3, 1, 2, 4)
    n_sel = min(TOP_N, n_slc)

    n_pad = WINDOW - win_buf.shape[1]
    win = jnp.concatenate([jnp.zeros((bsz, n_pad, 2, N_KV, HEAD_DIM), dt), win_buf.astype(dt), win_new], axis=1)

    qb_len = Q_BLOCK if t_len % Q_BLOCK == 0 else t_len
    n_qb = t_len // qb_len
    slopes = alibi_slopes()
    blk = jnp.arange(n_slc)
    b_ix = jnp.arange(bsz)[:, None, None, None]
    g_ix = jnp.arange(N_KV)[None, :, None, None]

    def attend_block(args):
        qb, gb, t0 = args
        t = p_len + t0 + jnp.arange(qb_len)
        tf = t.astype(jnp.float32)
        s_c = jnp.einsum('bqghd,bngd->bghqn', qb, k_c, preferred_element_type=jnp.float32)
        s_c = s_c - slopes[:, :, None, None] * (tf[:, None] - cmp_end[None, :].astype(jnp.float32))
        p_c = masked_softmax(s_c, cmp_end[None, :] <= t[:, None])
        o_c = jnp.einsum('bghqn,bngd->bqghd', p_c.astype(dt), v_c)
        imp = jnp.pad(p_c.sum(axis=2), ((0, 0), (0, 0), (0, 0), (0, n_slc * CMP_PER_SLC - n_cmp)))
        imp = imp.reshape(bsz, N_KV, qb_len, n_slc, CMP_PER_SLC).sum(-1)
        cur = (t // SLC_BLOCK)[:, None]
        forced = (blk == 0) | ((blk <= cur) & (blk > cur - N_LOCAL))
        score = jnp.where(forced, FORCE, jnp.where(blk <= cur, imp, NEG_INF))
        _, idx = lax.top_k(score, n_sel)
        k_s = k_sl[b_ix, g_ix, idx].reshape(bsz, N_KV, qb_len, n_sel * SLC_BLOCK, HEAD_DIM)
        v_s = v_sl[b_ix, g_ix, idx].reshape(bsz, N_KV, qb_len, n_sel * SLC_BLOCK, HEAD_DIM)
        pos_s = (idx[..., None] * SLC_BLOCK + jnp.arange(SLC_BLOCK)).reshape(bsz, N_KV, qb_len, n_sel * SLC_BLOCK)
        s_s = jnp.einsum('bqghd,bgqsd->bghqs', qb, k_s, preferred_element_type=jnp.float32)
        s_s = s_s - slopes[None, :, :, None, None] * (tf[:, None] - pos_s.astype(jnp.float32))[:, :, None]
        p_s = masked_softmax(s_s, (pos_s <= t[:, None])[:, :, None])
        o_s = jnp.einsum('bghqs,bgqsd->bqghd', p_s.astype(dt), v_s)
        kw = lax.dynamic_slice_in_dim(win, t0, WINDOW + qb_len, axis=1)
        pos_w = p_len - WINDOW + t0 + jnp.arange(WINDOW + qb_len)
        dist = t[:, None] - pos_w[None, :]
        valid_w = (pos_w[None, :] >= 0) & (dist >= 0) & (dist < WINDOW)
        s_w = jnp.einsum('bqghd,bsgd->bghqs', qb, kw[:, :, 0], preferred_element_type=jnp.float32)
        s_w = s_w - slopes[:, :, None, None] * dist.astype(jnp.float32)
        p_w = masked_softmax(s_w, valid_w)
        o_w = jnp.einsum('bghqs,bsgd->bqghd', p_w.astype(dt), kw[:, :, 1])
        g = gb[..., None].astype(dt)
        return g[:, :, 0] * o_c + g[:, :, 1] * o_s + g[:, :, 2] * o_w

    q_blocks = q.reshape(bsz, n_qb, qb_len, N_KV, HPG, HEAD_DIM).transpose(1, 0, 2, 3, 4, 5)
    g_blocks = gates.reshape(bsz, n_qb, qb_len, 3, N_KV, HPG).transpose(1, 0, 2, 3, 4, 5)
    starts = jnp.arange(n_qb, dtype=jnp.int32) * qb_len
    o = lax.map(attend_block, (q_blocks, g_blocks, starts))
    o = o.transpose(1, 0, 2, 3, 4, 5).reshape(bsz, t_len, D_ATT)
    y = (o * jax.nn.silu(z)) @ w_out

    all_win = jnp.concatenate([win_buf.astype(dt), win_new], axis=1)
    keep = win_buf.shape[1] if p_len > 0 else min(WINDOW, t_len)
    return y, rows, all_win[:, all_win.shape[1] - keep:]


def setup_inputs(seed: int = 0) -> dict:
    key = jax.random.key(seed)
    ks = jax.random.split(key, 32)
    f32 = jnp.float32

    def nrm(k, shape, scale):
        return jax.random.normal(k, shape, f32) * scale

    n_pages = PAST_LEN // PAGE_SIZE
    n_used = DEC_BATCH * n_pages
    n_pool = n_used + max(1, n_used // 4)
    win_rows = min(WINDOW, PAST_LEN)
    page_table = jax.random.permutation(ks[7], n_pool)[:n_used].reshape(DEC_BATCH, n_pages).astype(jnp.int32)
    a0 = jax.random.uniform(ks[23], (N_LAYERS_C, D_RNN), f32, 0.9, 0.999)
    s0 = a0 ** (1.0 / RG_C)
    d_bin = D_ATT + 6 * N_KV * HEAD_DIM + 3 * N_HEADS + D_ATT
    return {
        'x_prompt': nrm(ks[0], (BATCH, SEQ, D_MODEL), 1.0),
        'x_sample': nrm(ks[1], (DEC_BATCH, DEC_SEQ, D_MODEL), 1.0),
        'cache_nsa_kv': nrm(ks[2], (N_LAYERS_B, n_pool, PAGE_SIZE, 4, N_KV, HEAD_DIM), 1.0),
        'cache_nsa_win': nrm(ks[3], (N_LAYERS_B, DEC_BATCH, win_rows, 2, N_KV, HEAD_DIM), 1.0),
        'state_conv_a': nrm(ks[4], (N_LAYERS_A, DEC_BATCH, CONV_A_WIDTH - 1, D_CONV), 1.0),
        'state_lru_h': nrm(ks[5], (N_LAYERS_C, DEC_BATCH, D_RNN), 0.5),
        'state_lru_conv': nrm(ks[6], (N_LAYERS_C, DEC_BATCH, CONV_C_WIDTH - 1, D_RNN), 1.0),
        'page_table': page_table,
        'ln_g': 1.0 + nrm(ks[8], (DEPTH, D_MODEL), 0.02),
        'ln_b': nrm(ks[9], (DEPTH, D_MODEL), 0.02),
        'a_w_in': nrm(ks[10], (N_LAYERS_A, D_MODEL, 4 * D_CONV), D_MODEL ** -0.5),
        'a_conv_w': nrm(ks[11], (N_LAYERS_A, CONV_A_WIDTH, D_CONV), CONV_A_WIDTH ** -0.5),
        'a_w_out': nrm(ks[12], (N_LAYERS_A, D_CONV, D_MODEL), BETA * D_CONV ** -0.5),
        'b_w_in': nrm(ks[13], (N_LAYERS_B, D_MODEL, d_bin), D_MODEL ** -0.5),
        'b_w_cmp': (1.0 + nrm(ks[14], (N_LAYERS_B, 2, CMP_BLOCK, N_KV, HEAD_DIM), 0.1)) * CMP_BLOCK ** -0.5,
        'b_w_out': nrm(ks[15], (N_LAYERS_B, D_ATT, D_MODEL), BETA * D_ATT ** -0.5),
        'c_w_in': nrm(ks[16], (N_LAYERS_C, D_MODEL, 2 * D_RNN), D_MODEL ** -0.5),
        'c_conv_w': nrm(ks[17], (N_LAYERS_C, CONV_C_WIDTH, D_RNN), CONV_C_WIDTH ** -0.5),
        'c_conv_b': nrm(ks[18], (N_LAYERS_C, D_RNN), 0.02),
        'c_w_a': nrm(ks[19], (N_LAYERS_C, N_RG_BLOCKS, RG_BLOCK, RG_BLOCK), RG_BLOCK ** -0.5),
        'c_b_a': nrm(ks[20], (N_LAYERS_C, D_RNN), 0.02),
        'c_w_x': nrm(ks[21], (N_LAYERS_C, N_RG_BLOCKS, RG_BLOCK, RG_BLOCK), RG_BLOCK ** -0.5),
        'c_b_x': nrm(ks[22], (N_LAYERS_C, D_RNN), 0.02),
        'c_lam': jnp.log(s0) - jnp.log1p(-s0),
        'c_w_out': nrm(ks[24], (N_LAYERS_C, D_RNN, D_MODEL), BETA * D_RNN ** -0.5),
    }


def reference(x_prompt, x_sample, cache_nsa_kv, cache_nsa_win, state_conv_a, state_lru_h, state_lru_conv, page_table,
              ln_g, ln_b, a_w_in, a_conv_w, a_w_out, b_w_in, b_w_cmp, b_w_out,
              c_w_in, c_conv_w, c_conv_b, c_w_a, c_b_a, c_w_x, c_b_x, c_lam, c_w_out):
    yp, ys = x_prompt, x_sample
    bp, bs = x_prompt.shape[0], x_sample.shape[0]
    dt = x_prompt.dtype
    n_pages = page_table.shape[1]
    conv_a_p, conv_a_s = [], []
    rows_p, rows_s, win_p, win_s = [], [], [], []
    h_p, h_s, cc_p, cc_s = [], [], [], []
    for i in range(DEPTH):
        kind, j = i % N_MIXERS, i // N_MIXERS
        if kind == 0:
            mp, sp = short_conv_mixer(yp, jnp.zeros((bp, CONV_A_WIDTH - 1, D_CONV), dt), a_w_in[j], a_conv_w[j], a_w_out[j])
            ms, ss = short_conv_mixer(ys, state_conv_a[j], a_w_in[j], a_conv_w[j], a_w_out[j])
            conv_a_p.append(sp)
            conv_a_s.append(ss)
        elif kind == 1:
            past = cache_nsa_kv[j][page_table].reshape(bs, n_pages * PAGE_SIZE, 4, N_KV, HEAD_DIM)
            mp, rp, wp = nsa_mixer(yp, jnp.zeros((bp, 0, 4, N_KV, HEAD_DIM), dt), jnp.zeros((bp, 0, 2, N_KV, HEAD_DIM), dt),
                                   b_w_in[j], b_w_cmp[j], b_w_out[j])
            ms, rs, ws = nsa_mixer(ys, past, cache_nsa_win[j], b_w_in[j], b_w_cmp[j], b_w_out[j])
            rows_p.append(rp)
            rows_s.append(rs)
            win_p.append(wp)
            win_s.append(ws)
        else:
            mp, hp, cp = rglru_mixer(yp, jnp.zeros((bp, D_RNN), dt), jnp.zeros((bp, CONV_C_WIDTH - 1, D_RNN), dt),
                                     c_w_in[j], c_conv_w[j], c_conv_b[j], c_w_a[j], c_b_a[j], c_w_x[j], c_b_x[j], c_lam[j], c_w_out[j])
            ms, hs, cs = rglru_mixer(ys, state_lru_h[j], state_lru_conv[j],
                                     c_w_in[j], c_conv_w[j], c_conv_b[j], c_w_a[j], c_b_a[j], c_w_x[j], c_b_x[j], c_lam[j], c_w_out[j])
            h_p.append(hp)
            h_s.append(hs)
            cc_p.append(cp)
            cc_s.append(cs)
        yp = layer_norm(ALPHA * yp + mp, ln_g[i], ln_b[i])
        ys = layer_norm(ALPHA * ys + ms, ln_g[i], ln_b[i])
    return (yp, ys, jnp.stack(conv_a_p), jnp.stack(conv_a_s), jnp.stack(rows_p), jnp.stack(rows_s),
            jnp.stack(win_p), jnp.stack(win_s), jnp.stack(h_p), jnp.stack(h_s), jnp.stack(cc_p), jnp.stack(cc_s))
```

```python
import functools

import jax
import jax.numpy as jnp
from jax import lax
from jax.experimental import pallas as pl
from jax.experimental.pallas import tpu as pltpu

F32 = jnp.float32
BF16 = jnp.bfloat16

DEPTH = 4
N_HEADS = 16
N_KV = 2
HPG = N_HEADS // N_KV
HEAD_DIM = 64
CMP_BLOCK = 32
SLC_BLOCK = 64
TOP_N = 16
N_LOCAL = 2
WINDOW = 512
PAGE_SIZE = 128
NEG_INF = -1e30
FORCE = 1e9
RG_C = 8.0
N_RG_BLOCKS = 4
ALPHA = (2 * DEPTH) ** 0.25
LN_EPS = 1e-5

LANES = 128
VMEM_LIMIT = 56 * 1024 * 1024


def _dot(a, b):
    return jnp.dot(a, b, preferred_element_type=F32)


def _dot_nt(a, b):
    return lax.dot_general(a, b, (((1,), (1,)), ((), ())), preferred_element_type=F32)


def _silu(z):
    return z * jax.nn.sigmoid(z)


def _ln_residual(x, y, g, b):
    r = ALPHA * x + y
    mu = jnp.mean(r, axis=-1, keepdims=True)
    c = r - mu
    var = jnp.mean(c * c, axis=-1, keepdims=True)
    return c * lax.rsqrt(var + LN_EPS) * g + b


def _slope(head):
    return 2.0 ** (-8.0 * (head + 1) / N_HEADS)


def _prev_rows(u, carry, row, k):
    n_carry = carry.shape[0]
    out = pltpu.roll(u, k, 0)
    for i in range(k):
        out = jnp.where(row == i, carry[n_carry - k + i:n_carry - k + i + 1, :], out)
    return out


def _conv_layer_kernel(x_ref, buf_ref, win_ref, cw_ref, wout_ref, g_ref, b_ref,
                       y_ref, nbuf_ref, carry_ref, *, stride, col_chunk):
    rows, d = x_ref.shape
    taps = cw_ref.shape[0]
    x = x_ref[...]
    xb = x.astype(BF16)
    if stride == 1:
        @pl.when(pl.program_id(1) == 0)
        def _():
            carry_ref[...] = buf_ref[...]
        row = lax.broadcasted_iota(jnp.int32, (rows, 1), 0)
    acc = jnp.zeros((rows, d), F32)
    for j in range(d // col_chunk):
        lo = j * col_chunk
        sl = slice(lo, lo + col_chunk)
        h = _dot(xb, win_ref[:, lo:lo + col_chunk])
        bg = _dot(xb, win_ref[:, d + lo:d + lo + col_chunk])
        cg = _dot(xb, win_ref[:, 2 * d + lo:2 * d + lo + col_chunk])
        z = _dot(xb, win_ref[:, 3 * d + lo:3 * d + lo + col_chunk])
        u = cg * h
        conv = u * cw_ref[taps - 1:taps, sl]
        if stride == 1:
            carry = carry_ref[:, sl]
            for k in range(1, taps):
                conv = conv + _prev_rows(u, carry, row, k) * cw_ref[taps - 1 - k:taps - k, sl]
            new_buf = u[rows - (taps - 1):rows, :]
            carry_ref[:, sl] = new_buf
        else:
            ext = jnp.concatenate([buf_ref[:, sl], u], axis=0)
            for k in range(1, taps):
                s0 = (taps - 1 - k) * stride
                conv = conv + ext[s0:s0 + rows, :] * cw_ref[taps - 1 - k:taps - k, sl]
            new_buf = ext[rows:rows + (taps - 1) * stride, :]
        nbuf_ref[:, sl] = new_buf
        m = _silu(z) * bg * conv
        acc = acc + _dot(m.astype(BF16), wout_ref[sl, :])
    y_ref[...] = _ln_residual(x, acc, g_ref[...], b_ref[...])


def _conv_layer(x, buf, w_in, conv_w, w_out, ln_g, ln_b, *, stride, row_tile):
    bsz, t_len, d = x.shape
    n_buf = buf.shape[1]
    taps = conv_w.shape[0]
    grid = (bsz, t_len // row_tile)
    kern = functools.partial(_conv_layer_kernel, stride=stride, col_chunk=512)
    return pl.pallas_call(
        kern,
        grid=grid,
        in_specs=[
            pl.BlockSpec((None, row_tile, d), lambda b, t: (b, t, 0)),
            pl.BlockSpec((None, n_buf, d), lambda b, t: (b, 0, 0)),
            pl.BlockSpec((d, 4 * d), lambda b, t: (0, 0)),
            pl.BlockSpec((taps, d), lambda b, t: (0, 0)),
            pl.BlockSpec((d, d), lambda b, t: (0, 0)),
            pl.BlockSpec((1, d), lambda b, t: (0, 0)),
            pl.BlockSpec((1, d), lambda b, t: (0, 0)),
        ],
        out_specs=[
            pl.BlockSpec((None, row_tile, d), lambda b, t: (b, t, 0)),
            pl.BlockSpec((None, n_buf, d), lambda b, t: (b, 0, 0)),
        ],
        out_shape=[
            jax.ShapeDtypeStruct((bsz, t_len, d), F32),
            jax.ShapeDtypeStruct((bsz, n_buf, d), F32),
        ],
        scratch_shapes=[pltpu.VMEM((taps - 1, d), F32)],
        compiler_params=pltpu.CompilerParams(
            dimension_semantics=("arbitrary", "arbitrary"), vmem_limit_bytes=VMEM_LIMIT),
        name="conv_layer",
    )(x, buf, w_in, conv_w, w_out, ln_g, ln_b)


def _scan_rows(a, b, row):
    rows = a.shape[0]
    s = 1
    while s < rows:
        live = row >= s
        a_sh = jnp.where(live, pltpu.roll(a, s, 0), 1.0)
        b_sh = jnp.where(live, pltpu.roll(b, s, 0), 0.0)
        b = a * b_sh + b
        a = a * a_sh
        s *= 2
    return a, b


def _lru_layer_kernel(x_ref, h0_ref, buf_ref, win_ref, cw_ref, cb_ref, wa_ref, ba_ref, wx_ref, bx_ref,
                      lam_ref, wout_ref, g_ref, b_ref,
                      y_ref, hlast_ref, nbuf_ref, carry_ref, hcarry_ref, *, stride):
    rows, d = x_ref.shape
    taps = cw_ref.shape[0]
    n_blocks, blk, _ = wa_ref.shape
    x = x_ref[...]
    xb = x.astype(BF16)
    if stride == 1:
        @pl.when(pl.program_id(1) == 0)
        def _():
            carry_ref[...] = buf_ref[...]
            hcarry_ref[...] = h0_ref[...]
        row = lax.broadcasted_iota(jnp.int32, (rows, 1), 0)
    acc = jnp.zeros((rows, d), F32)
    for n in range(n_blocks):
        lo = n * blk
        sl = slice(lo, lo + blk)
        u = _dot(xb, win_ref[:, lo:lo + blk])
        z = _dot(xb, win_ref[:, d + lo:d + lo + blk])
        uc = u * cw_ref[taps - 1:taps, sl] + cb_ref[:, sl]
        if stride == 1:
            carry = carry_ref[:, sl]
            for k in range(1, taps):
                uc = uc + _prev_rows(u, carry, row, k) * cw_ref[taps - 1 - k:taps - k, sl]
            new_buf = u[rows - (taps - 1):rows, :]
            carry_ref[:, sl] = new_buf
        else:
            ext = jnp.concatenate([buf_ref[:, sl], u], axis=0)
            for k in range(1, taps):
                s0 = (taps - 1 - k) * stride
                uc = uc + ext[s0:s0 + rows, :] * cw_ref[taps - 1 - k:taps - k, sl]
            new_buf = ext[rows:rows + (taps - 1) * stride, :]
        nbuf_ref[:, sl] = new_buf
        ucb = uc.astype(BF16)
        r = jax.nn.sigmoid(_dot(ucb, wa_ref[n]) + ba_ref[:, sl])
        i = jax.nn.sigmoid(_dot(ucb, wx_ref[n]) + bx_ref[:, sl])
        nl = -lam_ref[:, sl]
        softplus = jnp.maximum(nl, 0.0) + jnp.log1p(jnp.exp(-jnp.abs(nl)))
        log_a = (-RG_C * softplus) * r
        a = jnp.exp(log_a)
        bb = jnp.sqrt(-jnp.tanh(log_a) * (a * a + 1.0)) * (i * uc)
        if stride == 1:
            a_cum, h_zero = _scan_rows(a, bb, row)
            h = a_cum * hcarry_ref[:, sl] + h_zero
            h_last = h[rows - 1:rows, :]
            hcarry_ref[:, sl] = h_last
        else:
            h_t = h0_ref[:, sl]
            hs = []
            for t in range(rows // stride):
                rs = slice(t * stride, (t + 1) * stride)
                h_t = a[rs, :] * h_t + bb[rs, :]
                hs.append(h_t)
            h = jnp.concatenate(hs, axis=0)
            h_last = h_t
        hlast_ref[:, sl] = h_last
        acc = acc + _dot((_silu(z) * h).astype(BF16), wout_ref[sl, :])
    y_ref[...] = _ln_residual(x, acc, g_ref[...], b_ref[...])


def _lru_layer(x, h0, buf, w_in, conv_w, conv_b, w_a, b_a, w_x, b_x, lam, w_out, ln_g, ln_b, *, stride, row_tile):
    bsz, t_len, d = x.shape
    n_buf = buf.shape[1]
    n_h = h0.shape[1]
    taps = conv_w.shape[0]
    nb, blk, _ = w_a.shape
    grid = (bsz, t_len // row_tile)
    const2 = lambda b, t: (0, 0)
    kern = functools.partial(_lru_layer_kernel, stride=stride)
    return pl.pallas_call(
        kern,
        grid=grid,
        in_specs=[
            pl.BlockSpec((None, row_tile, d), lambda b, t: (b, t, 0)),
            pl.BlockSpec((None, n_h, d), lambda b, t: (b, 0, 0)),
            pl.BlockSpec((None, n_buf, d), lambda b, t: (b, 0, 0)),
            pl.BlockSpec((d, 2 * d), const2),
            pl.BlockSpec((taps, d), const2),
            pl.BlockSpec((1, d), const2),
            pl.BlockSpec((nb, blk, blk), lambda b, t: (0, 0, 0)),
            pl.BlockSpec((1, d), const2),
            pl.BlockSpec((nb, blk, blk), lambda b, t: (0, 0, 0)),
            pl.BlockSpec((1, d), const2),
            pl.BlockSpec((1, d), const2),
            pl.BlockSpec((d, d), const2),
            pl.BlockSpec((1, d), const2),
            pl.BlockSpec((1, d), const2),
        ],
        out_specs=[
            pl.BlockSpec((None, row_tile, d), lambda b, t: (b, t, 0)),
            pl.BlockSpec((None, n_h, d), lambda b, t: (b, 0, 0)),
            pl.BlockSpec((None, n_buf, d), lambda b, t: (b, 0, 0)),
        ],
        out_shape=[
            jax.ShapeDtypeStruct((bsz, t_len, d), F32),
            jax.ShapeDtypeStruct((bsz, n_h, d), F32),
            jax.ShapeDtypeStruct((bsz, n_buf, d), F32),
        ],
        scratch_shapes=[pltpu.VMEM((taps - 1, d), F32), pltpu.VMEM((1, d), F32)],
        compiler_params=pltpu.CompilerParams(
            dimension_semantics=("arbitrary", "arbitrary"), vmem_limit_bytes=VMEM_LIMIT),
        name="lru_layer",
    )(x, h0, buf, w_in, conv_w, conv_b, w_a, b_a, w_x, b_x, lam, w_out, ln_g, ln_b)


D_ATT = N_HEADS * HEAD_DIM
N_ROWCOLS = 4 * N_KV * HEAD_DIM
N_WINCOLS = 2 * N_KV * HEAD_DIM
N_GATES = 3 * N_HEADS


def _nsa_proj_kernel(x_ref, w_ref, q_ref, rows_ref, win_ref, sz_ref, gates_ref):
    xb = x_ref[...].astype(BF16)
    c0 = 0
    q_ref[...] = (_dot(xb, w_ref[:, c0:c0 + D_ATT]) * (HEAD_DIM ** -0.5)).astype(BF16)
    c0 += D_ATT
    rows_ref[...] = _dot(xb, w_ref[:, c0:c0 + N_ROWCOLS])
    c0 += N_ROWCOLS
    win_ref[...] = _dot(xb, w_ref[:, c0:c0 + N_WINCOLS])
    c0 += N_WINCOLS
    sz_ref[...] = _silu(_dot(xb, w_ref[:, c0:c0 + D_ATT]))
    c0 += D_ATT
    gates_ref[...] = jax.nn.sigmoid(_dot(xb, w_ref[:, c0:c0 + LANES]))


def _nsa_proj(x2, w_all, *, row_tile):
    n, d = x2.shape
    n_cols = w_all.shape[1]
    widths = (D_ATT, N_ROWCOLS, N_WINCOLS, D_ATT, LANES)
    dtypes = (BF16, F32, F32, F32, F32)
    return pl.pallas_call(
        _nsa_proj_kernel,
        grid=(n // row_tile,),
        in_specs=[pl.BlockSpec((row_tile, d), lambda i: (i, 0)),
                  pl.BlockSpec((d, n_cols), lambda i: (0, 0))],
        out_specs=[pl.BlockSpec((row_tile, w), lambda i: (i, 0)) for w in widths],
        out_shape=[jax.ShapeDtypeStruct((n, w), dt) for w, dt in zip(widths, dtypes)],
        compiler_params=pltpu.CompilerParams(
            dimension_semantics=("arbitrary",), vmem_limit_bytes=VMEM_LIMIT),
        name="nsa_proj",
    )(x2, w_all)


def _nsa_out_kernel(x_ref, o_ref, sz_ref, w_ref, g_ref, b_ref, y_ref):
    m = (o_ref[...] * sz_ref[...]).astype(BF16)
    y_ref[...] = _ln_residual(x_ref[...], _dot(m, w_ref[...]), g_ref[...], b_ref[...])


def _nsa_out(x2, o2, sz2, w_out, ln_g, ln_b, *, row_tile):
    n, d = x2.shape
    row = lambda i: (i, 0)
    const = lambda i: (0, 0)
    return pl.pallas_call(
        _nsa_out_kernel,
        grid=(n // row_tile,),
        in_specs=[pl.BlockSpec((row_tile, d), row), pl.BlockSpec((row_tile, d), row),
                  pl.BlockSpec((row_tile, d), row), pl.BlockSpec((d, d), const),
                  pl.BlockSpec((1, d), const), pl.BlockSpec((1, d), const)],
        out_specs=pl.BlockSpec((row_tile, d), row),
        out_shape=jax.ShapeDtypeStruct((n, d), F32),
        compiler_params=pltpu.CompilerParams(
            dimension_semantics=("arbitrary",), vmem_limit_bytes=VMEM_LIMIT),
        name="nsa_out",
    )(x2, o2, sz2, w_out, ln_g, ln_b)


def _dup_half(slab, g, lo):
    rolled = pltpu.roll(slab, HEAD_DIM, 1)
    return jnp.where(lo, slab, rolled) if g == 0 else jnp.where(lo, rolled, slab)


def _select_blocks(score, nl, real, n_slc, n_sel):
    rank = jnp.zeros(score.shape, jnp.int32)
    for i in range(n_slc):
        col = score[:, 2 * i:2 * i + 1]
        beats = (col > score) | ((col == score) & (nl > 2 * i))
        rank = rank + beats.astype(jnp.int32)
    return real & (rank < n_sel)


def _block_scores(imp2, nl, tq, n_slc):
    blk = nl >> 1
    real = ((nl & 1) == 0) & (blk < n_slc)
    cur = tq >> 6
    forced = (blk == 0) | ((blk <= cur) & (blk > cur - N_LOCAL))
    score = jnp.where(forced, FORCE, jnp.where(blk <= cur, imp2, NEG_INF))
    return jnp.where(real, score, -jnp.inf), real


def _nsa_prompt_kernel(q_ref, rows_ref, win_ref, gates_ref, wc_ref, e_ref, o_ref,
                       ks2, vslo, vshi, kw2, vwlo, vwhi, kc2, vclo, vchi, mb_ref, *, kv_chunk):
    tq_len = q_ref.shape[0]
    t_len = rows_ref.shape[0]
    n_cmp = t_len // CMP_BLOCK
    n_slc = -(-t_len // SLC_BLOCK)
    n_sel = min(TOP_N, n_slc)
    n_kv_chunks = t_len // kv_chunk
    w_slab = WINDOW + tq_len
    qi = pl.program_id(1)
    t0 = qi * tq_len
    lo = lax.broadcasted_iota(jnp.int32, (1, LANES), 1) < HEAD_DIM

    @pl.when(qi == 0)
    def _build():
        lo_t = lax.broadcasted_iota(jnp.int32, (t_len, LANES), 1) < HEAD_DIM
        lo_c = lax.broadcasted_iota(jnp.int32, (LANES, LANES), 1) < HEAD_DIM
        cm = rows_ref[:, 0:2 * LANES]
        kvc = jnp.sum(cm.reshape(n_cmp, CMP_BLOCK, 2 * LANES) * wc_ref[...][None], axis=1)
        kvc = jnp.concatenate([kvc, jnp.zeros((LANES - n_cmp, 2 * LANES), F32)], axis=0)
        for g in range(N_KV):
            ks2[g] = _dup_half(rows_ref[:, 2 * LANES:3 * LANES], g, lo_t).astype(BF16)
            vd = _dup_half(rows_ref[:, 3 * LANES:4 * LANES], g, lo_t)
            vslo[g] = jnp.where(lo_t, vd, 0.0).astype(BF16)
            vshi[g] = jnp.where(lo_t, 0.0, vd).astype(BF16)
            kw2[g] = _dup_half(win_ref[:, 0:LANES], g, lo_t).astype(BF16)
            vd = _dup_half(win_ref[:, LANES:2 * LANES], g, lo_t)
            vwlo[g] = jnp.where(lo_t, vd, 0.0).astype(BF16)
            vwhi[g] = jnp.where(lo_t, 0.0, vd).astype(BF16)
            kc2[g] = _dup_half(kvc[:, 0:LANES], g, lo_c).astype(BF16)
            vd = _dup_half(kvc[:, LANES:2 * LANES], g, lo_c)
            vclo[g] = jnp.where(lo_c, vd, 0.0).astype(BF16)
            vchi[g] = jnp.where(lo_c, 0.0, vd).astype(BF16)

    row = lax.broadcasted_iota(jnp.int32, (tq_len, 1), 0)
    tq = t0 + row
    nl = lax.broadcasted_iota(jnp.int32, (tq_len, LANES), 1)
    cmp_end = nl * CMP_BLOCK + (CMP_BLOCK - 1)
    valid_c = (cmp_end <= tq) & (nl < n_cmp)
    dist_c = (tq - cmp_end).astype(F32)

    w_start = pl.multiple_of(jnp.maximum(t0 - WINDOW, 0), LANES)
    dist_w = (t0 - w_start) + row - lax.broadcasted_iota(jnp.int32, (tq_len, w_slab), 1)
    bias_w = jnp.where((dist_w >= 0) & (dist_w < WINDOW), 0.0, NEG_INF)
    dist_wf = dist_w.astype(F32)

    rl = row - lax.broadcasted_iota(jnp.int32, (tq_len, kv_chunk), 1)
    n_live = (t0 + tq_len - 1) // kv_chunk + 1

    for g in range(N_KV):
        def heads(j):
            q2 = q_ref[:, g * HPG * HEAD_DIM + j * LANES:g * HPG * HEAD_DIM + (j + 1) * LANES]
            zero = jnp.zeros_like(q2)
            return ((jnp.where(lo, q2, zero), _slope(g * HPG + 2 * j)),
                    (jnp.where(lo, zero, q2), _slope(g * HPG + 2 * j + 1)))

        def gate_pair(c, j):
            ca = c * N_HEADS + g * HPG + 2 * j
            return jnp.where(lo, gates_ref[:, ca:ca + 1], gates_ref[:, ca + 1:ca + 2])

        imp = jnp.zeros((tq_len, LANES), F32)
        o_cmp = []
        for j in range(HPG // 2):
            (qa, sa), (qb, sb) = heads(j)
            oc = jnp.zeros((tq_len, LANES), F32)
            for qh, slope, v_ref in ((qa, sa, vclo), (qb, sb, vchi)):
                s = _dot_nt(qh, kc2[g]) - slope * dist_c
                s = jnp.where(valid_c, s, NEG_INF)
                e = jnp.exp(s - jnp.max(s, axis=-1, keepdims=True))
                p = jnp.where(valid_c, e, 0.0) / jnp.sum(e, axis=-1, keepdims=True)
                imp = imp + p
                oc = oc + _dot(p.astype(BF16), v_ref[g])
            o_cmp.append(oc)

        imp2 = imp + pltpu.roll(imp, LANES - 1, 1)
        score, real = _block_scores(imp2, nl, tq, n_slc)
        sel = _select_blocks(score, nl, real, n_slc, n_sel)
        selb = jnp.where(sel, 1.0, 0.0).astype(BF16)
        for c in range(n_kv_chunks):
            hit = _dot(selb, e_ref[:, c * kv_chunk:(c + 1) * kv_chunk])
            causal = (rl + (t0 - c * kv_chunk)) >= 0
            mb_ref[c] = jnp.where((hit > 0.5) & causal, 0.0, NEG_INF)

        for j in range(HPG // 2):
            (qa, sa), (qb, sb) = heads(j)

            def body(c, carry):
                m_a, l_a, m_b, l_b, acc = carry
                off = pl.multiple_of(c * kv_chunk, kv_chunk)
                k = ks2[g, pl.ds(off, kv_chunk), :]
                dist = (rl + (t0 - off)).astype(F32)
                mb = mb_ref[c]
                s_a = _dot_nt(qa, k) - sa * dist + mb
                s_b = _dot_nt(qb, k) - sb * dist + mb
                m_a2 = jnp.maximum(m_a, jnp.max(s_a, axis=-1, keepdims=True))
                m_b2 = jnp.maximum(m_b, jnp.max(s_b, axis=-1, keepdims=True))
                al_a = jnp.exp(m_a - m_a2)
                al_b = jnp.exp(m_b - m_b2)
                p_a = jnp.exp(s_a - m_a2)
                p_b = jnp.exp(s_b - m_b2)
                l_a = al_a * l_a + jnp.sum(p_a, axis=-1, keepdims=True)
                l_b = al_b * l_b + jnp.sum(p_b, axis=-1, keepdims=True)
                acc = (acc * jnp.where(lo, al_a, al_b)
                       + _dot(p_a.astype(BF16), vslo[g, pl.ds(off, kv_chunk), :])
                       + _dot(p_b.astype(BF16), vshi[g, pl.ds(off, kv_chunk), :]))
                return m_a2, l_a, m_b2, l_b, acc

            neg = jnp.full((tq_len, 1), NEG_INF, F32)
            zero1 = jnp.zeros((tq_len, 1), F32)
            _, l_a, _, l_b, acc = lax.fori_loop(
                0, n_live, body, (neg, zero1, neg, zero1, jnp.zeros((tq_len, LANES), F32)))
            o_sel = acc / jnp.where(lo, l_a, l_b)

            kw = kw2[g, pl.ds(w_start, w_slab), :]
            s_a = _dot_nt(qa, kw) - sa * dist_wf + bias_w
            s_b = _dot_nt(qb, kw) - sb * dist_wf + bias_w
            p_a = jnp.exp(s_a - jnp.max(s_a, axis=-1, keepdims=True))
            p_b = jnp.exp(s_b - jnp.max(s_b, axis=-1, keepdims=True))
            l_w = jnp.where(lo, jnp.sum(p_a, axis=-1, keepdims=True), jnp.sum(p_b, axis=-1, keepdims=True))
            o_win = (_dot(p_a.astype(BF16), vwlo[g, pl.ds(w_start, w_slab), :])
                     + _dot(p_b.astype(BF16), vwhi[g, pl.ds(w_start, w_slab), :])) / l_w

            out = gate_pair(0, j) * o_cmp[j] + gate_pair(1, j) * o_sel + gate_pair(2, j) * o_win
            c0 = g * HPG * HEAD_DIM + j * LANES
            o_ref[:, c0:c0 + LANES] = out


def _block_expand_matrix(n_rows, n_keys):
    n = jnp.arange(n_rows, dtype=jnp.int32)[:, None]
    p = jnp.arange(n_keys, dtype=jnp.int32)[None, :]
    return (n == 2 * (p // SLC_BLOCK)).astype(BF16)


def _nsa_prompt_attention(q, rows, win, gates, wc, *, q_tile=128, kv_chunk=512):
    bsz, t_len, _ = q.shape
    e_mat = _block_expand_matrix(LANES, t_len)
    kern = functools.partial(_nsa_prompt_kernel, kv_chunk=kv_chunk)
    kv_scratch = pltpu.VMEM((N_KV, t_len, LANES), BF16)
    c_scratch = pltpu.VMEM((N_KV, LANES, LANES), BF16)
    return pl.pallas_call(
        kern,
        grid=(bsz, t_len // q_tile),
        in_specs=[
            pl.BlockSpec((None, q_tile, D_ATT), lambda b, i: (b, i, 0)),
            pl.BlockSpec((None, t_len, N_ROWCOLS), lambda b, i: (b, 0, 0)),
            pl.BlockSpec((None, t_len, N_WINCOLS), lambda b, i: (b, 0, 0)),
            pl.BlockSpec((None, q_tile, LANES), lambda b, i: (b, i, 0)),
            pl.BlockSpec((CMP_BLOCK, 2 * LANES), lambda b, i: (0, 0)),
            pl.BlockSpec((LANES, t_len), lambda b, i: (0, 0)),
        ],
        out_specs=pl.BlockSpec((None, q_tile, D_ATT), lambda b, i: (b, i, 0)),
        out_shape=jax.ShapeDtypeStruct((bsz, t_len, D_ATT), F32),
        scratch_shapes=[kv_scratch] * 6 + [c_scratch] * 3
        + [pltpu.VMEM((t_len // kv_chunk, q_tile, kv_chunk), F32)],
        compiler_params=pltpu.CompilerParams(
            dimension_semantics=("arbitrary", "arbitrary"), vmem_limit_bytes=VMEM_LIMIT),
        name="nsa_prompt_attention",
    )(q, rows, win, gates, wc, e_mat)


def _nsa_sample_kernel(pt_ref, *refs, pages_per_step, n_steps, n_new):
    page_refs = refs[:pages_per_step]
    (qbd_ref, new_rows_ref, new_win_ref, winbuf_ref, gates_ref, wc_ref, e_ref, o_ref,
     kslab, vslab, kcs, vcs, kwslab, vwslab) = refs[pages_per_step:]
    del pt_ref
    step = pl.program_id(1)
    p_len = n_steps * pages_per_step * PAGE_SIZE
    n_keys = kslab.shape[0]
    n_cmp = p_len // CMP_BLOCK
    n_slc = -(-(p_len + n_new) // SLC_BLOCK)
    n_sel = min(TOP_N, n_slc)
    cmp_per_step = pages_per_step * PAGE_SIZE // CMP_BLOCK
    n_rows = N_KV * n_new * HPG
    w_keys = kwslab.shape[0]
    tail = n_keys - p_len
    new_pad = new_rows_ref.shape[0]

    cm = jnp.concatenate([r[:, 0:2 * LANES] for r in page_refs], axis=0)
    kvc = jnp.sum(cm.reshape(cmp_per_step, CMP_BLOCK, 2 * LANES) * wc_ref[...][None], axis=1)
    c_off = pl.multiple_of(step * cmp_per_step, cmp_per_step)
    kcs[pl.ds(c_off, cmp_per_step), :] = kvc[:, 0:LANES].astype(BF16)
    vcs[pl.ds(c_off, cmp_per_step), :] = kvc[:, LANES:2 * LANES].astype(BF16)
    for k, r in enumerate(page_refs):
        r0 = pl.multiple_of((step * pages_per_step + k) * PAGE_SIZE, PAGE_SIZE)
        kslab[pl.ds(r0, PAGE_SIZE), :] = r[:, 2 * LANES:3 * LANES].astype(BF16)
        vslab[pl.ds(r0, PAGE_SIZE), :] = r[:, 3 * LANES:4 * LANES].astype(BF16)

    @pl.when(step == n_steps - 1)
    def _attend():
        kslab[p_len:n_keys, :] = jnp.zeros((tail, LANES), BF16)
        vslab[p_len:n_keys, :] = jnp.zeros((tail, LANES), BF16)
        kslab[p_len:p_len + new_pad, :] = new_rows_ref[:, 2 * LANES:3 * LANES].astype(BF16)
        vslab[p_len:p_len + new_pad, :] = new_rows_ref[:, 3 * LANES:4 * LANES].astype(BF16)
        n_wb = winbuf_ref.shape[0]
        kwslab[0:n_wb, :] = winbuf_ref[:, 0:LANES].astype(BF16)
        vwslab[0:n_wb, :] = winbuf_ref[:, LANES:2 * LANES].astype(BF16)
        kwslab[n_wb:w_keys, :] = jnp.zeros((w_keys - n_wb, LANES), BF16)
        vwslab[n_wb:w_keys, :] = jnp.zeros((w_keys - n_wb, LANES), BF16)
        kwslab[n_wb:n_wb + new_pad, :] = new_win_ref[:, 0:LANES].astype(BF16)
        vwslab[n_wb:n_wb + new_pad, :] = new_win_ref[:, LANES:2 * LANES].astype(BF16)

        qbd = qbd_ref[...]
        r = lax.broadcasted_iota(jnp.int32, (n_rows, 1), 0)
        head = (r // (n_new * HPG)) * HPG + (r % HPG)
        slope = jnp.exp2((head + 1).astype(F32) * (-8.0 / N_HEADS))
        tq = p_len + (r // HPG) % n_new

        def attend(k_ref, v_ref, dist, bias):
            s = _dot_nt(qbd, k_ref[...]) - slope * dist.astype(F32) + bias
            e = jnp.exp(s - jnp.max(s, axis=-1, keepdims=True))
            return _dot(e.astype(BF16), v_ref[...]) / jnp.sum(e, axis=-1, keepdims=True)

        nl = lax.broadcasted_iota(jnp.int32, (n_rows, n_cmp), 1)
        cmp_end = nl * CMP_BLOCK + (CMP_BLOCK - 1)
        valid_c = cmp_end <= tq
        s = _dot_nt(qbd, kcs[...]) - slope * (tq - cmp_end).astype(F32)
        s = jnp.where(valid_c, s, NEG_INF)
        e = jnp.exp(s - jnp.max(s, axis=-1, keepdims=True))
        p = jnp.where(valid_c, e, 0.0) / jnp.sum(e, axis=-1, keepdims=True)
        o_cmp = _dot(p.astype(BF16), vcs[...])

        n_gt = N_KV * n_new
        imp = jnp.sum(p.reshape(n_gt, HPG, n_cmp), axis=1)
        imp2 = imp + pltpu.roll(imp, n_cmp - 1, 1)
        w_sel = e_ref.shape[0]
        imp2 = jnp.concatenate([imp2, jnp.zeros((n_gt, w_sel - n_cmp), F32)], axis=1)
        nl8 = lax.broadcasted_iota(jnp.int32, (n_gt, w_sel), 1)
        r8 = lax.broadcasted_iota(jnp.int32, (n_gt, 1), 0)
        score, real = _block_scores(imp2, nl8, p_len + r8 % n_new, n_slc)
        sel = _select_blocks(score, nl8, real, n_slc, n_sel)
        selb = jnp.where(sel, 1.0, 0.0)
        selb = jnp.broadcast_to(selb[:, None, :], (n_gt, HPG, w_sel)).reshape(n_rows, w_sel)
        hit = _dot(selb.astype(BF16), e_ref[...])
        dist = tq - lax.broadcasted_iota(jnp.int32, (n_rows, n_keys), 1)
        o_sel = attend(kslab, vslab, dist, jnp.where((hit > 0.5) & (dist >= 0), 0.0, NEG_INF))

        dist_w = tq - (p_len - n_wb) - lax.broadcasted_iota(jnp.int32, (n_rows, w_keys), 1)
        o_win = attend(kwslab, vwslab, dist_w,
                       jnp.where((dist_w >= 0) & (dist_w < WINDOW), 0.0, NEG_INF))

        o_ref[...] = (gates_ref[:, 0:1] * o_cmp + gates_ref[:, 1:2] * o_sel + gates_ref[:, 2:3] * o_win)


def _nsa_sample_attention(page_table, cache, qbd, new_rows, new_win, winbuf, gates, wc, *, n_new,
                          pages_per_step=16):
    bsz, n_pages = page_table.shape
    n_steps = n_pages // pages_per_step
    p_len = n_pages * PAGE_SIZE
    n_rows = qbd.shape[1]
    new_pad = new_rows.shape[1]
    n_wb = winbuf.shape[1]
    n_keys = p_len + LANES
    w_keys = n_wb + LANES
    n_cmp = p_len // CMP_BLOCK
    n_slc = -(-(p_len + n_new) // SLC_BLOCK)
    w_sel = -(-2 * n_slc // LANES) * LANES
    e_mat = _block_expand_matrix(w_sel, n_keys)
    kern = functools.partial(_nsa_sample_kernel, pages_per_step=pages_per_step, n_steps=n_steps, n_new=n_new)

    def page_spec(k):
        return pl.BlockSpec((None, PAGE_SIZE, N_ROWCOLS),
                            lambda b, s, pt: (pt[b, s * pages_per_step + k], 0, 0))

    per_b = lambda b, s, pt: (b, 0, 0)
    const = lambda b, s, pt: (0, 0)
    grid_spec = pltpu.PrefetchScalarGridSpec(
        num_scalar_prefetch=1,
        grid=(bsz, n_steps),
        in_specs=[page_spec(k) for k in range(pages_per_step)] + [
            pl.BlockSpec((None, n_rows, LANES), per_b),
            pl.BlockSpec((None, new_pad, N_ROWCOLS), per_b),
            pl.BlockSpec((None, new_pad, N_WINCOLS), per_b),
            pl.BlockSpec((None, n_wb, N_WINCOLS), per_b),
            pl.BlockSpec((None, n_rows, LANES), per_b),
            pl.BlockSpec((CMP_BLOCK, 2 * LANES), const),
            pl.BlockSpec((w_sel, n_keys), const),
        ],
        out_specs=pl.BlockSpec((None, n_rows, LANES), per_b),
        scratch_shapes=[
            pltpu.VMEM((n_keys, LANES), BF16), pltpu.VMEM((n_keys, LANES), BF16),
            pltpu.VMEM((n_cmp, LANES), BF16), pltpu.VMEM((n_cmp, LANES), BF16),
            pltpu.VMEM((w_keys, LANES), BF16), pltpu.VMEM((w_keys, LANES), BF16),
        ],
    )
    return pl.pallas_call(
        kern,
        grid_spec=grid_spec,
        out_shape=jax.ShapeDtypeStruct((bsz, n_rows, LANES), F32),
        compiler_params=pltpu.CompilerParams(
            dimension_semantics=("arbitrary", "arbitrary"), vmem_limit_bytes=VMEM_LIMIT),
        name="nsa_sample_attention",
    )(page_table, *([cache] * pages_per_step), qbd, new_rows, new_win, winbuf, gates, wc, e_mat)


def _nsa_weights(w_in):
    d = w_in.shape[0]
    c_kv = D_ATT
    c_g = c_kv + N_ROWCOLS + N_WINCOLS
    c_z = c_g + N_GATES
    pad = jnp.zeros((d, LANES - N_GATES), w_in.dtype)
    return jnp.concatenate([w_in[:, :c_g], w_in[:, c_z:], w_in[:, c_g:c_z], pad], axis=1).astype(BF16)


def _cmp_weight_tile(w_cmp):
    return w_cmp.transpose(1, 0, 2, 3).reshape(CMP_BLOCK, 2 * N_KV * HEAD_DIM)


def _nsa_layer_prompt(x, w_all, wc, w_out, ln_g, ln_b):
    bsz, t_len, d = x.shape
    x2 = x.reshape(bsz * t_len, d)
    q, rows, win, sz, gates = _nsa_proj(x2, w_all, row_tile=512)
    o = _nsa_prompt_attention(q.reshape(bsz, t_len, D_ATT), rows.reshape(bsz, t_len, N_ROWCOLS),
                              win.reshape(bsz, t_len, N_WINCOLS), gates.reshape(bsz, t_len, LANES), wc)
    y = _nsa_out(x2, o.reshape(bsz * t_len, D_ATT), sz, w_out, ln_g, ln_b, row_tile=512)
    keep = min(WINDOW, t_len)
    rows_out = rows.reshape(bsz, t_len, 4, N_KV, HEAD_DIM)
    win_out = win.reshape(bsz, t_len, 2, N_KV, HEAD_DIM)[:, t_len - keep:]
    return y.reshape(bsz, t_len, d), rows_out, win_out


def _nsa_layer_sample(x, cache, win_buf, page_table, w_all, wc, w_out, ln_g, ln_b):
    bsz, n_new, d = x.shape
    x2 = x.reshape(bsz * n_new, d)
    q, rows, win, sz, gates = _nsa_proj(x2, w_all, row_tile=bsz * n_new)
    q5 = q.reshape(bsz, n_new, N_KV, HPG, HEAD_DIM).transpose(0, 2, 1, 3, 4)
    eye = jnp.eye(N_KV, dtype=q.dtype)
    qbd = (q5[:, :, :, :, None, :] * eye[None, :, None, None, :, None]).reshape(
        bsz, N_KV * n_new * HPG, N_KV * HEAD_DIM)
    g5 = gates[:, :N_GATES].reshape(bsz, n_new, 3, N_KV, HPG).transpose(0, 3, 1, 4, 2)
    g_rows = jnp.pad(g5.reshape(bsz, N_KV * n_new * HPG, 3), ((0, 0), (0, 0), (0, LANES - 3)))
    new_pad = 16
    rows3 = rows.reshape(bsz, n_new, N_ROWCOLS)
    win3 = win.reshape(bsz, n_new, N_WINCOLS)
    pad_rows = lambda a: jnp.pad(a, ((0, 0), (0, new_pad - n_new), (0, 0)))
    n_wb = win_buf.shape[1]
    o_rows = _nsa_sample_attention(
        page_table, cache.reshape(cache.shape[0], PAGE_SIZE, N_ROWCOLS), qbd, pad_rows(rows3), pad_rows(win3),
        win_buf.reshape(bsz, n_wb, N_WINCOLS), g_rows, wc, n_new=n_new)
    o6 = o_rows.reshape(bsz, N_KV, n_new, HPG, N_KV, HEAD_DIM)
    o = jnp.stack([o6[:, g, :, :, g, :] for g in range(N_KV)], axis=2).reshape(bsz * n_new, D_ATT)
    y = _nsa_out(x2, o, sz, w_out, ln_g, ln_b, row_tile=bsz * n_new)
    all_win = jnp.concatenate([win_buf, win3.reshape(bsz, n_new, 2, N_KV, HEAD_DIM)], axis=1)
    return (y.reshape(bsz, n_new, d), rows3.reshape(bsz, n_new, 4, N_KV, HEAD_DIM),
            all_win[:, all_win.shape[1] - n_wb:])


def _time_major(a):
    bsz, k, d = a.shape
    return a.transpose(1, 0, 2).reshape(1, k * bsz, d)


def _batch_major(a, bsz):
    _, kb, d = a.shape
    return a.reshape(kb // bsz, bsz, d).transpose(1, 0, 2)


def kernel(x_prompt, x_sample, cache_nsa_kv, cache_nsa_win, state_conv_a, state_lru_h, state_lru_conv, page_table,
           ln_g, ln_b, a_w_in, a_conv_w, a_w_out, b_w_in, b_w_cmp, b_w_out,
           c_w_in, c_conv_w, c_conv_b, c_w_a, c_b_a, c_w_x, c_b_x, c_lam, c_w_out):
    bp, t_len, d = x_prompt.shape
    bs, n_new, _ = x_sample.shape
    yp = x_prompt
    ys = _time_major(x_sample)
    conv_a_p, conv_a_s = [], []
    rows_p, rows_s, win_p, win_s = [], [], [], []
    h_p, h_s, cc_p, cc_s = [], [], [], []
    row2 = lambda v: v.reshape(1, -1)
    for i in range(DEPTH):
        kind, j = i % 3, i // 3
        g, b = row2(ln_g[i]), row2(ln_b[i])
        if kind == 0:
            w_in, w_out = a_w_in[j].astype(BF16), a_w_out[j].astype(BF16)
            taps = a_conv_w.shape[1]
            yp, sp = _conv_layer(yp, jnp.zeros((bp, taps - 1, d), F32), w_in, a_conv_w[j], w_out, g, b,
                                 stride=1, row_tile=512)
            ys, ss = _conv_layer(ys, _time_major(state_conv_a[j]), w_in, a_conv_w[j], w_out, g, b,
                                 stride=bs, row_tile=n_new * bs)
            conv_a_p.append(sp)
            conv_a_s.append(_batch_major(ss, bs))
        elif kind == 1:
            w_all = _nsa_weights(b_w_in[j])
            wc = _cmp_weight_tile(b_w_cmp[j])
            w_out = b_w_out[j].astype(BF16)
            yp, rp, wp = _nsa_layer_prompt(yp, w_all, wc, w_out, g, b)
            ys_b, rs, ws = _nsa_layer_sample(_batch_major(ys, bs), cache_nsa_kv[j], cache_nsa_win[j], page_table,
                                             w_all, wc, w_out, g, b)
            ys = _time_major(ys_b)
            rows_p.append(rp)
            rows_s.append(rs)
            win_p.append(wp)
            win_s.append(ws)
        else:
            w_in, w_out = c_w_in[j].astype(BF16), c_w_out[j].astype(BF16)
            w_a, w_x = c_w_a[j].astype(BF16), c_w_x[j].astype(BF16)
            taps = c_conv_w.shape[1]
            args = (w_in, c_conv_w[j], row2(c_conv_b[j]), w_a, row2(c_b_a[j]), w_x, row2(c_b_x[j]),
                    row2(c_lam[j]), w_out, g, b)
            yp, hp, cp = _lru_layer(yp, jnp.zeros((bp, 1, d), F32), jnp.zeros((bp, taps - 1, d), F32), *args,
                                    stride=1, row_tile=256)
            ys, hs, cs = _lru_layer(ys, state_lru_h[j].reshape(1, bs, d), _time_major(state_lru_conv[j]), *args,
                                    stride=bs, row_tile=n_new * bs)
            h_p.append(hp.reshape(bp, d))
            h_s.append(hs.reshape(bs, d))
            cc_p.append(cp)
            cc_s.append(_batch_major(cs, bs))
    return (yp, _batch_major(ys, bs), jnp.stack(conv_a_p), jnp.stack(conv_a_s), jnp.stack(rows_p), jnp.stack(rows_s),
            jnp.stack(win_p), jnp.stack(win_s), jnp.stack(h_p), jnp.stack(h_s), jnp.stack(cc_p), jnp.stack(cc_s))
```

```python
import functools

import jax
import jax.numpy as jnp
from jax import lax
from jax.experimental import pallas as pl
from jax.experimental.pallas import tpu as pltpu

F32 = jnp.float32
BF16 = jnp.bfloat16

DEPTH = 4
N_HEADS = 16
N_KV = 2
HPG = N_HEADS // N_KV
HEAD_DIM = 64
CMP_BLOCK = 32
SLC_BLOCK = 64
TOP_N = 16
N_LOCAL = 2
WINDOW = 512
PAGE_SIZE = 128
NEG_INF = -1e30
FORCE = 1e9
RG_C = 8.0
N_RG_BLOCKS = 4
ALPHA = (2 * DEPTH) ** 0.25
LN_EPS = 1e-5
LOG2E = 1.4426950408889634

LANES = 128
SUBLANES = 8
VMEM_LIMIT = 56 * 1024 * 1024


def _dot(a, b):
    return jnp.dot(a, b, preferred_element_type=F32)


def _dot_nt(a, b):
    return lax.dot_general(a, b, (((1,), (1,)), ((), ())), preferred_element_type=F32)


def _silu(z):
    return z * jax.nn.sigmoid(z)


def _ln_residual(x, y, g, b):
    r = ALPHA * x + y
    mu = jnp.mean(r, axis=-1, keepdims=True)
    c = r - mu
    var = jnp.mean(c * c, axis=-1, keepdims=True)
    return c * lax.rsqrt(var + LN_EPS) * g + b


def _slope(head):
    return 2.0 ** (-8.0 * (head + 1) / N_HEADS)


def _prev_rows(u, carry, row, k):
    n_carry = carry.shape[0]
    out = pltpu.roll(u, k, 0)
    for i in range(k):
        out = jnp.where(row == i, carry[n_carry - k + i:n_carry - k + i + 1, :], out)
    return out


def _conv_layer_kernel(x_ref, buf_ref, win_ref, cw_ref, wout_ref, g_ref, b_ref,
                       y_ref, nbuf_ref, carry_ref, *, stride, col_chunk):
    rows, d = x_ref.shape
    taps = cw_ref.shape[0]
    x = x_ref[...]
    xb = x.astype(BF16)
    if stride == 1:
        @pl.when(pl.program_id(1) == 0)
        def _():
            carry_ref[...] = buf_ref[...]
        row = lax.broadcasted_iota(jnp.int32, (rows, 1), 0)
    acc = jnp.zeros((rows, d), F32)
    for j in range(d // col_chunk):
        lo = j * col_chunk
        sl = slice(lo, lo + col_chunk)
        h = _dot(xb, win_ref[:, lo:lo + col_chunk])
        bg = _dot(xb, win_ref[:, d + lo:d + lo + col_chunk])
        cg = _dot(xb, win_ref[:, 2 * d + lo:2 * d + lo + col_chunk])
        z = _dot(xb, win_ref[:, 3 * d + lo:3 * d + lo + col_chunk])
        u = cg * h
        conv = u * cw_ref[taps - 1:taps, sl]
        if stride == 1:
            carry = carry_ref[:, sl]
            for k in range(1, taps):
                conv = conv + _prev_rows(u, carry, row, k) * cw_ref[taps - 1 - k:taps - k, sl]
            new_buf = u[rows - (taps - 1):rows, :]
            carry_ref[:, sl] = new_buf
        else:
            ext = jnp.concatenate([buf_ref[:, sl], u], axis=0)
            for k in range(1, taps):
                s0 = (taps - 1 - k) * stride
                conv = conv + ext[s0:s0 + rows, :] * cw_ref[taps - 1 - k:taps - k, sl]
            new_buf = ext[rows:rows + (taps - 1) * stride, :]
        nbuf_ref[:, sl] = new_buf
        m = _silu(z) * bg * conv
        acc = acc + _dot(m.astype(BF16), wout_ref[sl, :])
    y_ref[...] = _ln_residual(x, acc, g_ref[...], b_ref[...])


def _conv_layer(x, buf, w_in, conv_w, w_out, ln_g, ln_b, *, stride, row_tile):
    bsz, t_len, d = x.shape
    n_buf = buf.shape[1]
    taps = conv_w.shape[0]
    grid = (bsz, t_len // row_tile)
    kern = functools.partial(_conv_layer_kernel, stride=stride, col_chunk=512)
    return pl.pallas_call(
        kern,
        grid=grid,
        in_specs=[
            pl.BlockSpec((None, row_tile, d), lambda b, t: (b, t, 0)),
            pl.BlockSpec((None, n_buf, d), lambda b, t: (b, 0, 0)),
            pl.BlockSpec((d, 4 * d), lambda b, t: (0, 0)),
            pl.BlockSpec((taps, d), lambda b, t: (0, 0)),
            pl.BlockSpec((d, d), lambda b, t: (0, 0)),
            pl.BlockSpec((1, d), lambda b, t: (0, 0)),
            pl.BlockSpec((1, d), lambda b, t: (0, 0)),
        ],
        out_specs=[
            pl.BlockSpec((None, row_tile, d), lambda b, t: (b, t, 0)),
            pl.BlockSpec((None, n_buf, d), lambda b, t: (b, 0, 0)),
        ],
        out_shape=[
            jax.ShapeDtypeStruct((bsz, t_len, d), F32),
            jax.ShapeDtypeStruct((bsz, n_buf, d), F32),
        ],
        scratch_shapes=[pltpu.VMEM((taps - 1, d), F32)],
        compiler_params=pltpu.CompilerParams(
            dimension_semantics=("arbitrary", "arbitrary"), vmem_limit_bytes=VMEM_LIMIT),
        name="conv_layer",
    )(x, buf, w_in, conv_w, w_out, ln_g, ln_b)


def _scan_rows(a, b, row):
    rows = a.shape[0]
    s = 1
    while s < rows:
        live = row >= s
        a_sh = jnp.where(live, pltpu.roll(a, s, 0), 1.0)
        b_sh = jnp.where(live, pltpu.roll(b, s, 0), 0.0)
        b = a * b_sh + b
        a = a * a_sh
        s *= 2
    return a, b


def _lru_layer_kernel(x_ref, h0_ref, buf_ref, win_ref, cw_ref, cb_ref, wa_ref, ba_ref, wx_ref, bx_ref,
                      lam_ref, wout_ref, g_ref, b_ref,
                      y_ref, hlast_ref, nbuf_ref, carry_ref, hcarry_ref, *, stride):
    rows, d = x_ref.shape
    taps = cw_ref.shape[0]
    n_blocks, blk, _ = wa_ref.shape
    x = x_ref[...]
    xb = x.astype(BF16)
    if stride == 1:
        @pl.when(pl.program_id(1) == 0)
        def _():
            carry_ref[...] = buf_ref[...]
            hcarry_ref[...] = h0_ref[...]
        row = lax.broadcasted_iota(jnp.int32, (rows, 1), 0)
    acc = jnp.zeros((rows, d), F32)
    for n in range(n_blocks):
        lo = n * blk
        sl = slice(lo, lo + blk)
        u = _dot(xb, win_ref[:, lo:lo + blk])
        z = _dot(xb, win_ref[:, d + lo:d + lo + blk])
        uc = u * cw_ref[taps - 1:taps, sl] + cb_ref[:, sl]
        if stride == 1:
            carry = carry_ref[:, sl]
            for k in range(1, taps):
                uc = uc + _prev_rows(u, carry, row, k) * cw_ref[taps - 1 - k:taps - k, sl]
            new_buf = u[rows - (taps - 1):rows, :]
            carry_ref[:, sl] = new_buf
        else:
            ext = jnp.concatenate([buf_ref[:, sl], u], axis=0)
            for k in range(1, taps):
                s0 = (taps - 1 - k) * stride
                uc = uc + ext[s0:s0 + rows, :] * cw_ref[taps - 1 - k:taps - k, sl]
            new_buf = ext[rows:rows + (taps - 1) * stride, :]
        nbuf_ref[:, sl] = new_buf
        ucb = uc.astype(BF16)
        r = jax.nn.sigmoid(_dot(ucb, wa_ref[n]) + ba_ref[:, sl])
        i = jax.nn.sigmoid(_dot(ucb, wx_ref[n]) + bx_ref[:, sl])
        nl = -lam_ref[:, sl]
        softplus = jnp.maximum(nl, 0.0) + jnp.log1p(jnp.exp(-jnp.abs(nl)))
        log_a = (-RG_C * softplus) * r
        a = jnp.exp(log_a)
        bb = jnp.sqrt(-jnp.tanh(log_a) * (a * a + 1.0)) * (i * uc)
        if stride == 1:
            a_cum, h_zero = _scan_rows(a, bb, row)
            h = a_cum * hcarry_ref[:, sl] + h_zero
            h_last = h[rows - 1:rows, :]
            hcarry_ref[:, sl] = h_last
        else:
            h_t = h0_ref[:, sl]
            hs = []
            for t in range(rows // stride):
                rs = slice(t * stride, (t + 1) * stride)
                h_t = a[rs, :] * h_t + bb[rs, :]
                hs.append(h_t)
            h = jnp.concatenate(hs, axis=0)
            h_last = h_t
        hlast_ref[:, sl] = h_last
        acc = acc + _dot((_silu(z) * h).astype(BF16), wout_ref[sl, :])
    y_ref[...] = _ln_residual(x, acc, g_ref[...], b_ref[...])


def _lru_layer(x, h0, buf, w_in, conv_w, conv_b, w_a, b_a, w_x, b_x, lam, w_out, ln_g, ln_b, *, stride, row_tile):
    bsz, t_len, d = x.shape
    n_buf = buf.shape[1]
    n_h = h0.shape[1]
    taps = conv_w.shape[0]
    nb, blk, _ = w_a.shape
    grid = (bsz, t_len // row_tile)
    const2 = lambda b, t: (0, 0)
    kern = functools.partial(_lru_layer_kernel, stride=stride)
    return pl.pallas_call(
        kern,
        grid=grid,
        in_specs=[
            pl.BlockSpec((None, row_tile, d), lambda b, t: (b, t, 0)),
            pl.BlockSpec((None, n_h, d), lambda b, t: (b, 0, 0)),
            pl.BlockSpec((None, n_buf, d), lambda b, t: (b, 0, 0)),
            pl.BlockSpec((d, 2 * d), const2),
            pl.BlockSpec((taps, d), const2),
            pl.BlockSpec((1, d), const2),
            pl.BlockSpec((nb, blk, blk), lambda b, t: (0, 0, 0)),
            pl.BlockSpec((1, d), const2),
            pl.BlockSpec((nb, blk, blk), lambda b, t: (0, 0, 0)),
            pl.BlockSpec((1, d), const2),
            pl.BlockSpec((1, d), const2),
            pl.BlockSpec((d, d), const2),
            pl.BlockSpec((1, d), const2),
            pl.BlockSpec((1, d), const2),
        ],
        out_specs=[
            pl.BlockSpec((None, row_tile, d), lambda b, t: (b, t, 0)),
            pl.BlockSpec((None, n_h, d), lambda b, t: (b, 0, 0)),
            pl.BlockSpec((None, n_buf, d), lambda b, t: (b, 0, 0)),
        ],
        out_shape=[
            jax.ShapeDtypeStruct((bsz, t_len, d), F32),
            jax.ShapeDtypeStruct((bsz, n_h, d), F32),
            jax.ShapeDtypeStruct((bsz, n_buf, d), F32),
        ],
        scratch_shapes=[pltpu.VMEM((taps - 1, d), F32), pltpu.VMEM((1, d), F32)],
        compiler_params=pltpu.CompilerParams(
            dimension_semantics=("arbitrary", "arbitrary"), vmem_limit_bytes=VMEM_LIMIT),
        name="lru_layer",
    )(x, h0, buf, w_in, conv_w, conv_b, w_a, b_a, w_x, b_x, lam, w_out, ln_g, ln_b)


D_ATT = N_HEADS * HEAD_DIM
N_ROWCOLS = 4 * N_KV * HEAD_DIM
N_WINCOLS = 2 * N_KV * HEAD_DIM
N_GATES = 3 * N_HEADS


def _nsa_proj_kernel(x_ref, w_ref, q_ref, rows_ref, win_ref, sz_ref, gates_ref):
    xb = x_ref[...].astype(BF16)
    c0 = 0
    q_ref[...] = (_dot(xb, w_ref[:, c0:c0 + D_ATT]) * (HEAD_DIM ** -0.5 * LOG2E)).astype(BF16)
    c0 += D_ATT
    rows_ref[...] = _dot(xb, w_ref[:, c0:c0 + N_ROWCOLS])
    c0 += N_ROWCOLS
    win_ref[...] = _dot(xb, w_ref[:, c0:c0 + N_WINCOLS])
    c0 += N_WINCOLS
    sz_ref[...] = _silu(_dot(xb, w_ref[:, c0:c0 + D_ATT]))
    c0 += D_ATT
    gates_ref[...] = jax.nn.sigmoid(_dot(xb, w_ref[:, c0:c0 + LANES]))


def _nsa_proj(x2, w_all, *, row_tile):
    n, d = x2.shape
    n_cols = w_all.shape[1]
    widths = (D_ATT, N_ROWCOLS, N_WINCOLS, D_ATT, LANES)
    dtypes = (BF16, F32, F32, F32, F32)
    return pl.pallas_call(
        _nsa_proj_kernel,
        grid=(n // row_tile,),
        in_specs=[pl.BlockSpec((row_tile, d), lambda i: (i, 0)),
                  pl.BlockSpec((d, n_cols), lambda i: (0, 0))],
        out_specs=[pl.BlockSpec((row_tile, w), lambda i: (i, 0)) for w in widths],
        out_shape=[jax.ShapeDtypeStruct((n, w), dt) for w, dt in zip(widths, dtypes)],
        compiler_params=pltpu.CompilerParams(
            dimension_semantics=("arbitrary",), vmem_limit_bytes=VMEM_LIMIT),
        name="nsa_proj",
    )(x2, w_all)


def _nsa_out_kernel(x_ref, o_ref, sz_ref, w_ref, g_ref, b_ref, y_ref):
    m = (o_ref[...] * sz_ref[...]).astype(BF16)
    y_ref[...] = _ln_residual(x_ref[...], _dot(m, w_ref[...]), g_ref[...], b_ref[...])


def _nsa_out(x2, o2, sz2, w_out, ln_g, ln_b, *, row_tile):
    n, d = x2.shape
    row = lambda i: (i, 0)
    const = lambda i: (0, 0)
    return pl.pallas_call(
        _nsa_out_kernel,
        grid=(n // row_tile,),
        in_specs=[pl.BlockSpec((row_tile, d), row), pl.BlockSpec((row_tile, d), row),
                  pl.BlockSpec((row_tile, d), row), pl.BlockSpec((d, d), const),
                  pl.BlockSpec((1, d), const), pl.BlockSpec((1, d), const)],
        out_specs=pl.BlockSpec((row_tile, d), row),
        out_shape=jax.ShapeDtypeStruct((n, d), F32),
        compiler_params=pltpu.CompilerParams(
            dimension_semantics=("arbitrary",), vmem_limit_bytes=VMEM_LIMIT),
        name="nsa_out",
    )(x2, o2, sz2, w_out, ln_g, ln_b)


def _dup_half(slab, g, lo):
    rolled = pltpu.roll(slab, HEAD_DIM, 1)
    return jnp.where(lo, slab, rolled) if g == 0 else jnp.where(lo, rolled, slab)


def _select_blocks(score, nl, real, n_slc, n_sel, axis):
    rank = jnp.zeros(score.shape, jnp.int32)
    for i in range(n_slc):
        cand = score[:, 2 * i:2 * i + 1] if axis == 1 else score[2 * i:2 * i + 1, :]
        beats = (cand > score) | ((cand == score) & (nl > 2 * i))
        rank = rank + beats.astype(jnp.int32)
    return real & (rank < n_sel)


def _block_scores(imp2, nl, tq, n_slc):
    blk = nl >> 1
    real = ((nl & 1) == 0) & (blk < n_slc)
    cur = tq >> 6
    forced = (blk == 0) | ((blk <= cur) & (blk > cur - N_LOCAL))
    score = jnp.where(forced, FORCE, jnp.where(blk <= cur, imp2, NEG_INF))
    return jnp.where(real, score, -jnp.inf), real


def _alibi_query_cols(head, n_rows):
    lane = lax.broadcasted_iota(jnp.int32, (1, LANES), 1)
    rem = jnp.full((1, LANES), _slope(head) * LOG2E, F32)
    row = jnp.zeros((1, LANES), F32)
    for i in range(3):
        piece = rem.astype(BF16).astype(F32)
        rem = rem - piece
        row = jnp.where(lane == i, 16.0 * piece, jnp.where(lane == 3 + i, piece, row))
    return jnp.broadcast_to(row, (n_rows, LANES)).astype(BF16)


def _alibi_key_cols(n_rows, period):
    r = lax.broadcasted_iota(jnp.int32, (n_rows, LANES), 0) & (period - 1)
    lane = lax.broadcasted_iota(jnp.int32, (n_rows, LANES), 1)
    return jnp.where(lane < 3, (r >> 4).astype(F32), jnp.where(lane < 6, (r & 15).astype(F32), 0.0))


def _nsa_prompt_kernel(q_ref, rows_ref, win_ref, gates_ref, wc_ref, o_ref,
                       ka, vta, kwa, vwta, kcd, vct_lo, vct_hi, qa_ref, selb_ref, acc_ref):
    blk = q_ref.shape[0]
    t_len = rows_ref.shape[0]
    n_cmp = t_len // CMP_BLOCK
    n_slc = -(-t_len // SLC_BLOCK)
    n_sel = min(TOP_N, n_slc)
    n_blk = t_len // blk
    slc_per_blk = blk // SLC_BLOCK
    n_wblk = WINDOW // blk + 1
    sel_rows = -(-2 * n_slc // SUBLANES) * SUBLANES
    v_rows = vta.shape[2]
    qi = pl.program_id(1)
    t0 = qi * blk
    lo = lax.broadcasted_iota(jnp.int32, (1, LANES), 1) < HEAD_DIM

    @pl.when(qi == 0)
    def _build():
        lo_t = lax.broadcasted_iota(jnp.int32, (t_len, LANES), 1) < HEAD_DIM
        lo_c = lax.broadcasted_iota(jnp.int32, (LANES, LANES), 1) < HEAD_DIM
        lo_r = lax.broadcasted_iota(jnp.int32, (LANES, 1), 0) < HEAD_DIM
        vrow = lax.broadcasted_iota(jnp.int32, (v_rows, 1), 0)
        cm = rows_ref[:, 0:2 * LANES]
        kvc = jnp.sum(cm.reshape(n_cmp, CMP_BLOCK, 2 * LANES) * wc_ref[...][None], axis=1)
        kvc = jnp.concatenate([kvc, jnp.zeros((LANES - n_cmp, 2 * LANES), F32)], axis=0)
        key_cols = _alibi_key_cols(t_len, blk).astype(BF16)

        def values_t(v):
            vt = v.T[0:v_rows, :]
            return jnp.where(vrow < HEAD_DIM, vt, jnp.where(vrow == HEAD_DIM, 1.0, 0.0)).astype(BF16)

        for g in range(N_KV):
            kd = _dup_half(rows_ref[:, 2 * LANES:3 * LANES], g, lo_t).astype(BF16)
            ka[g] = jnp.concatenate([kd, key_cols], axis=1)
            vt = values_t(_dup_half(rows_ref[:, 3 * LANES:4 * LANES], g, lo_t))
            kd = _dup_half(win_ref[:, 0:LANES], g, lo_t).astype(BF16)
            kwa[g] = jnp.concatenate([kd, key_cols], axis=1)
            vwt = values_t(_dup_half(win_ref[:, LANES:2 * LANES], g, lo_t))
            for c in range(n_blk):
                vta[g, c] = vt[:, c * blk:(c + 1) * blk]
                vwta[g, c] = vwt[:, c * blk:(c + 1) * blk]
            kcd[g] = _dup_half(kvc[:, 0:LANES], g, lo_c).astype(BF16)
            vct = _dup_half(kvc[:, LANES:2 * LANES], g, lo_c).T
            vct_lo[g] = jnp.where(lo_r, vct, 0.0).astype(BF16)
            vct_hi[g] = jnp.where(lo_r, 0.0, vct).astype(BF16)

    tq = t0 + lax.broadcasted_iota(jnp.int32, (1, blk), 1)
    blk_r = lax.broadcasted_iota(jnp.int32, (LANES, 1), 0)
    cmp_end = blk_r * CMP_BLOCK + (CMP_BLOCK - 1)
    valid_c = (cmp_end <= tq) & (blk_r < n_cmp)
    dist_c = (tq - cmp_end).astype(F32)

    lr = lax.broadcasted_iota(jnp.int32, (1, blk), 1) - lax.broadcasted_iota(jnp.int32, (blk, 1), 0)
    causal_bias = jnp.where(lr >= 0, 0.0, NEG_INF)

    w_start = pl.multiple_of(jnp.maximum(t0 - WINDOW, 0), blk)
    w_blk = w_start // blk
    w_shift, w_bias = [], []
    for i in range(n_wblk):
        shift = t0 - w_start - i * blk
        dist = lr + shift
        w_shift.append(shift.astype(F32))
        w_bias.append(jnp.where((dist >= 0) & (dist < WINDOW), 0.0, NEG_INF))

    gates_t = gates_ref[...].T

    for hh in range(N_HEADS):
        q2 = q_ref[:, (hh // 2) * LANES:(hh // 2 + 1) * LANES]
        zero = jnp.zeros_like(q2)
        qm = jnp.where(lo, q2, zero) if hh % 2 == 0 else jnp.where(lo, zero, q2)
        qa_ref[hh] = jnp.concatenate([qm, _alibi_query_cols(hh, blk)], axis=1)

    o_cmp = []
    for g in range(N_KV):
        imp = jnp.zeros((LANES, blk), F32)
        for j in range(HPG // 2):
            oc = jnp.zeros((LANES, blk), F32)
            for hh, v_ref in ((g * HPG + 2 * j, vct_lo), (g * HPG + 2 * j + 1, vct_hi)):
                s = _dot_nt(kcd[g], qa_ref[hh, :, 0:LANES]) - (_slope(hh) * LOG2E) * dist_c
                s = jnp.where(valid_c, s, NEG_INF)
                e = jnp.exp2(s - jnp.max(s, axis=0, keepdims=True))
                p = jnp.where(valid_c, e, 0.0) / jnp.sum(e, axis=0, keepdims=True)
                imp = imp + p
                oc = oc + _dot(v_ref[g], p.astype(BF16))
            o_cmp.append(oc)

        imp2 = (imp + pltpu.roll(imp, LANES - 1, 0))[0:sel_rows, :]
        score, real = _block_scores(imp2, blk_r[0:sel_rows, :], tq, n_slc)
        sel = _select_blocks(score, blk_r[0:sel_rows, :], real, n_slc, n_sel, axis=0)
        selb_ref[g] = jnp.where(sel, 0.0, NEG_INF)

    acc_ref[...] = jnp.zeros(acc_ref.shape, F32)

    def key_block(c, ms, extra_bias):
        off = pl.multiple_of(c * blk, blk)
        shift = (t0 - off).astype(F32)
        scores, new_ms = [], []
        for g in range(N_KV):
            k = ka[g, pl.ds(off, blk), :]
            rows2 = selb_ref[g, pl.ds(pl.multiple_of(c * (2 * slc_per_blk), 2 * slc_per_blk), 2 * slc_per_blk), :]
            bias = jnp.concatenate([jnp.broadcast_to(rows2[2 * m:2 * m + 1, :], (SLC_BLOCK, blk))
                                    for m in range(slc_per_blk)], axis=0)
            if extra_bias is not None:
                bias = bias + extra_bias
            for hh in range(g * HPG, (g + 1) * HPG):
                s = _dot_nt(k, qa_ref[hh]) + bias
                c_h = (_slope(hh) * LOG2E) * shift
                new_ms.append(jnp.maximum(ms[hh], jnp.max(s, axis=0, keepdims=True) - c_h))
                scores.append(s)
        for hh in range(N_HEADS):
            c_h = (_slope(hh) * LOG2E) * shift
            p = jnp.exp2(scores[hh] - (new_ms[hh] + c_h))
            acc_ref[hh] = (acc_ref[hh] * jnp.exp2(ms[hh] - new_ms[hh])
                           + _dot(vta[hh // HPG, c], p.astype(BF16)))
        return tuple(new_ms)

    ms = lax.fori_loop(0, qi, lambda c, ms: key_block(c, ms, None),
                       (jnp.full((1, blk), NEG_INF, F32),) * N_HEADS)
    key_block(qi, ms, causal_bias)

    for g in range(N_KV):
        k_win = [kwa[g, pl.ds(pl.multiple_of(w_start + i * blk, blk), blk), :] for i in range(n_wblk)]
        w_scores, w_max = [], []
        for hh in range(g * HPG, (g + 1) * HPG):
            slope = _slope(hh) * LOG2E
            tiles, m_w = [], None
            for i in range(n_wblk):
                s = _dot_nt(k_win[i], qa_ref[hh]) + w_bias[i]
                m_i = jnp.max(s, axis=0, keepdims=True) - slope * w_shift[i]
                m_w = m_i if m_w is None else jnp.maximum(m_w, m_i)
                tiles.append(s)
            w_scores.append(tiles)
            w_max.append(m_w)
        o_wins = []
        for h in range(HPG):
            slope = _slope(g * HPG + h) * LOG2E
            acc_w = jnp.zeros((v_rows, blk), F32)
            for i in range(n_wblk):
                p = jnp.exp2(w_scores[h][i] - (w_max[h] + slope * w_shift[i]))
                acc_w = acc_w + _dot(vwta[g, w_blk + i], p.astype(BF16))
            o_wins.append(acc_w[0:HEAD_DIM, :] / acc_w[HEAD_DIM:HEAD_DIM + 1, :])

        for j in range(HPG // 2):
            outs = []
            for hh in (g * HPG + 2 * j, g * HPG + 2 * j + 1):
                o_win = o_wins[hh - g * HPG]
                acc = acc_ref[hh]
                o_sel = acc[0:HEAD_DIM, :] / acc[HEAD_DIM:HEAD_DIM + 1, :]
                half = (hh % 2) * HEAD_DIM
                gate = lambda c: gates_t[c * N_HEADS + hh:c * N_HEADS + hh + 1, :]
                o_c = o_cmp[g * (HPG // 2) + j][half:half + HEAD_DIM, :]
                outs.append(gate(0) * o_c + gate(1) * o_sel + gate(2) * o_win)
            c0 = g * HPG * HEAD_DIM + j * LANES
            o_ref[:, c0:c0 + LANES] = jnp.concatenate(outs, axis=0).T


def _block_expand_matrix(n_rows, n_keys):
    n = jnp.arange(n_rows, dtype=jnp.int32)[:, None]
    p = jnp.arange(n_keys, dtype=jnp.int32)[None, :]
    return (n == 2 * (p // SLC_BLOCK)).astype(BF16)


def _nsa_prompt_attention(q, rows, win, gates, wc, *, q_tile=256):
    bsz, t_len, _ = q.shape
    assert WINDOW % q_tile == 0 and q_tile % (2 * LANES) == 0 and t_len >= WINDOW + q_tile
    assert t_len // CMP_BLOCK <= LANES
    v_rows = HEAD_DIM + 16
    k_scratch = pltpu.VMEM((N_KV, t_len, 2 * LANES), BF16)
    v_scratch = pltpu.VMEM((N_KV, t_len // q_tile, v_rows, q_tile), BF16)
    c_scratch = pltpu.VMEM((N_KV, LANES, LANES), BF16)
    return pl.pallas_call(
        _nsa_prompt_kernel,
        grid=(bsz, t_len // q_tile),
        in_specs=[
            pl.BlockSpec((None, q_tile, D_ATT), lambda b, i: (b, i, 0)),
            pl.BlockSpec((None, t_len, N_ROWCOLS), lambda b, i: (b, 0, 0)),
            pl.BlockSpec((None, t_len, N_WINCOLS), lambda b, i: (b, 0, 0)),
            pl.BlockSpec((None, q_tile, LANES), lambda b, i: (b, i, 0)),
            pl.BlockSpec((CMP_BLOCK, 2 * LANES), lambda b, i: (0, 0)),
        ],
        out_specs=pl.BlockSpec((None, q_tile, D_ATT), lambda b, i: (b, i, 0)),
        out_shape=jax.ShapeDtypeStruct((bsz, t_len, D_ATT), F32),
        scratch_shapes=[k_scratch, v_scratch, k_scratch, v_scratch,
                        c_scratch, c_scratch, c_scratch,
                        pltpu.VMEM((N_HEADS, q_tile, 2 * LANES), BF16),
                        pltpu.VMEM((N_KV, -(-2 * (-(-t_len // SLC_BLOCK)) // SUBLANES) * SUBLANES, q_tile), F32),
                        pltpu.VMEM((N_HEADS, v_rows, q_tile), F32)],
        compiler_params=pltpu.CompilerParams(
            dimension_semantics=("arbitrary", "arbitrary"), vmem_limit_bytes=VMEM_LIMIT),
        name="nsa_prompt_attention",
    )(q, rows, win, gates, wc)


def _nsa_sample_kernel(pt_ref, *refs, pages_per_step, n_steps, n_new):
    page_refs = refs[:pages_per_step]
    (qbd_ref, new_rows_ref, new_win_ref, winbuf_ref, gates_ref, wc_ref, e_ref, o_ref,
     kslab, vslab, kcs, vcs, kwslab, vwslab) = refs[pages_per_step:]
    del pt_ref
    step = pl.program_id(1)
    p_len = n_steps * pages_per_step * PAGE_SIZE
    n_keys = kslab.shape[0]
    n_cmp = p_len // CMP_BLOCK
    n_slc = -(-(p_len + n_new) // SLC_BLOCK)
    n_sel = min(TOP_N, n_slc)
    cmp_per_step = pages_per_step * PAGE_SIZE // CMP_BLOCK
    n_rows = N_KV * n_new * HPG
    w_keys = kwslab.shape[0]
    tail = n_keys - p_len
    new_pad = new_rows_ref.shape[0]

    cm = jnp.concatenate([r[:, 0:2 * LANES] for r in page_refs], axis=0)
    kvc = jnp.sum(cm.reshape(cmp_per_step, CMP_BLOCK, 2 * LANES) * wc_ref[...][None], axis=1)
    c_off = pl.multiple_of(step * cmp_per_step, cmp_per_step)
    kcs[pl.ds(c_off, cmp_per_step), :] = kvc[:, 0:LANES].astype(BF16)
    vcs[pl.ds(c_off, cmp_per_step), :] = kvc[:, LANES:2 * LANES].astype(BF16)
    for k, r in enumerate(page_refs):
        r0 = pl.multiple_of((step * pages_per_step + k) * PAGE_SIZE, PAGE_SIZE)
        kslab[pl.ds(r0, PAGE_SIZE), :] = r[:, 2 * LANES:3 * LANES].astype(BF16)
        vslab[pl.ds(r0, PAGE_SIZE), :] = r[:, 3 * LANES:4 * LANES].astype(BF16)

    @pl.when(step == n_steps - 1)
    def _attend():
        kslab[p_len:n_keys, :] = jnp.zeros((tail, LANES), BF16)
        vslab[p_len:n_keys, :] = jnp.zeros((tail, LANES), BF16)
        kslab[p_len:p_len + new_pad, :] = new_rows_ref[:, 2 * LANES:3 * LANES].astype(BF16)
        vslab[p_len:p_len + new_pad, :] = new_rows_ref[:, 3 * LANES:4 * LANES].astype(BF16)
        n_wb = winbuf_ref.shape[0]
        kwslab[0:n_wb, :] = winbuf_ref[:, 0:LANES].astype(BF16)
        vwslab[0:n_wb, :] = winbuf_ref[:, LANES:2 * LANES].astype(BF16)
        kwslab[n_wb:w_keys, :] = jnp.zeros((w_keys - n_wb, LANES), BF16)
        vwslab[n_wb:w_keys, :] = jnp.zeros((w_keys - n_wb, LANES), BF16)
        kwslab[n_wb:n_wb + new_pad, :] = new_win_ref[:, 0:LANES].astype(BF16)
        vwslab[n_wb:n_wb + new_pad, :] = new_win_ref[:, LANES:2 * LANES].astype(BF16)

        qbd = qbd_ref[...]
        r = lax.broadcasted_iota(jnp.int32, (n_rows, 1), 0)
        head = (r // (n_new * HPG)) * HPG + (r % HPG)
        slope = jnp.exp2((head + 1).astype(F32) * (-8.0 / N_HEADS)) * LOG2E
        tq = p_len + (r // HPG) % n_new

        def attend(k_ref, v_ref, dist, bias):
            s = _dot_nt(qbd, k_ref[...]) - slope * dist.astype(F32) + bias
            e = jnp.exp2(s - jnp.max(s, axis=-1, keepdims=True))
            return _dot(e.astype(BF16), v_ref[...]) / jnp.sum(e, axis=-1, keepdims=True)

        nl = lax.broadcasted_iota(jnp.int32, (n_rows, n_cmp), 1)
        cmp_end = nl * CMP_BLOCK + (CMP_BLOCK - 1)
        valid_c = cmp_end <= tq
        s = _dot_nt(qbd, kcs[...]) - slope * (tq - cmp_end).astype(F32)
        s = jnp.where(valid_c, s, NEG_INF)
        e = jnp.exp2(s - jnp.max(s, axis=-1, keepdims=True))
        p = jnp.where(valid_c, e, 0.0) / jnp.sum(e, axis=-1, keepdims=True)
        o_cmp = _dot(p.astype(BF16), vcs[...])

        n_gt = N_KV * n_new
        imp = jnp.sum(p.reshape(n_gt, HPG, n_cmp), axis=1)
        imp2 = imp + pltpu.roll(imp, n_cmp - 1, 1)
        w_sel = e_ref.shape[0]
        imp2 = jnp.concatenate([imp2, jnp.zeros((n_gt, w_sel - n_cmp), F32)], axis=1)
        nl8 = lax.broadcasted_iota(jnp.int32, (n_gt, w_sel), 1)
        r8 = lax.broadcasted_iota(jnp.int32, (n_gt, 1), 0)
        score, real = _block_scores(imp2, nl8, p_len + r8 % n_new, n_slc)
        sel = _select_blocks(score, nl8, real, n_slc, n_sel, axis=1)
        selb = jnp.where(sel, 1.0, 0.0)
        selb = jnp.broadcast_to(selb[:, None, :], (n_gt, HPG, w_sel)).reshape(n_rows, w_sel)
        hit = _dot(selb.astype(BF16), e_ref[...])
        dist = tq - lax.broadcasted_iota(jnp.int32, (n_rows, n_keys), 1)
        o_sel = attend(kslab, vslab, dist, jnp.where((hit > 0.5) & (dist >= 0), 0.0, NEG_INF))

        dist_w = tq - (p_len - n_wb) - lax.broadcasted_iota(jnp.int32, (n_rows, w_keys), 1)
        o_win = attend(kwslab, vwslab, dist_w,
                       jnp.where((dist_w >= 0) & (dist_w < WINDOW), 0.0, NEG_INF))

        o_ref[...] = (gates_ref[:, 0:1] * o_cmp + gates_ref[:, 1:2] * o_sel + gates_ref[:, 2:3] * o_win)


def _nsa_sample_attention(page_table, cache, qbd, new_rows, new_win, winbuf, gates, wc, *, n_new,
                          pages_per_step=16):
    bsz, n_pages = page_table.shape
    n_steps = n_pages // pages_per_step
    p_len = n_pages * PAGE_SIZE
    n_rows = qbd.shape[1]
    new_pad = new_rows.shape[1]
    n_wb = winbuf.shape[1]
    n_keys = p_len + LANES
    w_keys = n_wb + LANES
    n_cmp = p_len // CMP_BLOCK
    n_slc = -(-(p_len + n_new) // SLC_BLOCK)
    w_sel = -(-2 * n_slc // LANES) * LANES
    e_mat = _block_expand_matrix(w_sel, n_keys)
    kern = functools.partial(_nsa_sample_kernel, pages_per_step=pages_per_step, n_steps=n_steps, n_new=n_new)

    def page_spec(k):
        return pl.BlockSpec((None, PAGE_SIZE, N_ROWCOLS),
                            lambda b, s, pt: (pt[b, s * pages_per_step + k], 0, 0))

    per_b = lambda b, s, pt: (b, 0, 0)
    const = lambda b, s, pt: (0, 0)
    grid_spec = pltpu.PrefetchScalarGridSpec(
        num_scalar_prefetch=1,
        grid=(bsz, n_steps),
        in_specs=[page_spec(k) for k in range(pages_per_step)] + [
            pl.BlockSpec((None, n_rows, LANES), per_b),
            pl.BlockSpec((None, new_pad, N_ROWCOLS), per_b),
            pl.BlockSpec((None, new_pad, N_WINCOLS), per_b),
            pl.BlockSpec((None, n_wb, N_WINCOLS), per_b),
            pl.BlockSpec((None, n_rows, LANES), per_b),
            pl.BlockSpec((CMP_BLOCK, 2 * LANES), const),
            pl.BlockSpec((w_sel, n_keys), const),
        ],
        out_specs=pl.BlockSpec((None, n_rows, LANES), per_b),
        scratch_shapes=[
            pltpu.VMEM((n_keys, LANES), BF16), pltpu.VMEM((n_keys, LANES), BF16),
            pltpu.VMEM((n_cmp, LANES), BF16), pltpu.VMEM((n_cmp, LANES), BF16),
            pltpu.VMEM((w_keys, LANES), BF16), pltpu.VMEM((w_keys, LANES), BF16),
        ],
    )
    return pl.pallas_call(
        kern,
        grid_spec=grid_spec,
        out_shape=jax.ShapeDtypeStruct((bsz, n_rows, LANES), F32),
        compiler_params=pltpu.CompilerParams(
            dimension_semantics=("arbitrary", "arbitrary"), vmem_limit_bytes=VMEM_LIMIT),
        name="nsa_sample_attention",
    )(page_table, *([cache] * pages_per_step), qbd, new_rows, new_win, winbuf, gates, wc, e_mat)


def _nsa_weights(w_in):
    d = w_in.shape[0]
    c_kv = D_ATT
    c_g = c_kv + N_ROWCOLS + N_WINCOLS
    c_z = c_g + N_GATES
    pad = jnp.zeros((d, LANES - N_GATES), w_in.dtype)
    return jnp.concatenate([w_in[:, :c_g], w_in[:, c_z:], w_in[:, c_g:c_z], pad], axis=1).astype(BF16)


def _cmp_weight_tile(w_cmp):
    return w_cmp.transpose(1, 0, 2, 3).reshape(CMP_BLOCK, 2 * N_KV * HEAD_DIM)


def _nsa_layer_prompt(x, w_all, wc, w_out, ln_g, ln_b):
    bsz, t_len, d = x.shape
    x2 = x.reshape(bsz * t_len, d)
    q, rows, win, sz, gates = _nsa_proj(x2, w_all, row_tile=512)
    o = _nsa_prompt_attention(q.reshape(bsz, t_len, D_ATT), rows.reshape(bsz, t_len, N_ROWCOLS),
                              win.reshape(bsz, t_len, N_WINCOLS), gates.reshape(bsz, t_len, LANES), wc)
    y = _nsa_out(x2, o.reshape(bsz * t_len, D_ATT), sz, w_out, ln_g, ln_b, row_tile=512)
    keep = min(WINDOW, t_len)
    rows_out = rows.reshape(bsz, t_len, 4, N_KV, HEAD_DIM)
    win_out = win.reshape(bsz, t_len, 2, N_KV, HEAD_DIM)[:, t_len - keep:]
    return y.reshape(bsz, t_len, d), rows_out, win_out


def _nsa_layer_sample(x, cache, win_buf, page_table, w_all, wc, w_out, ln_g, ln_b):
    bsz, n_new, d = x.shape
    x2 = x.reshape(bsz * n_new, d)
    q, rows, win, sz, gates = _nsa_proj(x2, w_all, row_tile=bsz * n_new)
    q5 = q.reshape(bsz, n_new, N_KV, HPG, HEAD_DIM).transpose(0, 2, 1, 3, 4)
    eye = jnp.eye(N_KV, dtype=q.dtype)
    qbd = (q5[:, :, :, :, None, :] * eye[None, :, None, None, :, None]).reshape(
        bsz, N_KV * n_new * HPG, N_KV * HEAD_DIM)
    g5 = gates[:, :N_GATES].reshape(bsz, n_new, 3, N_KV, HPG).transpose(0, 3, 1, 4, 2)
    g_rows = jnp.pad(g5.reshape(bsz, N_KV * n_new * HPG, 3), ((0, 0), (0, 0), (0, LANES - 3)))
    new_pad = 16
    rows3 = rows.reshape(bsz, n_new, N_ROWCOLS)
    win3 = win.reshape(bsz, n_new, N_WINCOLS)
    pad_rows = lambda a: jnp.pad(a, ((0, 0), (0, new_pad - n_new), (0, 0)))
    n_wb = win_buf.shape[1]
    o_rows = _nsa_sample_attention(
        page_table, cache.reshape(cache.shape[0], PAGE_SIZE, N_ROWCOLS), qbd, pad_rows(rows3), pad_rows(win3),
        win_buf.reshape(bsz, n_wb, N_WINCOLS), g_rows, wc, n_new=n_new)
    o6 = o_rows.reshape(bsz, N_KV, n_new, HPG, N_KV, HEAD_DIM)
    o = jnp.stack([o6[:, g, :, :, g, :] for g in range(N_KV)], axis=2).reshape(bsz * n_new, D_ATT)
    y = _nsa_out(x2, o, sz, w_out, ln_g, ln_b, row_tile=bsz * n_new)
    all_win = jnp.concatenate([win_buf, win3.reshape(bsz, n_new, 2, N_KV, HEAD_DIM)], axis=1)
    return (y.reshape(bsz, n_new, d), rows3.reshape(bsz, n_new, 4, N_KV, HEAD_DIM),
            all_win[:, all_win.shape[1] - n_wb:])


def _time_major(a):
    bsz, k, d = a.shape
    return a.transpose(1, 0, 2).reshape(1, k * bsz, d)


def _batch_major(a, bsz):
    _, kb, d = a.shape
    return a.reshape(kb // bsz, bsz, d).transpose(1, 0, 2)


def kernel(x_prompt, x_sample, cache_nsa_kv, cache_nsa_win, state_conv_a, state_lru_h, state_lru_conv, page_table,
           ln_g, ln_b, a_w_in, a_conv_w, a_w_out, b_w_in, b_w_cmp, b_w_out,
           c_w_in, c_conv_w, c_conv_b, c_w_a, c_b_a, c_w_x, c_b_x, c_lam, c_w_out):
    bp, t_len, d = x_prompt.shape
    bs, n_new, _ = x_sample.shape
    yp = x_prompt
    ys = _time_major(x_sample)
    conv_a_p, conv_a_s = [], []
    rows_p, rows_s, win_p, win_s = [], [], [], []
    h_p, h_s, cc_p, cc_s = [], [], [], []
    row2 = lambda v: v.reshape(1, -1)
    for i in range(DEPTH):
        kind, j = i % 3, i // 3
        g, b = row2(ln_g[i]), row2(ln_b[i])
        if kind == 0:
            w_in, w_out = a_w_in[j].astype(BF16), a_w_out[j].astype(BF16)
            taps = a_conv_w.shape[1]
            yp, sp = _conv_layer(yp, jnp.zeros((bp, taps - 1, d), F32), w_in, a_conv_w[j], w_out, g, b,
                                 stride=1, row_tile=512)
            ys, ss = _conv_layer(ys, _time_major(state_conv_a[j]), w_in, a_conv_w[j], w_out, g, b,
                                 stride=bs, row_tile=n_new * bs)
            conv_a_p.append(sp)
            conv_a_s.append(_batch_major(ss, bs))
        elif kind == 1:
            w_all = _nsa_weights(b_w_in[j])
            wc = _cmp_weight_tile(b_w_cmp[j])
            w_out = b_w_out[j].astype(BF16)
            yp, rp, wp = _nsa_layer_prompt(yp, w_all, wc, w_out, g, b)
            ys_b, rs, ws = _nsa_layer_sample(_batch_major(ys, bs), cache_nsa_kv[j], cache_nsa_win[j], page_table,
                                             w_all, wc, w_out, g, b)
            ys = _time_major(ys_b)
            rows_p.append(rp)
            rows_s.append(rs)
            win_p.append(wp)
            win_s.append(ws)
        else:
            w_in, w_out = c_w_in[j].astype(BF16), c_w_out[j].astype(BF16)
            w_a, w_x = c_w_a[j].astype(BF16), c_w_x[j].astype(BF16)
            taps = c_conv_w.shape[1]
            args = (w_in, c_conv_w[j], row2(c_conv_b[j]), w_a, row2(c_b_a[j]), w_x, row2(c_b_x[j]),
                    row2(c_lam[j]), w_out, g, b)
            yp, hp, cp = _lru_layer(yp, jnp.zeros((bp, 1, d), F32), jnp.zeros((bp, taps - 1, d), F32), *args,
                                    stride=1, row_tile=256)
            ys, hs, cs = _lru_layer(ys, state_lru_h[j].reshape(1, bs, d), _time_major(state_lru_conv[j]), *args,
                                    stride=bs, row_tile=n_new * bs)
            h_p.append(hp.reshape(bp, d))
            h_s.append(hs.reshape(bs, d))
            cc_p.append(cp)
            cc_s.append(_batch_major(cs, bs))
    return (yp, _batch_major(ys, bs), jnp.stack(conv_a_p), jnp.stack(conv_a_s), jnp.stack(rows_p), jnp.stack(rows_s),
            jnp.stack(win_p), jnp.stack(win_s), jnp.stack(h_p), jnp.stack(h_s), jnp.stack(cc_p), jnp.stack(cc_s))
```

```python
import functools

import jax
import jax.numpy as jnp
from jax import lax
from jax.experimental import pallas as pl
from jax.experimental.pallas import tpu as pltpu

F32 = jnp.float32
BF16 = jnp.bfloat16

DEPTH = 4
N_HEADS = 16
N_KV = 2
HPG = N_HEADS // N_KV
HEAD_DIM = 64
CMP_BLOCK = 32
SLC_BLOCK = 64
TOP_N = 16
N_LOCAL = 2
WINDOW = 512
PAGE_SIZE = 128
NEG_INF = -1e30
FORCE = 1e9
RG_C = 8.0
N_RG_BLOCKS = 4
ALPHA = (2 * DEPTH) ** 0.25
LN_EPS = 1e-5
LOG2E = 1.4426950408889634

LANES = 128
SUBLANES = 8
VMEM_LIMIT = 56 * 1024 * 1024


def _dot(a, b):
    return jnp.dot(a, b, preferred_element_type=F32)


def _dot_nt(a, b):
    return lax.dot_general(a, b, (((1,), (1,)), ((), ())), preferred_element_type=F32)


def _silu(z):
    return z * jax.nn.sigmoid(z)


def _ln_residual(x, y, g, b):
    r = ALPHA * x + y
    mu = jnp.mean(r, axis=-1, keepdims=True)
    c = r - mu
    var = jnp.mean(c * c, axis=-1, keepdims=True)
    return c * lax.rsqrt(var + LN_EPS) * g + b


def _slope(head):
    return 2.0 ** (-8.0 * (head + 1) / N_HEADS)


def _prev_rows(u, carry, k):
    n_carry = carry.shape[0]
    rolled = pltpu.roll(u, k, 0)
    row = lax.broadcasted_iota(jnp.int32, (SUBLANES, 1), 0)
    head = rolled[0:SUBLANES, :]
    for i in range(k):
        head = jnp.where(row == i, carry[n_carry - k + i:n_carry - k + i + 1, :], head)
    return jnp.concatenate([head, rolled[SUBLANES:, :]], axis=0)


def _conv_layer_kernel(x_ref, buf_ref, win_ref, cw_ref, wout_ref, g_ref, b_ref,
                       y_ref, nbuf_ref, carry_ref, *, stride, col_chunk):
    rows, d = x_ref.shape
    taps = cw_ref.shape[0]
    x = x_ref[...]
    xb = x.astype(BF16)
    if stride == 1:
        @pl.when(pl.program_id(1) == 0)
        def _():
            carry_ref[...] = buf_ref[...]
    acc = jnp.zeros((rows, d), F32)
    for j in range(d // col_chunk):
        lo = j * col_chunk
        sl = slice(lo, lo + col_chunk)
        h = _dot(xb, win_ref[:, lo:lo + col_chunk])
        bg = _dot(xb, win_ref[:, d + lo:d + lo + col_chunk])
        cg = _dot(xb, win_ref[:, 2 * d + lo:2 * d + lo + col_chunk])
        z = _dot(xb, win_ref[:, 3 * d + lo:3 * d + lo + col_chunk])
        u = cg * h
        conv = u * cw_ref[taps - 1:taps, sl]
        if stride == 1:
            carry = carry_ref[:, sl]
            for k in range(1, taps):
                conv = conv + _prev_rows(u, carry, k) * cw_ref[taps - 1 - k:taps - k, sl]
            new_buf = u[rows - (taps - 1):rows, :]
            carry_ref[:, sl] = new_buf
        else:
            ext = jnp.concatenate([buf_ref[:, sl], u], axis=0)
            for k in range(1, taps):
                s0 = (taps - 1 - k) * stride
                conv = conv + ext[s0:s0 + rows, :] * cw_ref[taps - 1 - k:taps - k, sl]
            new_buf = ext[rows:rows + (taps - 1) * stride, :]
        nbuf_ref[:, sl] = new_buf
        m = _silu(z) * bg * conv
        acc = acc + _dot(m.astype(BF16), wout_ref[sl, :])
    y_ref[...] = _ln_residual(x, acc, g_ref[...], b_ref[...])


def _conv_layer(x, buf, w_in, conv_w, w_out, ln_g, ln_b, *, stride, row_tile):
    bsz, t_len, d = x.shape
    n_buf = buf.shape[1]
    taps = conv_w.shape[0]
    grid = (bsz, t_len // row_tile)
    kern = functools.partial(_conv_layer_kernel, stride=stride, col_chunk=512)
    return pl.pallas_call(
        kern,
        grid=grid,
        in_specs=[
            pl.BlockSpec((None, row_tile, d), lambda b, t: (b, t, 0)),
            pl.BlockSpec((None, n_buf, d), lambda b, t: (b, 0, 0)),
            pl.BlockSpec((d, 4 * d), lambda b, t: (0, 0)),
            pl.BlockSpec((taps, d), lambda b, t: (0, 0)),
            pl.BlockSpec((d, d), lambda b, t: (0, 0)),
            pl.BlockSpec((1, d), lambda b, t: (0, 0)),
            pl.BlockSpec((1, d), lambda b, t: (0, 0)),
        ],
        out_specs=[
            pl.BlockSpec((None, row_tile, d), lambda b, t: (b, t, 0)),
            pl.BlockSpec((None, n_buf, d), lambda b, t: (b, 0, 0)),
        ],
        out_shape=[
            jax.ShapeDtypeStruct((bsz, t_len, d), F32),
            jax.ShapeDtypeStruct((bsz, n_buf, d), F32),
        ],
        scratch_shapes=[pltpu.VMEM((taps - 1, d), F32)],
        compiler_params=pltpu.CompilerParams(
            dimension_semantics=("arbitrary", "arbitrary"), vmem_limit_bytes=VMEM_LIMIT),
        name="conv_layer",
    )(x, buf, w_in, conv_w, w_out, ln_g, ln_b)


def _scan_rows(a, b):
    rows = a.shape[0]
    row = lax.broadcasted_iota(jnp.int32, (rows, 1), 0)
    s = 1
    while s < rows:
        if s < SUBLANES:
            live = row >= s
            a_sh = jnp.where(live, pltpu.roll(a, s, 0), 1.0)
            b_sh = jnp.where(live, pltpu.roll(b, s, 0), 0.0)
            b = a * b_sh + b
            a = a * a_sh
        else:
            b = jnp.concatenate([b[:s, :], a[s:, :] * b[:rows - s, :] + b[s:, :]], axis=0)
            a = jnp.concatenate([a[:s, :], a[s:, :] * a[:rows - s, :]], axis=0)
        s *= 2
    return a, b


def _lru_layer_kernel(x_ref, h0_ref, buf_ref, win_ref, cw_ref, cb_ref, wa_ref, ba_ref, wx_ref, bx_ref,
                      lam_ref, wout_ref, g_ref, b_ref,
                      y_ref, hlast_ref, nbuf_ref, carry_ref, hcarry_ref, *, stride):
    rows, d = x_ref.shape
    taps = cw_ref.shape[0]
    n_blocks, blk, _ = wa_ref.shape
    x = x_ref[...]
    xb = x.astype(BF16)
    if stride == 1:
        @pl.when(pl.program_id(1) == 0)
        def _():
            carry_ref[...] = buf_ref[...]
            hcarry_ref[...] = h0_ref[...]
    acc = jnp.zeros((rows, d), F32)
    for n in range(n_blocks):
        lo = n * blk
        sl = slice(lo, lo + blk)
        u = _dot(xb, win_ref[:, lo:lo + blk])
        z = _dot(xb, win_ref[:, d + lo:d + lo + blk])
        uc = u * cw_ref[taps - 1:taps, sl] + cb_ref[:, sl]
        if stride == 1:
            carry = carry_ref[:, sl]
            for k in range(1, taps):
                uc = uc + _prev_rows(u, carry, k) * cw_ref[taps - 1 - k:taps - k, sl]
            new_buf = u[rows - (taps - 1):rows, :]
            carry_ref[:, sl] = new_buf
        else:
            ext = jnp.concatenate([buf_ref[:, sl], u], axis=0)
            for k in range(1, taps):
                s0 = (taps - 1 - k) * stride
                uc = uc + ext[s0:s0 + rows, :] * cw_ref[taps - 1 - k:taps - k, sl]
            new_buf = ext[rows:rows + (taps - 1) * stride, :]
        nbuf_ref[:, sl] = new_buf
        ucb = uc.astype(BF16)
        r = jax.nn.sigmoid(_dot(ucb, wa_ref[n]) + ba_ref[:, sl])
        i = jax.nn.sigmoid(_dot(ucb, wx_ref[n]) + bx_ref[:, sl])
        nl = -lam_ref[:, sl]
        softplus = jnp.maximum(nl, 0.0) + jnp.log1p(jnp.exp(-jnp.abs(nl)))
        log_a = (-RG_C * softplus) * r
        a = jnp.exp(log_a)
        bb = jnp.sqrt(-jnp.tanh(log_a) * (a * a + 1.0)) * (i * uc)
        if stride == 1:
            a_cum, h_zero = _scan_rows(a, bb)
            h = a_cum * hcarry_ref[:, sl] + h_zero
            h_last = h[rows - 1:rows, :]
            hcarry_ref[:, sl] = h_last
        else:
            h_t = h0_ref[:, sl]
            hs = []
            for t in range(rows // stride):
                rs = slice(t * stride, (t + 1) * stride)
                h_t = a[rs, :] * h_t + bb[rs, :]
                hs.append(h_t)
            h = jnp.concatenate(hs, axis=0)
            h_last = h_t
        hlast_ref[:, sl] = h_last
        acc = acc + _dot((_silu(z) * h).astype(BF16), wout_ref[sl, :])
    y_ref[...] = _ln_residual(x, acc, g_ref[...], b_ref[...])


def _lru_layer(x, h0, buf, w_in, conv_w, conv_b, w_a, b_a, w_x, b_x, lam, w_out, ln_g, ln_b, *, stride, row_tile):
    bsz, t_len, d = x.shape
    n_buf = buf.shape[1]
    n_h = h0.shape[1]
    taps = conv_w.shape[0]
    nb, blk, _ = w_a.shape
    grid = (bsz, t_len // row_tile)
    const2 = lambda b, t: (0, 0)
    kern = functools.partial(_lru_layer_kernel, stride=stride)
    return pl.pallas_call(
        kern,
        grid=grid,
        in_specs=[
            pl.BlockSpec((None, row_tile, d), lambda b, t: (b, t, 0)),
            pl.BlockSpec((None, n_h, d), lambda b, t: (b, 0, 0)),
            pl.BlockSpec((None, n_buf, d), lambda b, t: (b, 0, 0)),
            pl.BlockSpec((d, 2 * d), const2),
            pl.BlockSpec((taps, d), const2),
            pl.BlockSpec((1, d), const2),
            pl.BlockSpec((nb, blk, blk), lambda b, t: (0, 0, 0)),
            pl.BlockSpec((1, d), const2),
            pl.BlockSpec((nb, blk, blk), lambda b, t: (0, 0, 0)),
            pl.BlockSpec((1, d), const2),
            pl.BlockSpec((1, d), const2),
            pl.BlockSpec((d, d), const2),
            pl.BlockSpec((1, d), const2),
            pl.BlockSpec((1, d), const2),
        ],
        out_specs=[
            pl.BlockSpec((None, row_tile, d), lambda b, t: (b, t, 0)),
            pl.BlockSpec((None, n_h, d), lambda b, t: (b, 0, 0)),
            pl.BlockSpec((None, n_buf, d), lambda b, t: (b, 0, 0)),
        ],
        out_shape=[
            jax.ShapeDtypeStruct((bsz, t_len, d), F32),
            jax.ShapeDtypeStruct((bsz, n_h, d), F32),
            jax.ShapeDtypeStruct((bsz, n_buf, d), F32),
        ],
        scratch_shapes=[pltpu.VMEM((taps - 1, d), F32), pltpu.VMEM((1, d), F32)],
        compiler_params=pltpu.CompilerParams(
            dimension_semantics=("arbitrary", "arbitrary"), vmem_limit_bytes=VMEM_LIMIT),
        name="lru_layer",
    )(x, h0, buf, w_in, conv_w, conv_b, w_a, b_a, w_x, b_x, lam, w_out, ln_g, ln_b)


D_ATT = N_HEADS * HEAD_DIM
N_ROWCOLS = 4 * N_KV * HEAD_DIM
N_WINCOLS = 2 * N_KV * HEAD_DIM
N_GATES = 3 * N_HEADS


def _nsa_proj_kernel(x_ref, w_ref, q_ref, rows_ref, win_ref, sz_ref, gates_ref, *t_refs):
    xb = x_ref[...].astype(BF16)
    c0 = 0
    q_ref[...] = (_dot(xb, w_ref[:, c0:c0 + D_ATT]) * (HEAD_DIM ** -0.5 * LOG2E)).astype(BF16)
    c0 += D_ATT
    rows = _dot(xb, w_ref[:, c0:c0 + N_ROWCOLS])
    rows_ref[...] = rows
    c0 += N_ROWCOLS
    win = _dot(xb, w_ref[:, c0:c0 + N_WINCOLS])
    win_ref[...] = win
    c0 += N_WINCOLS
    sz_ref[...] = _silu(_dot(xb, w_ref[:, c0:c0 + D_ATT]))
    c0 += D_ATT
    gates_ref[...] = jax.nn.sigmoid(_dot(xb, w_ref[:, c0:c0 + LANES]))
    if t_refs:
        rows_t_ref, win_t_ref = t_refs
        rows_t_ref[...] = rows.T
        win_t_ref[...] = win.T


def _nsa_proj(x2, w_all, *, row_tile, seq_len=None):
    n, d = x2.shape
    n_cols = w_all.shape[1]
    widths = (D_ATT, N_ROWCOLS, N_WINCOLS, D_ATT, LANES)
    dtypes = (BF16, F32, F32, F32, F32)
    out_specs = [pl.BlockSpec((row_tile, w), lambda i: (i, 0)) for w in widths]
    out_shape = [jax.ShapeDtypeStruct((n, w), dt) for w, dt in zip(widths, dtypes)]
    if seq_len is not None:
        tiles = seq_len // row_tile
        for w in (N_ROWCOLS, N_WINCOLS):
            out_specs.append(pl.BlockSpec((None, w, row_tile), lambda i: (i // tiles, 0, i % tiles)))
            out_shape.append(jax.ShapeDtypeStruct((n // seq_len, w, seq_len), F32))
    return pl.pallas_call(
        _nsa_proj_kernel,
        grid=(n // row_tile,),
        in_specs=[pl.BlockSpec((row_tile, d), lambda i: (i, 0)),
                  pl.BlockSpec((d, n_cols), lambda i: (0, 0))],
        out_specs=out_specs,
        out_shape=out_shape,
        compiler_params=pltpu.CompilerParams(
            dimension_semantics=("arbitrary",), vmem_limit_bytes=VMEM_LIMIT),
        name="nsa_proj",
    )(x2, w_all)


def _nsa_out_kernel(x_ref, o_ref, sz_ref, w_ref, g_ref, b_ref, y_ref):
    m = (o_ref[...] * sz_ref[...]).astype(BF16)
    y_ref[...] = _ln_residual(x_ref[...], _dot(m, w_ref[...]), g_ref[...], b_ref[...])


def _nsa_out(x2, o2, sz2, w_out, ln_g, ln_b, *, row_tile):
    n, d = x2.shape
    row = lambda i: (i, 0)
    const = lambda i: (0, 0)
    return pl.pallas_call(
        _nsa_out_kernel,
        grid=(n // row_tile,),
        in_specs=[pl.BlockSpec((row_tile, d), row), pl.BlockSpec((row_tile, d), row),
                  pl.BlockSpec((row_tile, d), row), pl.BlockSpec((d, d), const),
                  pl.BlockSpec((1, d), const), pl.BlockSpec((1, d), const)],
        out_specs=pl.BlockSpec((row_tile, d), row),
        out_shape=jax.ShapeDtypeStruct((n, d), F32),
        compiler_params=pltpu.CompilerParams(
            dimension_semantics=("arbitrary",), vmem_limit_bytes=VMEM_LIMIT),
        name="nsa_out",
    )(x2, o2, sz2, w_out, ln_g, ln_b)


def _dup_half(slab, g, lo):
    rolled = pltpu.roll(slab, HEAD_DIM, 1)
    return jnp.where(lo, slab, rolled) if g == 0 else jnp.where(lo, rolled, slab)


def _select_blocks(score, nl, real, n_slc, n_sel, axis):
    rank = jnp.zeros(score.shape, jnp.int32)
    for i in range(n_slc):
        cand = score[:, 2 * i:2 * i + 1] if axis == 1 else score[2 * i:2 * i + 1, :]
        beats = (cand > score) | ((cand == score) & (nl > 2 * i))
        rank = rank + beats.astype(jnp.int32)
    return real & (rank < n_sel)


def _block_scores(imp2, nl, tq, n_slc):
    blk = nl >> 1
    real = ((nl & 1) == 0) & (blk < n_slc)
    cur = tq >> 6
    forced = (blk == 0) | ((blk <= cur) & (blk > cur - N_LOCAL))
    score = jnp.where(forced, FORCE, jnp.where(blk <= cur, imp2, NEG_INF))
    return jnp.where(real, score, -jnp.inf), real


def _alibi_query_cols(head, n_rows):
    lane = lax.broadcasted_iota(jnp.int32, (1, LANES), 1)
    rem = jnp.full((1, LANES), _slope(head) * LOG2E, F32)
    row = jnp.zeros((1, LANES), F32)
    for i in range(3):
        piece = rem.astype(BF16).astype(F32)
        rem = rem - piece
        row = jnp.where(lane == i, 16.0 * piece, jnp.where(lane == 3 + i, piece, row))
    return jnp.broadcast_to(row, (n_rows, LANES)).astype(BF16)


def _alibi_key_cols(n_rows, period):
    r = lax.broadcasted_iota(jnp.int32, (n_rows, LANES), 0) & (period - 1)
    lane = lax.broadcasted_iota(jnp.int32, (n_rows, LANES), 1)
    return jnp.where(lane < 3, (r >> 4).astype(F32), jnp.where(lane < 6, (r & 15).astype(F32), 0.0))


def _nsa_prompt_kernel(q_ref, rows_ref, win_ref, gates_ref, wc_ref, o_ref,
                       ka, vta, kwa, vwta, kcd, vct_lo, vct_hi, qa_ref, selb_ref, acc_ref):
    blk = q_ref.shape[0]
    t_len = rows_ref.shape[0]
    n_cmp = t_len // CMP_BLOCK
    n_slc = -(-t_len // SLC_BLOCK)
    n_sel = min(TOP_N, n_slc)
    n_blk = t_len // blk
    slc_per_blk = blk // SLC_BLOCK
    n_wblk = WINDOW // blk + 1
    sel_rows = -(-2 * n_slc // SUBLANES) * SUBLANES
    v_rows = vta.shape[2]
    qi = pl.program_id(1)
    t0 = qi * blk
    lo = lax.broadcasted_iota(jnp.int32, (1, LANES), 1) < HEAD_DIM

    @pl.when(qi == 0)
    def _build():
        lo_t = lax.broadcasted_iota(jnp.int32, (t_len, LANES), 1) < HEAD_DIM
        lo_c = lax.broadcasted_iota(jnp.int32, (LANES, LANES), 1) < HEAD_DIM
        lo_r = lax.broadcasted_iota(jnp.int32, (LANES, 1), 0) < HEAD_DIM
        vrow = lax.broadcasted_iota(jnp.int32, (v_rows, 1), 0)
        cm = rows_ref[:, 0:2 * LANES]
        kvc = jnp.sum(cm.reshape(n_cmp, CMP_BLOCK, 2 * LANES) * wc_ref[...][None], axis=1)
        kvc = jnp.concatenate([kvc, jnp.zeros((LANES - n_cmp, 2 * LANES), F32)], axis=0)
        key_cols = _alibi_key_cols(t_len, blk).astype(BF16)

        def values_t(v):
            vt = v.T[0:v_rows, :]
            return jnp.where(vrow < HEAD_DIM, vt, jnp.where(vrow == HEAD_DIM, 1.0, 0.0)).astype(BF16)

        for g in range(N_KV):
            kd = _dup_half(rows_ref[:, 2 * LANES:3 * LANES], g, lo_t).astype(BF16)
            ka[g] = jnp.concatenate([kd, key_cols], axis=1)
            vt = values_t(_dup_half(rows_ref[:, 3 * LANES:4 * LANES], g, lo_t))
            kd = _dup_half(win_ref[:, 0:LANES], g, lo_t).astype(BF16)
            kwa[g] = jnp.concatenate([kd, key_cols], axis=1)
            vwt = values_t(_dup_half(win_ref[:, LANES:2 * LANES], g, lo_t))
            for c in range(n_blk):
                vta[g, c] = vt[:, c * blk:(c + 1) * blk]
                vwta[g, c] = vwt[:, c * blk:(c + 1) * blk]
            kcd[g] = _dup_half(kvc[:, 0:LANES], g, lo_c).astype(BF16)
            vct = _dup_half(kvc[:, LANES:2 * LANES], g, lo_c).T
            vct_lo[g] = jnp.where(lo_r, vct, 0.0).astype(BF16)
            vct_hi[g] = jnp.where(lo_r, 0.0, vct).astype(BF16)

    tq = t0 + lax.broadcasted_iota(jnp.int32, (1, blk), 1)
    blk_r = lax.broadcasted_iota(jnp.int32, (LANES, 1), 0)
    cmp_end = blk_r * CMP_BLOCK + (CMP_BLOCK - 1)
    valid_c = (cmp_end <= tq) & (blk_r < n_cmp)
    dist_c = (tq - cmp_end).astype(F32)

    lr = lax.broadcasted_iota(jnp.int32, (1, blk), 1) - lax.broadcasted_iota(jnp.int32, (blk, 1), 0)
    causal_bias = jnp.where(lr >= 0, 0.0, NEG_INF)

    w_start = pl.multiple_of(jnp.maximum(t0 - WINDOW, 0), blk)
    w_blk = w_start // blk
    w_shift, w_bias = [], []
    for i in range(n_wblk):
        shift = t0 - w_start - i * blk
        dist = lr + shift
        w_shift.append(shift.astype(F32))
        w_bias.append(jnp.where((dist >= 0) & (dist < WINDOW), 0.0, NEG_INF))

    gates_t = gates_ref[...].T

    for hh in range(N_HEADS):
        q2 = q_ref[:, (hh // 2) * LANES:(hh // 2 + 1) * LANES]
        zero = jnp.zeros_like(q2)
        qm = jnp.where(lo, q2, zero) if hh % 2 == 0 else jnp.where(lo, zero, q2)
        qa_ref[hh] = jnp.concatenate([qm, _alibi_query_cols(hh, blk)], axis=1)

    o_cmp = []
    for g in range(N_KV):
        imp = jnp.zeros((LANES, blk), F32)
        for j in range(HPG // 2):
            oc = jnp.zeros((LANES, blk), F32)
            for hh, v_ref in ((g * HPG + 2 * j, vct_lo), (g * HPG + 2 * j + 1, vct_hi)):
                s = _dot_nt(kcd[g], qa_ref[hh, :, 0:LANES]) - (_slope(hh) * LOG2E) * dist_c
                s = jnp.where(valid_c, s, NEG_INF)
                e = jnp.exp2(s - jnp.max(s, axis=0, keepdims=True))
                p = jnp.where(valid_c, e, 0.0) / jnp.sum(e, axis=0, keepdims=True)
                imp = imp + p
                oc = oc + _dot(v_ref[g], p.astype(BF16))
            o_cmp.append(oc)

        imp2 = (imp + pltpu.roll(imp, LANES - 1, 0))[0:sel_rows, :]
        score, real = _block_scores(imp2, blk_r[0:sel_rows, :], tq, n_slc)
        sel = _select_blocks(score, blk_r[0:sel_rows, :], real, n_slc, n_sel, axis=0)
        selb_ref[g] = jnp.where(sel, 0.0, NEG_INF)

    acc_ref[...] = jnp.zeros(acc_ref.shape, F32)

    def key_block(c, ms, extra_bias):
        off = pl.multiple_of(c * blk, blk)
        shift = (t0 - off).astype(F32)
        scores, new_ms = [], []
        for g in range(N_KV):
            k = ka[g, pl.ds(off, blk), :]
            rows2 = selb_ref[g, pl.ds(pl.multiple_of(c * (2 * slc_per_blk), 2 * slc_per_blk), 2 * slc_per_blk), :]
            bias = jnp.concatenate([jnp.broadcast_to(rows2[2 * m:2 * m + 1, :], (SLC_BLOCK, blk))
                                    for m in range(slc_per_blk)], axis=0)
            if extra_bias is not None:
                bias = bias + extra_bias
            for hh in range(g * HPG, (g + 1) * HPG):
                s = _dot_nt(k, qa_ref[hh]) + bias
                c_h = (_slope(hh) * LOG2E) * shift
                new_ms.append(jnp.maximum(ms[hh], jnp.max(s, axis=0, keepdims=True) - c_h))
                scores.append(s)
        for hh in range(N_HEADS):
            c_h = (_slope(hh) * LOG2E) * shift
            p = jnp.exp2(scores[hh] - (new_ms[hh] + c_h))
            acc_ref[hh] = (acc_ref[hh] * jnp.exp2(ms[hh] - new_ms[hh])
                           + _dot(vta[hh // HPG, c], p.astype(BF16)))
        return tuple(new_ms)

    ms = lax.fori_loop(0, qi, lambda c, ms: key_block(c, ms, None),
                       (jnp.full((1, blk), NEG_INF, F32),) * N_HEADS)
    key_block(qi, ms, causal_bias)

    for g in range(N_KV):
        k_win = [kwa[g, pl.ds(pl.multiple_of(w_start + i * blk, blk), blk), :] for i in range(n_wblk)]
        w_scores, w_max = [], []
        for hh in range(g * HPG, (g + 1) * HPG):
            slope = _slope(hh) * LOG2E
            tiles, m_w = [], None
            for i in range(n_wblk):
                s = _dot_nt(k_win[i], qa_ref[hh]) + w_bias[i]
                m_i = jnp.max(s, axis=0, keepdims=True) - slope * w_shift[i]
                m_w = m_i if m_w is None else jnp.maximum(m_w, m_i)
                tiles.append(s)
            w_scores.append(tiles)
            w_max.append(m_w)
        o_wins = []
        for h in range(HPG):
            slope = _slope(g * HPG + h) * LOG2E
            acc_w = jnp.zeros((v_rows, blk), F32)
            for i in range(n_wblk):
                p = jnp.exp2(w_scores[h][i] - (w_max[h] + slope * w_shift[i]))
                acc_w = acc_w + _dot(vwta[g, w_blk + i], p.astype(BF16))
            o_wins.append(acc_w[0:HEAD_DIM, :] / acc_w[HEAD_DIM:HEAD_DIM + 1, :])

        for j in range(HPG // 2):
            outs = []
            for hh in (g * HPG + 2 * j, g * HPG + 2 * j + 1):
                o_win = o_wins[hh - g * HPG]
                acc = acc_ref[hh]
                o_sel = acc[0:HEAD_DIM, :] / acc[HEAD_DIM:HEAD_DIM + 1, :]
                half = (hh % 2) * HEAD_DIM
                gate = lambda c: gates_t[c * N_HEADS + hh:c * N_HEADS + hh + 1, :]
                o_c = o_cmp[g * (HPG // 2) + j][half:half + HEAD_DIM, :]
                outs.append(gate(0) * o_c + gate(1) * o_sel + gate(2) * o_win)
            c0 = g * HPG * HEAD_DIM + j * LANES
            o_ref[:, c0:c0 + LANES] = jnp.concatenate(outs, axis=0).T


def _block_expand_matrix(n_rows, n_keys):
    n = jnp.arange(n_rows, dtype=jnp.int32)[:, None]
    p = jnp.arange(n_keys, dtype=jnp.int32)[None, :]
    return (n == 2 * (p // SLC_BLOCK)).astype(BF16)


def _nsa_prompt_attention(q, rows, win, gates, wc, *, q_tile=256):
    bsz, t_len, _ = q.shape
    assert WINDOW % q_tile == 0 and q_tile % (2 * LANES) == 0 and t_len >= WINDOW + q_tile
    assert t_len // CMP_BLOCK <= LANES
    v_rows = HEAD_DIM + 16
    k_scratch = pltpu.VMEM((N_KV, t_len, 2 * LANES), BF16)
    v_scratch = pltpu.VMEM((N_KV, t_len // q_tile, v_rows, q_tile), BF16)
    c_scratch = pltpu.VMEM((N_KV, LANES, LANES), BF16)
    return pl.pallas_call(
        _nsa_prompt_kernel,
        grid=(bsz, t_len // q_tile),
        in_specs=[
            pl.BlockSpec((None, q_tile, D_ATT), lambda b, i: (b, i, 0)),
            pl.BlockSpec((None, t_len, N_ROWCOLS), lambda b, i: (b, 0, 0)),
            pl.BlockSpec((None, t_len, N_WINCOLS), lambda b, i: (b, 0, 0)),
            pl.BlockSpec((None, q_tile, LANES), lambda b, i: (b, i, 0)),
            pl.BlockSpec((CMP_BLOCK, 2 * LANES), lambda b, i: (0, 0)),
        ],
        out_specs=pl.BlockSpec((None, q_tile, D_ATT), lambda b, i: (b, i, 0)),
        out_shape=jax.ShapeDtypeStruct((bsz, t_len, D_ATT), F32),
        scratch_shapes=[k_scratch, v_scratch, k_scratch, v_scratch,
                        c_scratch, c_scratch, c_scratch,
                        pltpu.VMEM((N_HEADS, q_tile, 2 * LANES), BF16),
                        pltpu.VMEM((N_KV, -(-2 * (-(-t_len // SLC_BLOCK)) // SUBLANES) * SUBLANES, q_tile), F32),
                        pltpu.VMEM((N_HEADS, v_rows, q_tile), F32)],
        compiler_params=pltpu.CompilerParams(
            dimension_semantics=("arbitrary", "arbitrary"), vmem_limit_bytes=VMEM_LIMIT),
        name="nsa_prompt_attention",
    )(q, rows, win, gates, wc)


def _nsa_sample_kernel(pt_ref, *refs, pages_per_step, n_steps, n_new):
    page_refs = refs[:pages_per_step]
    (qbd_ref, new_ref, winbuf_ref, gates_ref, wc_ref, e_ref, o_ref, s_ref, vst_ref, kcs, vcs) = refs[pages_per_step:]
    del pt_ref
    step = pl.program_id(1)
    keys_per_step = pages_per_step * PAGE_SIZE
    p_len = n_steps * keys_per_step
    n_keys = p_len + LANES
    n_cmp = p_len // CMP_BLOCK
    n_slc = -(-(p_len + n_new) // SLC_BLOCK)
    n_sel = min(TOP_N, n_slc)
    cmp_per_step = keys_per_step // CMP_BLOCK
    n_rows = N_KV * n_new * HPG
    gd = N_KV * HEAD_DIM

    def slab(r, kind):
        return r[kind].reshape(gd, r.shape[-1])

    cm = jnp.concatenate([jnp.concatenate([slab(r, 0).T, slab(r, 1).T], axis=1) for r in page_refs], axis=0)
    kvc = jnp.sum(cm.reshape(cmp_per_step, CMP_BLOCK, 2 * LANES) * wc_ref[...][None], axis=1)
    c_off = pl.multiple_of(step * cmp_per_step, cmp_per_step)
    kcs[pl.ds(c_off, cmp_per_step), :] = kvc[:, 0:LANES].astype(BF16)
    vcs[pl.ds(c_off, cmp_per_step), :] = kvc[:, LANES:2 * LANES].astype(BF16)
    s_ref[step] = _dot(qbd_ref[...], jnp.concatenate([slab(r, 2).astype(BF16) for r in page_refs], axis=1))
    vst_ref[step] = jnp.concatenate([slab(r, 3).astype(BF16) for r in page_refs], axis=1)

    @pl.when(step == n_steps - 1)
    def _attend():
        n_wb = winbuf_ref.shape[-1]
        w_keys = n_wb + LANES
        qbd = qbd_ref[...]
        r = lax.broadcasted_iota(jnp.int32, (n_rows, 1), 0)
        head = (r // (n_new * HPG)) * HPG + (r % HPG)
        slope = jnp.exp2((head + 1).astype(F32) * (-8.0 / N_HEADS)) * LOG2E
        tq = p_len + (r // HPG) % n_new

        def attend(s, values_t, dist, bias):
            s = s - slope * dist.astype(F32) + bias
            e = jnp.exp2(s - jnp.max(s, axis=-1, keepdims=True))
            eb = e.astype(BF16)
            o = sum(_dot_nt(eb[:, ks], vt) for ks, vt in values_t)
            return o / jnp.sum(e, axis=-1, keepdims=True)

        nl = lax.broadcasted_iota(jnp.int32, (n_rows, n_cmp), 1)
        cmp_end = nl * CMP_BLOCK + (CMP_BLOCK - 1)
        valid_c = cmp_end <= tq
        s = _dot_nt(qbd, kcs[...]) - slope * (tq - cmp_end).astype(F32)
        s = jnp.where(valid_c, s, NEG_INF)
        e = jnp.exp2(s - jnp.max(s, axis=-1, keepdims=True))
        p = jnp.where(valid_c, e, 0.0) / jnp.sum(e, axis=-1, keepdims=True)
        o_cmp = _dot(p.astype(BF16), vcs[...])

        n_gt = N_KV * n_new
        imp = jnp.sum(p.reshape(n_gt, HPG, n_cmp), axis=1)
        imp2 = imp + pltpu.roll(imp, n_cmp - 1, 1)
        w_sel = e_ref.shape[0]
        imp2 = jnp.concatenate([imp2, jnp.zeros((n_gt, w_sel - n_cmp), F32)], axis=1)
        nl8 = lax.broadcasted_iota(jnp.int32, (n_gt, w_sel), 1)
        r8 = lax.broadcasted_iota(jnp.int32, (n_gt, 1), 0)
        score, real = _block_scores(imp2, nl8, p_len + r8 % n_new, n_slc)
        sel = _select_blocks(score, nl8, real, n_slc, n_sel, axis=1)
        selb = jnp.where(sel, 1.0, 0.0)
        selb = jnp.broadcast_to(selb[:, None, :], (n_gt, HPG, w_sel)).reshape(n_rows, w_sel)
        hit = _dot(selb.astype(BF16), e_ref[...])
        dist = tq - lax.broadcasted_iota(jnp.int32, (n_rows, n_keys), 1)
        new_k, new_v, new_kw, new_vw = (new_ref[i] for i in range(4))
        s_sel = jnp.concatenate([s_ref[i] for i in range(n_steps)] + [_dot(qbd, new_k)], axis=1)
        values_t = [(slice(i * keys_per_step, (i + 1) * keys_per_step), vst_ref[i]) for i in range(n_steps)]
        o_sel = attend(s_sel, values_t + [(slice(p_len, n_keys), new_v)], dist,
                       jnp.where((hit > 0.5) & (dist >= 0), 0.0, NEG_INF))

        dist_w = tq - (p_len - n_wb) - lax.broadcasted_iota(jnp.int32, (n_rows, w_keys), 1)
        s_win = jnp.concatenate([_dot(qbd, slab(winbuf_ref, 0).astype(BF16)), _dot(qbd, new_kw)], axis=1)
        o_win = attend(s_win, [(slice(0, n_wb), slab(winbuf_ref, 1).astype(BF16)), (slice(n_wb, w_keys), new_vw)],
                       dist_w, jnp.where((dist_w >= 0) & (dist_w < WINDOW), 0.0, NEG_INF))

        o_ref[...] = (gates_ref[:, 0:1] * o_cmp + gates_ref[:, 1:2] * o_sel + gates_ref[:, 2:3] * o_win)


def _nsa_sample_attention(page_table, cache_t, qbd, new_t, winbuf_t, gates, wc, *, n_new, pages_per_step=16):
    bsz, n_pages = page_table.shape
    n_steps = n_pages // pages_per_step
    keys_per_step = pages_per_step * PAGE_SIZE
    p_len = n_pages * PAGE_SIZE
    n_rows = qbd.shape[1]
    n_wb = winbuf_t.shape[-1]
    n_keys = p_len + LANES
    n_cmp = p_len // CMP_BLOCK
    n_slc = -(-(p_len + n_new) // SLC_BLOCK)
    w_sel = -(-2 * n_slc // LANES) * LANES
    gd = N_KV * HEAD_DIM
    e_mat = _block_expand_matrix(w_sel, n_keys)
    kern = functools.partial(_nsa_sample_kernel, pages_per_step=pages_per_step, n_steps=n_steps, n_new=n_new)

    def page_spec(k):
        return pl.BlockSpec((None, 4, N_KV, HEAD_DIM, PAGE_SIZE),
                            lambda b, s, pt: (pt[b, s * pages_per_step + k], 0, 0, 0, 0))

    per_b = lambda b, s, pt: (b, 0, 0)
    const = lambda b, s, pt: (0, 0)
    grid_spec = pltpu.PrefetchScalarGridSpec(
        num_scalar_prefetch=1,
        grid=(bsz, n_steps),
        in_specs=[page_spec(k) for k in range(pages_per_step)] + [
            pl.BlockSpec((None, n_rows, LANES), per_b),
            pl.BlockSpec((None, 4, gd, LANES), lambda b, s, pt: (b, 0, 0, 0)),
            pl.BlockSpec((None, 2, N_KV, HEAD_DIM, n_wb), lambda b, s, pt: (b, 0, 0, 0, 0)),
            pl.BlockSpec((None, n_rows, LANES), per_b),
            pl.BlockSpec((CMP_BLOCK, 2 * LANES), const),
            pl.BlockSpec((w_sel, n_keys), const),
        ],
        out_specs=pl.BlockSpec((None, n_rows, LANES), per_b),
        scratch_shapes=[
            pltpu.VMEM((n_steps, n_rows, keys_per_step), F32),
            pltpu.VMEM((n_steps, gd, keys_per_step), BF16),
            pltpu.VMEM((n_cmp, LANES), BF16), pltpu.VMEM((n_cmp, LANES), BF16),
        ],
    )
    return pl.pallas_call(
        kern,
        grid_spec=grid_spec,
        out_shape=jax.ShapeDtypeStruct((bsz, n_rows, LANES), F32),
        compiler_params=pltpu.CompilerParams(
            dimension_semantics=("arbitrary", "arbitrary"), vmem_limit_bytes=VMEM_LIMIT),
        name="nsa_sample_attention",
    )(page_table, *([cache_t] * pages_per_step), qbd, new_t, winbuf_t, gates, wc, e_mat)


def _nsa_weights(w_in):
    d = w_in.shape[0]
    c_kv = D_ATT
    c_g = c_kv + N_ROWCOLS + N_WINCOLS
    c_z = c_g + N_GATES
    pad = jnp.zeros((d, LANES - N_GATES), w_in.dtype)
    return jnp.concatenate([w_in[:, :c_g], w_in[:, c_z:], w_in[:, c_g:c_z], pad], axis=1).astype(BF16)


def _cmp_weight_tile(w_cmp):
    return w_cmp.transpose(1, 0, 2, 3).reshape(CMP_BLOCK, 2 * N_KV * HEAD_DIM)


def _nsa_layer_prompt(x, w_all, wc, w_out, ln_g, ln_b):
    bsz, t_len, d = x.shape
    x2 = x.reshape(bsz * t_len, d)
    q, rows, win, sz, gates, rows_t, win_t = _nsa_proj(x2, w_all, row_tile=512, seq_len=t_len)
    o = _nsa_prompt_attention(q.reshape(bsz, t_len, D_ATT), rows.reshape(bsz, t_len, N_ROWCOLS),
                              win.reshape(bsz, t_len, N_WINCOLS), gates.reshape(bsz, t_len, LANES), wc)
    y = _nsa_out(x2, o.reshape(bsz * t_len, D_ATT), sz, w_out, ln_g, ln_b, row_tile=512)
    keep = min(WINDOW, t_len)
    rows_out = rows_t.reshape(bsz, 4, N_KV, HEAD_DIM, t_len).transpose(0, 4, 1, 2, 3)
    win_out = win_t[:, :, t_len - keep:].reshape(bsz, 2, N_KV, HEAD_DIM, keep).transpose(0, 4, 1, 2, 3)
    return y.reshape(bsz, t_len, d), rows_out, win_out


def _nsa_layer_sample(x, cache, win_buf, page_table, w_all, wc, w_out, ln_g, ln_b):
    bsz, n_new, d = x.shape
    x2 = x.reshape(bsz * n_new, d)
    q, rows, win, sz, gates = _nsa_proj(x2, w_all, row_tile=bsz * n_new)
    q5 = q.reshape(bsz, n_new, N_KV, HPG, HEAD_DIM).transpose(0, 2, 1, 3, 4)
    eye = jnp.eye(N_KV, dtype=q.dtype)
    qbd = (q5[:, :, :, :, None, :] * eye[None, :, None, None, :, None]).reshape(
        bsz, N_KV * n_new * HPG, N_KV * HEAD_DIM)
    g5 = gates[:, :N_GATES].reshape(bsz, n_new, 3, N_KV, HPG).transpose(0, 3, 1, 4, 2)
    g_rows = jnp.pad(g5.reshape(bsz, N_KV * n_new * HPG, 3), ((0, 0), (0, 0), (0, LANES - 3)))
    rows3 = rows.reshape(bsz, n_new, N_ROWCOLS)
    win3 = win.reshape(bsz, n_new, N_WINCOLS)
    n_wb = win_buf.shape[1]
    gd = N_KV * HEAD_DIM
    new_t = jnp.stack([rows3[:, :, 2 * gd:3 * gd], rows3[:, :, 3 * gd:4 * gd],
                       win3[:, :, 0:gd], win3[:, :, gd:2 * gd]], axis=1).transpose(0, 1, 3, 2)
    new_t = jnp.pad(new_t, ((0, 0), (0, 0), (0, 0), (0, LANES - n_new))).astype(BF16)
    o_rows = _nsa_sample_attention(
        page_table, cache.transpose(0, 2, 3, 4, 1), qbd, new_t, win_buf.transpose(0, 2, 3, 4, 1),
        g_rows, wc, n_new=n_new)
    o6 = o_rows.reshape(bsz, N_KV, n_new, HPG, N_KV, HEAD_DIM)
    o = jnp.stack([o6[:, g, :, :, g, :] for g in range(N_KV)], axis=2).reshape(bsz * n_new, D_ATT)
    y = _nsa_out(x2, o, sz, w_out, ln_g, ln_b, row_tile=bsz * n_new)
    all_win = jnp.concatenate([win_buf, win3.reshape(bsz, n_new, 2, N_KV, HEAD_DIM)], axis=1)
    return (y.reshape(bsz, n_new, d), rows3.reshape(bsz, n_new, 4, N_KV, HEAD_DIM),
            all_win[:, all_win.shape[1] - n_wb:])


def _time_major(a):
    bsz, k, d = a.shape
    return a.transpose(1, 0, 2).reshape(1, k * bsz, d)


def _batch_major(a, bsz):
    _, kb, d = a.shape
    return a.reshape(kb // bsz, bsz, d).transpose(1, 0, 2)


def kernel(x_prompt, x_sample, cache_nsa_kv, cache_nsa_win, state_conv_a, state_lru_h, state_lru_conv, page_table,
           ln_g, ln_b, a_w_in, a_conv_w, a_w_out, b_w_in, b_w_cmp, b_w_out,
           c_w_in, c_conv_w, c_conv_b, c_w_a, c_b_a, c_w_x, c_b_x, c_lam, c_w_out):
    bp, t_len, d = x_prompt.shape
    bs, n_new, _ = x_sample.shape
    yp = x_prompt
    ys = _time_major(x_sample)
    conv_a_p, conv_a_s = [], []
    rows_p, rows_s, win_p, win_s = [], [], [], []
    h_p, h_s, cc_p, cc_s = [], [], [], []
    row2 = lambda v: v.reshape(1, -1)
    for i in range(DEPTH):
        kind, j = i % 3, i // 3
        g, b = row2(ln_g[i]), row2(ln_b[i])
        if kind == 0:
            w_in, w_out = a_w_in[j].astype(BF16), a_w_out[j].astype(BF16)
            taps = a_conv_w.shape[1]
            yp, sp = _conv_layer(yp, jnp.zeros((bp, taps - 1, d), F32), w_in, a_conv_w[j], w_out, g, b,
                                 stride=1, row_tile=512)
            ys, ss = _conv_layer(ys, _time_major(state_conv_a[j]), w_in, a_conv_w[j], w_out, g, b,
                                 stride=bs, row_tile=n_new * bs)
            conv_a_p.append(sp)
            conv_a_s.append(_batch_major(ss, bs))
        elif kind == 1:
            w_all = _nsa_weights(b_w_in[j])
            wc = _cmp_weight_tile(b_w_cmp[j])
            w_out = b_w_out[j].astype(BF16)
            yp, rp, wp = _nsa_layer_prompt(yp, w_all, wc, w_out, g, b)
            ys_b, rs, ws = _nsa_layer_sample(_batch_major(ys, bs), cache_nsa_kv[j], cache_nsa_win[j], page_table,
                                             w_all, wc, w_out, g, b)
            ys = _time_major(ys_b)
            rows_p.append(rp)
            rows_s.append(rs)
            win_p.append(wp)
            win_s.append(ws)
        else:
            w_in, w_out = c_w_in[j].astype(BF16), c_w_out[j].astype(BF16)
            w_a, w_x = c_w_a[j].astype(BF16), c_w_x[j].astype(BF16)
            taps = c_conv_w.shape[1]
            args = (w_in, c_conv_w[j], row2(c_conv_b[j]), w_a, row2(c_b_a[j]), w_x, row2(c_b_x[j]),
                    row2(c_lam[j]), w_out, g, b)
            yp, hp, cp = _lru_layer(yp, jnp.zeros((bp, 1, d), F32), jnp.zeros((bp, taps - 1, d), F32), *args,
                                    stride=1, row_tile=256)
            ys, hs, cs = _lru_layer(ys, state_lru_h[j].reshape(1, bs, d), _time_major(state_lru_conv[j]), *args,
                                    stride=bs, row_tile=n_new * bs)
            h_p.append(hp.reshape(bp, d))
            h_s.append(hs.reshape(bs, d))
            cc_p.append(cp)
            cc_s.append(_batch_major(cs, bs))
    return (yp, _batch_major(ys, bs), jnp.stack(conv_a_p), jnp.stack(conv_a_s), jnp.stack(rows_p), jnp.stack(rows_s),
            jnp.stack(win_p), jnp.stack(win_s), jnp.stack(h_p), jnp.stack(h_s), jnp.stack(cc_p), jnp.stack(cc_s))
```

```python
import functools

import jax
import jax.numpy as jnp
from jax import lax
from jax.experimental import pallas as pl
from jax.experimental.pallas import tpu as pltpu

F32 = jnp.float32
BF16 = jnp.bfloat16

DEPTH = 4
N_HEADS = 16
N_KV = 2
HPG = N_HEADS // N_KV
HEAD_DIM = 64
CMP_BLOCK = 32
SLC_BLOCK = 64
TOP_N = 16
N_LOCAL = 2
WINDOW = 512
PAGE_SIZE = 128
NEG_INF = -1e30
FORCE = 1e9
RG_C = 8.0
N_RG_BLOCKS = 4
ALPHA = (2 * DEPTH) ** 0.25
LN_EPS = 1e-5
LOG2E = 1.4426950408889634

LANES = 128
SUBLANES = 8
VMEM_LIMIT = 56 * 1024 * 1024


def _dot(a, b):
    return jnp.dot(a, b, preferred_element_type=F32)


def _dot_nt(a, b):
    return lax.dot_general(a, b, (((1,), (1,)), ((), ())), preferred_element_type=F32)


def _silu(z):
    return z * jax.nn.sigmoid(z)


def _ln_residual(x, y, g, b):
    r = ALPHA * x + y
    mu = jnp.mean(r, axis=-1, keepdims=True)
    c = r - mu
    var = jnp.mean(c * c, axis=-1, keepdims=True)
    return c * lax.rsqrt(var + LN_EPS) * g + b


def _slope(head):
    return 2.0 ** (-8.0 * (head + 1) / N_HEADS)


def _prev_rows(u, carry, k):
    n_carry = carry.shape[0]
    rolled = pltpu.roll(u, k, 0)
    row = lax.broadcasted_iota(jnp.int32, (SUBLANES, 1), 0)
    head = rolled[0:SUBLANES, :]
    for i in range(k):
        head = jnp.where(row == i, carry[n_carry - k + i:n_carry - k + i + 1, :], head)
    return jnp.concatenate([head, rolled[SUBLANES:, :]], axis=0)


def _conv_layer_kernel(x_ref, buf_ref, win_ref, cw_ref, wout_ref, g_ref, b_ref,
                       y_ref, nbuf_ref, carry_ref, *, stride, col_chunk):
    rows, d = x_ref.shape
    taps = cw_ref.shape[0]
    x = x_ref[...]
    xb = x.astype(BF16)
    if stride == 1:
        @pl.when(pl.program_id(1) == 0)
        def _():
            carry_ref[...] = buf_ref[...]
    acc = jnp.zeros((rows, d), F32)
    for j in range(d // col_chunk):
        lo = j * col_chunk
        sl = slice(lo, lo + col_chunk)
        h = _dot(xb, win_ref[:, lo:lo + col_chunk])
        bg = _dot(xb, win_ref[:, d + lo:d + lo + col_chunk])
        cg = _dot(xb, win_ref[:, 2 * d + lo:2 * d + lo + col_chunk])
        z = _dot(xb, win_ref[:, 3 * d + lo:3 * d + lo + col_chunk])
        u = cg * h
        conv = u * cw_ref[taps - 1:taps, sl]
        if stride == 1:
            carry = carry_ref[:, sl]
            for k in range(1, taps):
                conv = conv + _prev_rows(u, carry, k) * cw_ref[taps - 1 - k:taps - k, sl]
            new_buf = u[rows - (taps - 1):rows, :]
            carry_ref[:, sl] = new_buf
        else:
            ext = jnp.concatenate([buf_ref[:, sl], u], axis=0)
            for k in range(1, taps):
                s0 = (taps - 1 - k) * stride
                conv = conv + ext[s0:s0 + rows, :] * cw_ref[taps - 1 - k:taps - k, sl]
            new_buf = ext[rows:rows + (taps - 1) * stride, :]
        nbuf_ref[:, sl] = new_buf
        m = _silu(z) * bg * conv
        acc = acc + _dot(m.astype(BF16), wout_ref[sl, :])
    y_ref[...] = _ln_residual(x, acc, g_ref[...], b_ref[...])


def _conv_layer(x, buf, w_in, conv_w, w_out, ln_g, ln_b, *, stride, row_tile):
    bsz, t_len, d = x.shape
    n_buf = buf.shape[1]
    taps = conv_w.shape[0]
    grid = (bsz, t_len // row_tile)
    kern = functools.partial(_conv_layer_kernel, stride=stride, col_chunk=512)
    return pl.pallas_call(
        kern,
        grid=grid,
        in_specs=[
            pl.BlockSpec((None, row_tile, d), lambda b, t: (b, t, 0)),
            pl.BlockSpec((None, n_buf, d), lambda b, t: (b, 0, 0)),
            pl.BlockSpec((d, 4 * d), lambda b, t: (0, 0)),
            pl.BlockSpec((taps, d), lambda b, t: (0, 0)),
            pl.BlockSpec((d, d), lambda b, t: (0, 0)),
            pl.BlockSpec((1, d), lambda b, t: (0, 0)),
            pl.BlockSpec((1, d), lambda b, t: (0, 0)),
        ],
        out_specs=[
            pl.BlockSpec((None, row_tile, d), lambda b, t: (b, t, 0)),
            pl.BlockSpec((None, n_buf, d), lambda b, t: (b, 0, 0)),
        ],
        out_shape=[
            jax.ShapeDtypeStruct((bsz, t_len, d), F32),
            jax.ShapeDtypeStruct((bsz, n_buf, d), F32),
        ],
        scratch_shapes=[pltpu.VMEM((taps - 1, d), F32)],
        compiler_params=pltpu.CompilerParams(
            dimension_semantics=("arbitrary", "arbitrary"), vmem_limit_bytes=VMEM_LIMIT),
        name="conv_layer",
    )(x, buf, w_in, conv_w, w_out, ln_g, ln_b)


def _scan_rows(a_list, b_list, h_prev_list):
    rows, cols = a_list[0].shape
    groups = rows // SUBLANES
    a3 = [a.reshape(groups, SUBLANES, cols) for a in a_list]
    b3 = [b.reshape(groups, SUBLANES, cols) for b in b_list]
    sub = lax.broadcasted_iota(jnp.int32, (1, SUBLANES, 1), 1)
    s = 1
    while s < SUBLANES:
        live = sub >= s
        for n, (a, b) in enumerate(zip(a3, b3)):
            a_sh = jnp.where(live, pltpu.roll(a, s, 1), 1.0)
            b_sh = jnp.where(live, pltpu.roll(b, s, 1), 0.0)
            b3[n] = a * b_sh + b
            a3[n] = a * a_sh
        s *= 2
    carries = list(h_prev_list)
    outs = [[] for _ in a3]
    for j in range(groups):
        for n in range(len(a3)):
            h = a3[n][j] * carries[n] + b3[n][j]
            carries[n] = h[SUBLANES - 1:SUBLANES, :]
            outs[n].append(h)
    return [jnp.concatenate(o, axis=0) for o in outs]


def _lru_layer_kernel(x_ref, h0_ref, buf_ref, win_ref, cw_ref, cb_ref, wa_ref, ba_ref, wx_ref, bx_ref,
                      lam_ref, wout_ref, g_ref, b_ref,
                      y_ref, hlast_ref, nbuf_ref, carry_ref, hcarry_ref, *, stride):
    rows, d = x_ref.shape
    taps = cw_ref.shape[0]
    n_blocks, blk, _ = wa_ref.shape
    x = x_ref[...]
    xb = x.astype(BF16)
    if stride == 1:
        @pl.when(pl.program_id(1) == 0)
        def _():
            carry_ref[...] = buf_ref[...]
            hcarry_ref[...] = h0_ref[...]
    slices = [slice(n * blk, (n + 1) * blk) for n in range(n_blocks)]
    ucs = []
    for sl in slices:
        u = _dot(xb, win_ref[:, sl])
        uc = u * cw_ref[taps - 1:taps, sl] + cb_ref[:, sl]
        if stride == 1:
            carry = carry_ref[:, sl]
            for k in range(1, taps):
                uc = uc + _prev_rows(u, carry, k) * cw_ref[taps - 1 - k:taps - k, sl]
            new_buf = u[rows - (taps - 1):rows, :]
            carry_ref[:, sl] = new_buf
        else:
            ext = jnp.concatenate([buf_ref[:, sl], u], axis=0)
            for k in range(1, taps):
                s0 = (taps - 1 - k) * stride
                uc = uc + ext[s0:s0 + rows, :] * cw_ref[taps - 1 - k:taps - k, sl]
            new_buf = ext[rows:rows + (taps - 1) * stride, :]
        nbuf_ref[:, sl] = new_buf
        ucs.append(uc)
    a_list, b_list = [], []
    for n, sl in enumerate(slices):
        ucb = ucs[n].astype(BF16)
        r = jax.nn.sigmoid(_dot(ucb, wa_ref[n]) + ba_ref[:, sl])
        i = jax.nn.sigmoid(_dot(ucb, wx_ref[n]) + bx_ref[:, sl])
        nl = -lam_ref[:, sl]
        softplus = jnp.maximum(nl, 0.0) + jnp.log1p(jnp.exp(-jnp.abs(nl)))
        log_a = (-RG_C * softplus) * r
        a = jnp.exp(log_a)
        y = -jnp.tanh(log_a) * (a * a + 1.0)
        a_list.append(a)
        b_list.append(jnp.exp2(0.5 * jnp.log2(y)) * (i * ucs[n]))
    if stride == 1:
        hs = _scan_rows(a_list, b_list, [hcarry_ref[:, sl] for sl in slices])
        for n, sl in enumerate(slices):
            hcarry_ref[:, sl] = hs[n][rows - 1:rows, :]
            hlast_ref[:, sl] = hs[n][rows - 1:rows, :]
    else:
        hs = []
        for n, sl in enumerate(slices):
            h_t = h0_ref[:, sl]
            steps = []
            for t in range(rows // stride):
                rs = slice(t * stride, (t + 1) * stride)
                h_t = a_list[n][rs, :] * h_t + b_list[n][rs, :]
                steps.append(h_t)
            hs.append(jnp.concatenate(steps, axis=0))
            hlast_ref[:, sl] = h_t
    acc = jnp.zeros((rows, d), F32)
    for n, sl in enumerate(slices):
        z = _dot(xb, win_ref[:, d + n * blk:d + (n + 1) * blk])
        acc = acc + _dot((_silu(z) * hs[n]).astype(BF16), wout_ref[sl, :])
    y_ref[...] = _ln_residual(x, acc, g_ref[...], b_ref[...])


def _lru_layer(x, h0, buf, w_in, conv_w, conv_b, w_a, b_a, w_x, b_x, lam, w_out, ln_g, ln_b, *, stride, row_tile):
    bsz, t_len, d = x.shape
    n_buf = buf.shape[1]
    n_h = h0.shape[1]
    taps = conv_w.shape[0]
    nb, blk, _ = w_a.shape
    grid = (bsz, t_len // row_tile)
    const2 = lambda b, t: (0, 0)
    kern = functools.partial(_lru_layer_kernel, stride=stride)
    return pl.pallas_call(
        kern,
        grid=grid,
        in_specs=[
            pl.BlockSpec((None, row_tile, d), lambda b, t: (b, t, 0)),
            pl.BlockSpec((None, n_h, d), lambda b, t: (b, 0, 0)),
            pl.BlockSpec((None, n_buf, d), lambda b, t: (b, 0, 0)),
            pl.BlockSpec((d, 2 * d), const2),
            pl.BlockSpec((taps, d), const2),
            pl.BlockSpec((1, d), const2),
            pl.BlockSpec((nb, blk, blk), lambda b, t: (0, 0, 0)),
            pl.BlockSpec((1, d), const2),
            pl.BlockSpec((nb, blk, blk), lambda b, t: (0, 0, 0)),
            pl.BlockSpec((1, d), const2),
            pl.BlockSpec((1, d), const2),
            pl.BlockSpec((d, d), const2),
            pl.BlockSpec((1, d), const2),
            pl.BlockSpec((1, d), const2),
        ],
        out_specs=[
            pl.BlockSpec((None, row_tile, d), lambda b, t: (b, t, 0)),
            pl.BlockSpec((None, n_h, d), lambda b, t: (b, 0, 0)),
            pl.BlockSpec((None, n_buf, d), lambda b, t: (b, 0, 0)),
        ],
        out_shape=[
            jax.ShapeDtypeStruct((bsz, t_len, d), F32),
            jax.ShapeDtypeStruct((bsz, n_h, d), F32),
            jax.ShapeDtypeStruct((bsz, n_buf, d), F32),
        ],
        scratch_shapes=[pltpu.VMEM((taps - 1, d), F32), pltpu.VMEM((1, d), F32)],
        compiler_params=pltpu.CompilerParams(
            dimension_semantics=("arbitrary", "arbitrary"), vmem_limit_bytes=VMEM_LIMIT),
        name="lru_layer",
    )(x, h0, buf, w_in, conv_w, conv_b, w_a, b_a, w_x, b_x, lam, w_out, ln_g, ln_b)


D_ATT = N_HEADS * HEAD_DIM
N_ROWCOLS = 4 * N_KV * HEAD_DIM
N_WINCOLS = 2 * N_KV * HEAD_DIM
N_GATES = 3 * N_HEADS


def _nsa_proj_kernel(x_ref, w_ref, q_ref, rows_ref, win_ref, sz_ref, gates_ref, *t_refs):
    xb = x_ref[...].astype(BF16)
    c0 = 0
    q_ref[...] = (_dot(xb, w_ref[:, c0:c0 + D_ATT]) * (HEAD_DIM ** -0.5 * LOG2E)).astype(BF16)
    c0 += D_ATT
    rows = _dot(xb, w_ref[:, c0:c0 + N_ROWCOLS])
    rows_ref[...] = rows
    c0 += N_ROWCOLS
    win = _dot(xb, w_ref[:, c0:c0 + N_WINCOLS])
    win_ref[...] = win
    c0 += N_WINCOLS
    sz_ref[...] = _silu(_dot(xb, w_ref[:, c0:c0 + D_ATT])).astype(sz_ref.dtype)
    c0 += D_ATT
    gates_ref[...] = jax.nn.sigmoid(_dot(xb, w_ref[:, c0:c0 + LANES]))
    if t_refs:
        rows_t_ref, win_t_ref = t_refs
        rows_t_ref[...] = rows.T
        win_t_ref[...] = win.T


def _nsa_proj(x2, w_all, *, row_tile, seq_len=None):
    n, d = x2.shape
    n_cols = w_all.shape[1]
    widths = (D_ATT, N_ROWCOLS, N_WINCOLS, D_ATT, LANES)
    dtypes = (BF16, F32, F32, BF16, F32)
    out_specs = [pl.BlockSpec((row_tile, w), lambda i: (i, 0)) for w in widths]
    out_shape = [jax.ShapeDtypeStruct((n, w), dt) for w, dt in zip(widths, dtypes)]
    if seq_len is not None:
        tiles = seq_len // row_tile
        for w in (N_ROWCOLS, N_WINCOLS):
            out_specs.append(pl.BlockSpec((None, w, row_tile), lambda i: (i // tiles, 0, i % tiles)))
            out_shape.append(jax.ShapeDtypeStruct((n // seq_len, w, seq_len), F32))
    return pl.pallas_call(
        _nsa_proj_kernel,
        grid=(n // row_tile,),
        in_specs=[pl.BlockSpec((row_tile, d), lambda i: (i, 0)),
                  pl.BlockSpec((d, n_cols), lambda i: (0, 0))],
        out_specs=out_specs,
        out_shape=out_shape,
        compiler_params=pltpu.CompilerParams(
            dimension_semantics=("arbitrary",), vmem_limit_bytes=VMEM_LIMIT),
        name="nsa_proj",
    )(x2, w_all)


def _nsa_out_kernel(x_ref, o_ref, sz_ref, w_ref, g_ref, b_ref, y_ref):
    m = (o_ref[...].astype(F32) * sz_ref[...].astype(F32)).astype(BF16)
    y_ref[...] = _ln_residual(x_ref[...], _dot(m, w_ref[...]), g_ref[...], b_ref[...])


def _nsa_out(x2, o2, sz2, w_out, ln_g, ln_b, *, row_tile):
    n, d = x2.shape
    row = lambda i: (i, 0)
    const = lambda i: (0, 0)
    return pl.pallas_call(
        _nsa_out_kernel,
        grid=(n // row_tile,),
        in_specs=[pl.BlockSpec((row_tile, d), row), pl.BlockSpec((row_tile, d), row),
                  pl.BlockSpec((row_tile, d), row), pl.BlockSpec((d, d), const),
                  pl.BlockSpec((1, d), const), pl.BlockSpec((1, d), const)],
        out_specs=pl.BlockSpec((row_tile, d), row),
        out_shape=jax.ShapeDtypeStruct((n, d), F32),
        compiler_params=pltpu.CompilerParams(
            dimension_semantics=("arbitrary",), vmem_limit_bytes=VMEM_LIMIT),
        name="nsa_out",
    )(x2, o2, sz2, w_out, ln_g, ln_b)


def _dup_half(slab, g, lo):
    rolled = pltpu.roll(slab, HEAD_DIM, 1)
    return jnp.where(lo, slab, rolled) if g == 0 else jnp.where(lo, rolled, slab)


def _select_blocks(score, nl, real, n_slc, n_sel, axis):
    rank = jnp.zeros(score.shape, jnp.int32)
    for i in range(n_slc):
        cand = score[:, 2 * i:2 * i + 1] if axis == 1 else score[2 * i:2 * i + 1, :]
        beats = (cand > score) | ((cand == score) & (nl > 2 * i))
        rank = rank + beats.astype(jnp.int32)
    return real & (rank < n_sel)


def _block_scores(imp2, nl, tq, n_slc):
    blk = nl >> 1
    real = ((nl & 1) == 0) & (blk < n_slc)
    cur = tq >> 6
    forced = (blk == 0) | ((blk <= cur) & (blk > cur - N_LOCAL))
    score = jnp.where(forced, FORCE, jnp.where(blk <= cur, imp2, NEG_INF))
    return jnp.where(real, score, -jnp.inf), real


def _alibi_query_cols(head, n_rows):
    lane = lax.broadcasted_iota(jnp.int32, (1, LANES), 1)
    rem = jnp.full((1, LANES), _slope(head) * LOG2E, F32)
    row = jnp.zeros((1, LANES), F32)
    for i in range(3):
        piece = rem.astype(BF16).astype(F32)
        rem = rem - piece
        row = jnp.where(lane == i, 16.0 * piece, jnp.where(lane == 3 + i, piece, row))
    return jnp.broadcast_to(row, (n_rows, LANES)).astype(BF16)


def _alibi_key_cols(n_rows, period):
    r = lax.broadcasted_iota(jnp.int32, (n_rows, LANES), 0) & (period - 1)
    lane = lax.broadcasted_iota(jnp.int32, (n_rows, LANES), 1)
    return jnp.where(lane < 3, (r >> 4).astype(F32), jnp.where(lane < 6, (r & 15).astype(F32), 0.0))


def _nsa_prompt_kernel(q_ref, rows_ref, win_ref, gates_ref, wc_ref, o_ref,
                       ka, vta, kwa, vwta, kcd, vct_lo, vct_hi, qa_ref, selb_ref, acc_ref):
    blk = q_ref.shape[0]
    t_len = rows_ref.shape[0]
    n_cmp = t_len // CMP_BLOCK
    n_slc = -(-t_len // SLC_BLOCK)
    n_sel = min(TOP_N, n_slc)
    n_blk = t_len // blk
    slc_per_blk = blk // SLC_BLOCK
    n_wblk = WINDOW // blk + 1
    sel_rows = -(-2 * n_slc // SUBLANES) * SUBLANES
    v_rows = vta.shape[2]
    qi = pl.program_id(1)
    t0 = qi * blk
    lo = lax.broadcasted_iota(jnp.int32, (1, LANES), 1) < HEAD_DIM

    @pl.when(qi == 0)
    def _build():
        lo_t = lax.broadcasted_iota(jnp.int32, (t_len, LANES), 1) < HEAD_DIM
        lo_c = lax.broadcasted_iota(jnp.int32, (LANES, LANES), 1) < HEAD_DIM
        lo_r = lax.broadcasted_iota(jnp.int32, (LANES, 1), 0) < HEAD_DIM
        vrow = lax.broadcasted_iota(jnp.int32, (v_rows, 1), 0)
        cm = rows_ref[:, 0:2 * LANES]
        kvc = jnp.sum(cm.reshape(n_cmp, CMP_BLOCK, 2 * LANES) * wc_ref[...][None], axis=1)
        kvc = jnp.concatenate([kvc, jnp.zeros((LANES - n_cmp, 2 * LANES), F32)], axis=0)
        key_cols = _alibi_key_cols(t_len, blk).astype(BF16)

        def values_t(v):
            vt = v.T[0:v_rows, :]
            return jnp.where(vrow < HEAD_DIM, vt, jnp.where(vrow == HEAD_DIM, 1.0, 0.0)).astype(BF16)

        for g in range(N_KV):
            kd = _dup_half(rows_ref[:, 2 * LANES:3 * LANES], g, lo_t).astype(BF16)
            ka[g] = jnp.concatenate([kd, key_cols], axis=1)
            vt = values_t(_dup_half(rows_ref[:, 3 * LANES:4 * LANES], g, lo_t))
            kd = _dup_half(win_ref[:, 0:LANES], g, lo_t).astype(BF16)
            kwa[g] = jnp.concatenate([kd, key_cols], axis=1)
            vwt = values_t(_dup_half(win_ref[:, LANES:2 * LANES], g, lo_t))
            for c in range(n_blk):
                vta[g, c] = vt[:, c * blk:(c + 1) * blk]
                vwta[g, c] = vwt[:, c * blk:(c + 1) * blk]
            kcd[g] = _dup_half(kvc[:, 0:LANES], g, lo_c).astype(BF16)
            vct = _dup_half(kvc[:, LANES:2 * LANES], g, lo_c).T
            vct_lo[g] = jnp.where(lo_r, vct, 0.0).astype(BF16)
            vct_hi[g] = jnp.where(lo_r, 0.0, vct).astype(BF16)

    tq = t0 + lax.broadcasted_iota(jnp.int32, (1, blk), 1)
    blk_r = lax.broadcasted_iota(jnp.int32, (LANES, 1), 0)
    cmp_end = blk_r * CMP_BLOCK + (CMP_BLOCK - 1)
    valid_c = (cmp_end <= tq) & (blk_r < n_cmp)
    dist_c = (tq - cmp_end).astype(F32)

    lr = lax.broadcasted_iota(jnp.int32, (1, blk), 1) - lax.broadcasted_iota(jnp.int32, (blk, 1), 0)
    causal_bias = jnp.where(lr >= 0, 0.0, NEG_INF)

    w_start = pl.multiple_of(jnp.maximum(t0 - WINDOW, 0), blk)
    w_blk = w_start // blk
    w_shift, w_bias = [], []
    for i in range(n_wblk):
        shift = t0 - w_start - i * blk
        dist = lr + shift
        w_shift.append(shift.astype(F32))
        w_bias.append(jnp.where((dist >= 0) & (dist < WINDOW), 0.0, NEG_INF))

    gates_t = gates_ref[...].T

    for hh in range(N_HEADS):
        q2 = q_ref[:, (hh // 2) * LANES:(hh // 2 + 1) * LANES]
        zero = jnp.zeros_like(q2)
        qm = jnp.where(lo, q2, zero) if hh % 2 == 0 else jnp.where(lo, zero, q2)
        qa_ref[hh] = jnp.concatenate([qm, _alibi_query_cols(hh, blk)], axis=1)

    o_cmp = []
    for g in range(N_KV):
        imp = jnp.zeros((LANES, blk), F32)
        for j in range(HPG // 2):
            oc = jnp.zeros((LANES, blk), F32)
            for hh, v_ref in ((g * HPG + 2 * j, vct_lo), (g * HPG + 2 * j + 1, vct_hi)):
                s = _dot_nt(kcd[g], qa_ref[hh, :, 0:LANES]) - (_slope(hh) * LOG2E) * dist_c
                s = jnp.where(valid_c, s, NEG_INF)
                e = jnp.exp2(s - jnp.max(s, axis=0, keepdims=True))
                p = jnp.where(valid_c, e, 0.0) / jnp.sum(e, axis=0, keepdims=True)
                imp = imp + p
                oc = oc + _dot(v_ref[g], p.astype(BF16))
            o_cmp.append(oc)

        imp2 = (imp + pltpu.roll(imp, LANES - 1, 0))[0:sel_rows, :]
        score, real = _block_scores(imp2, blk_r[0:sel_rows, :], tq, n_slc)
        sel = _select_blocks(score, blk_r[0:sel_rows, :], real, n_slc, n_sel, axis=0)
        selb_ref[g] = jnp.where(sel, 0.0, NEG_INF)

    acc_ref[...] = jnp.zeros(acc_ref.shape, F32)

    def key_block(c, ms, extra_bias):
        off = pl.multiple_of(c * blk, blk)
        shift = (t0 - off).astype(F32)
        scores, new_ms = [], []
        for g in range(N_KV):
            k = ka[g, pl.ds(off, blk), :]
            rows2 = selb_ref[g, pl.ds(pl.multiple_of(c * (2 * slc_per_blk), 2 * slc_per_blk), 2 * slc_per_blk), :]
            bias = jnp.concatenate([jnp.broadcast_to(rows2[2 * m:2 * m + 1, :], (SLC_BLOCK, blk))
                                    for m in range(slc_per_blk)], axis=0)
            if extra_bias is not None:
                bias = bias + extra_bias
            for hh in range(g * HPG, (g + 1) * HPG):
                s = _dot_nt(k, qa_ref[hh]) + bias
                c_h = (_slope(hh) * LOG2E) * shift
                new_ms.append(jnp.maximum(ms[hh], jnp.max(s, axis=0, keepdims=True) - c_h))
                scores.append(s)
        for hh in range(N_HEADS):
            c_h = (_slope(hh) * LOG2E) * shift
            p = jnp.exp2(scores[hh] - (new_ms[hh] + c_h))
            acc_ref[hh] = (acc_ref[hh] * jnp.exp2(ms[hh] - new_ms[hh])
                           + _dot(vta[hh // HPG, c], p.astype(BF16)))
        return tuple(new_ms)

    ms = lax.fori_loop(0, qi, lambda c, ms: key_block(c, ms, None),
                       (jnp.full((1, blk), NEG_INF, F32),) * N_HEADS)
    key_block(qi, ms, causal_bias)

    for g in range(N_KV):
        k_win = [kwa[g, pl.ds(pl.multiple_of(w_start + i * blk, blk), blk), :] for i in range(n_wblk)]
        w_scores, w_max = [], []
        for hh in range(g * HPG, (g + 1) * HPG):
            slope = _slope(hh) * LOG2E
            tiles, m_w = [], None
            for i in range(n_wblk):
                s = _dot_nt(k_win[i], qa_ref[hh]) + w_bias[i]
                m_i = jnp.max(s, axis=0, keepdims=True) - slope * w_shift[i]
                m_w = m_i if m_w is None else jnp.maximum(m_w, m_i)
                tiles.append(s)
            w_scores.append(tiles)
            w_max.append(m_w)
        o_wins = []
        for h in range(HPG):
            slope = _slope(g * HPG + h) * LOG2E
            acc_w = jnp.zeros((v_rows, blk), F32)
            for i in range(n_wblk):
                p = jnp.exp2(w_scores[h][i] - (w_max[h] + slope * w_shift[i]))
                acc_w = acc_w + _dot(vwta[g, w_blk + i], p.astype(BF16))
            o_wins.append(acc_w[0:HEAD_DIM, :] / acc_w[HEAD_DIM:HEAD_DIM + 1, :])

        for j in range(HPG // 2):
            outs = []
            for hh in (g * HPG + 2 * j, g * HPG + 2 * j + 1):
                o_win = o_wins[hh - g * HPG]
                acc = acc_ref[hh]
                o_sel = acc[0:HEAD_DIM, :] / acc[HEAD_DIM:HEAD_DIM + 1, :]
                half = (hh % 2) * HEAD_DIM
                gate = lambda c: gates_t[c * N_HEADS + hh:c * N_HEADS + hh + 1, :]
                o_c = o_cmp[g * (HPG // 2) + j][half:half + HEAD_DIM, :]
                outs.append(gate(0) * o_c + gate(1) * o_sel + gate(2) * o_win)
            c0 = g * HPG * HEAD_DIM + j * LANES
            o_ref[:, c0:c0 + LANES] = jnp.concatenate(outs, axis=0).T.astype(o_ref.dtype)


def _block_expand_matrix(n_rows, n_keys):
    n = jnp.arange(n_rows, dtype=jnp.int32)[:, None]
    p = jnp.arange(n_keys, dtype=jnp.int32)[None, :]
    return (n == 2 * (p // SLC_BLOCK)).astype(BF16)


def _nsa_prompt_attention(q, rows, win, gates, wc, *, q_tile=256):
    bsz, t_len, _ = q.shape
    assert WINDOW % q_tile == 0 and q_tile % (2 * LANES) == 0 and t_len >= WINDOW + q_tile
    assert t_len // CMP_BLOCK <= LANES
    v_rows = HEAD_DIM + 16
    k_scratch = pltpu.VMEM((N_KV, t_len, 2 * LANES), BF16)
    v_scratch = pltpu.VMEM((N_KV, t_len // q_tile, v_rows, q_tile), BF16)
    c_scratch = pltpu.VMEM((N_KV, LANES, LANES), BF16)
    return pl.pallas_call(
        _nsa_prompt_kernel,
        grid=(bsz, t_len // q_tile),
        in_specs=[
            pl.BlockSpec((None, q_tile, D_ATT), lambda b, i: (b, i, 0)),
            pl.BlockSpec((None, t_len, N_ROWCOLS), lambda b, i: (b, 0, 0)),
            pl.BlockSpec((None, t_len, N_WINCOLS), lambda b, i: (b, 0, 0)),
            pl.BlockSpec((None, q_tile, LANES), lambda b, i: (b, i, 0)),
            pl.BlockSpec((CMP_BLOCK, 2 * LANES), lambda b, i: (0, 0)),
        ],
        out_specs=pl.BlockSpec((None, q_tile, D_ATT), lambda b, i: (b, i, 0)),
        out_shape=jax.ShapeDtypeStruct((bsz, t_len, D_ATT), BF16),
        scratch_shapes=[k_scratch, v_scratch, k_scratch, v_scratch,
                        c_scratch, c_scratch, c_scratch,
                        pltpu.VMEM((N_HEADS, q_tile, 2 * LANES), BF16),
                        pltpu.VMEM((N_KV, -(-2 * (-(-t_len // SLC_BLOCK)) // SUBLANES) * SUBLANES, q_tile), F32),
                        pltpu.VMEM((N_HEADS, v_rows, q_tile), F32)],
        compiler_params=pltpu.CompilerParams(
            dimension_semantics=("arbitrary", "arbitrary"), vmem_limit_bytes=VMEM_LIMIT),
        name="nsa_prompt_attention",
    )(q, rows, win, gates, wc)


def _nsa_sample_kernel(pt_ref, *refs, pages_per_step, n_steps, n_new):
    page_refs = refs[:pages_per_step]
    (qbd_ref, new_ref, winbuf_ref, gates_ref, wc_ref, e_ref, o_ref, s_ref, vst_ref, kcs, vcs) = refs[pages_per_step:]
    del pt_ref
    step = pl.program_id(1)
    keys_per_step = pages_per_step * PAGE_SIZE
    p_len = n_steps * keys_per_step
    n_keys = p_len + LANES
    n_cmp = p_len // CMP_BLOCK
    n_slc = -(-(p_len + n_new) // SLC_BLOCK)
    n_sel = min(TOP_N, n_slc)
    cmp_per_step = keys_per_step // CMP_BLOCK
    n_rows = N_KV * n_new * HPG
    gd = N_KV * HEAD_DIM

    def slab(r, kind):
        return r[kind].reshape(gd, r.shape[-1])

    cm = jnp.concatenate([jnp.concatenate([slab(r, 0).T, slab(r, 1).T], axis=1) for r in page_refs], axis=0)
    kvc = jnp.sum(cm.reshape(cmp_per_step, CMP_BLOCK, 2 * LANES) * wc_ref[...][None], axis=1)
    c_off = pl.multiple_of(step * cmp_per_step, cmp_per_step)
    kcs[pl.ds(c_off, cmp_per_step), :] = kvc[:, 0:LANES].astype(BF16)
    vcs[pl.ds(c_off, cmp_per_step), :] = kvc[:, LANES:2 * LANES].astype(BF16)
    s_ref[step] = _dot(qbd_ref[...], jnp.concatenate([slab(r, 2).astype(BF16) for r in page_refs], axis=1))
    vst_ref[step] = jnp.concatenate([slab(r, 3).astype(BF16) for r in page_refs], axis=1)

    @pl.when(step == n_steps - 1)
    def _attend():
        n_wb = winbuf_ref.shape[-1]
        w_keys = n_wb + LANES
        qbd = qbd_ref[...]
        r = lax.broadcasted_iota(jnp.int32, (n_rows, 1), 0)
        head = (r // (n_new * HPG)) * HPG + (r % HPG)
        slope = jnp.exp2((head + 1).astype(F32) * (-8.0 / N_HEADS)) * LOG2E
        tq = p_len + (r // HPG) % n_new

        def attend(s, values_t, dist, bias):
            s = s - slope * dist.astype(F32) + bias
            e = jnp.exp2(s - jnp.max(s, axis=-1, keepdims=True))
            eb = e.astype(BF16)
            o = sum(_dot_nt(eb[:, ks], vt) for ks, vt in values_t)
            return o / jnp.sum(e, axis=-1, keepdims=True)

        nl = lax.broadcasted_iota(jnp.int32, (n_rows, n_cmp), 1)
        cmp_end = nl * CMP_BLOCK + (CMP_BLOCK - 1)
        valid_c = cmp_end <= tq
        s = _dot_nt(qbd, kcs[...]) - slope * (tq - cmp_end).astype(F32)
        s = jnp.where(valid_c, s, NEG_INF)
        e = jnp.exp2(s - jnp.max(s, axis=-1, keepdims=True))
        p = jnp.where(valid_c, e, 0.0) / jnp.sum(e, axis=-1, keepdims=True)
        o_cmp = _dot(p.astype(BF16), vcs[...])

        n_gt = N_KV * n_new
        imp = jnp.sum(p.reshape(n_gt, HPG, n_cmp), axis=1)
        imp2 = imp + pltpu.roll(imp, n_cmp - 1, 1)
        w_sel = -(-2 * n_slc // LANES) * LANES
        imp2 = jnp.concatenate([imp2, jnp.zeros((n_gt, w_sel - n_cmp), F32)], axis=1)
        nl8 = lax.broadcasted_iota(jnp.int32, (n_gt, w_sel), 1)
        r8 = lax.broadcasted_iota(jnp.int32, (n_gt, 1), 0)
        score, real = _block_scores(imp2, nl8, p_len + r8 % n_new, n_slc)
        sel = _select_blocks(score, nl8, real, n_slc, n_sel, axis=1)
        selb = jnp.where(sel, 1.0, 0.0)
        selb = jnp.broadcast_to(selb[:, None, :], (n_gt, HPG, w_sel)).reshape(n_rows, w_sel).astype(BF16)
        keys_per_tile = e_ref.shape[1]
        hit = [_dot(selb[:, j * LANES:(j + 1) * LANES], e_ref[...]) for j in range(p_len // keys_per_tile)]
        hit.append(jnp.broadcast_to(selb[:, 2 * (p_len // SLC_BLOCK):2 * (p_len // SLC_BLOCK) + 1].astype(F32),
                                    (n_rows, LANES)))
        hit = jnp.concatenate(hit, axis=1)
        dist = tq - lax.broadcasted_iota(jnp.int32, (n_rows, n_keys), 1)
        new_k, new_v, new_kw, new_vw = (new_ref[i] for i in range(4))
        s_sel = jnp.concatenate([s_ref[i] for i in range(n_steps)] + [_dot(qbd, new_k)], axis=1)
        values_t = [(slice(i * keys_per_step, (i + 1) * keys_per_step), vst_ref[i]) for i in range(n_steps)]
        o_sel = attend(s_sel, values_t + [(slice(p_len, n_keys), new_v)], dist,
                       jnp.where((hit > 0.5) & (dist >= 0), 0.0, NEG_INF))

        dist_w = tq - (p_len - n_wb) - lax.broadcasted_iota(jnp.int32, (n_rows, w_keys), 1)
        s_win = jnp.concatenate([_dot(qbd, slab(winbuf_ref, 0).astype(BF16)), _dot(qbd, new_kw)], axis=1)
        o_win = attend(s_win, [(slice(0, n_wb), slab(winbuf_ref, 1).astype(BF16)), (slice(n_wb, w_keys), new_vw)],
                       dist_w, jnp.where((dist_w >= 0) & (dist_w < WINDOW), 0.0, NEG_INF))

        o_ref[...] = (gates_ref[:, 0:1] * o_cmp + gates_ref[:, 1:2] * o_sel + gates_ref[:, 2:3] * o_win)


def _nsa_sample_attention(page_table, cache_t, qbd, new_t, winbuf_t, gates, wc, *, n_new, pages_per_step=64):
    bsz, n_pages = page_table.shape
    n_steps = n_pages // pages_per_step
    keys_per_step = pages_per_step * PAGE_SIZE
    p_len = n_pages * PAGE_SIZE
    n_rows = qbd.shape[1]
    n_wb = winbuf_t.shape[-1]
    n_cmp = p_len // CMP_BLOCK
    gd = N_KV * HEAD_DIM
    keys_per_tile = (LANES // 2) * SLC_BLOCK
    assert p_len % keys_per_tile == 0 and n_new <= SLC_BLOCK
    e_mat = _block_expand_matrix(LANES, keys_per_tile)
    kern = functools.partial(_nsa_sample_kernel, pages_per_step=pages_per_step, n_steps=n_steps, n_new=n_new)

    def page_spec(k):
        return pl.BlockSpec((None, 4, N_KV, HEAD_DIM, PAGE_SIZE),
                            lambda b, s, pt: (pt[b, s * pages_per_step + k], 0, 0, 0, 0))

    per_b = lambda b, s, pt: (b, 0, 0)
    const = lambda b, s, pt: (0, 0)
    grid_spec = pltpu.PrefetchScalarGridSpec(
        num_scalar_prefetch=1,
        grid=(bsz, n_steps),
        in_specs=[page_spec(k) for k in range(pages_per_step)] + [
            pl.BlockSpec((None, n_rows, LANES), per_b),
            pl.BlockSpec((None, 4, gd, LANES), lambda b, s, pt: (b, 0, 0, 0)),
            pl.BlockSpec((None, 2, N_KV, HEAD_DIM, n_wb), lambda b, s, pt: (b, 0, 0, 0, 0)),
            pl.BlockSpec((None, n_rows, LANES), per_b),
            pl.BlockSpec((CMP_BLOCK, 2 * LANES), const),
            pl.BlockSpec((LANES, keys_per_tile), const),
        ],
        out_specs=pl.BlockSpec((None, n_rows, LANES), per_b),
        scratch_shapes=[
            pltpu.VMEM((n_steps, n_rows, keys_per_step), F32),
            pltpu.VMEM((n_steps, gd, keys_per_step), BF16),
            pltpu.VMEM((n_cmp, LANES), BF16), pltpu.VMEM((n_cmp, LANES), BF16),
        ],
    )
    return pl.pallas_call(
        kern,
        grid_spec=grid_spec,
        out_shape=jax.ShapeDtypeStruct((bsz, n_rows, LANES), F32),
        compiler_params=pltpu.CompilerParams(
            dimension_semantics=("arbitrary", "arbitrary"), vmem_limit_bytes=VMEM_LIMIT),
        name="nsa_sample_attention",
    )(page_table, *([cache_t] * pages_per_step), qbd, new_t, winbuf_t, gates, wc, e_mat)


def _nsa_weights(w_in):
    d = w_in.shape[0]
    c_kv = D_ATT
    c_g = c_kv + N_ROWCOLS + N_WINCOLS
    c_z = c_g + N_GATES
    pad = jnp.zeros((d, LANES - N_GATES), w_in.dtype)
    return jnp.concatenate([w_in[:, :c_g], w_in[:, c_z:], w_in[:, c_g:c_z], pad], axis=1).astype(BF16)


def _cmp_weight_tile(w_cmp):
    return w_cmp.transpose(1, 0, 2, 3).reshape(CMP_BLOCK, 2 * N_KV * HEAD_DIM)


def _nsa_layer_prompt(x, w_all, wc, w_out, ln_g, ln_b):
    bsz, t_len, d = x.shape
    x2 = x.reshape(bsz * t_len, d)
    q, rows, win, sz, gates, rows_t, win_t = _nsa_proj(x2, w_all, row_tile=512, seq_len=t_len)
    o = _nsa_prompt_attention(q.reshape(bsz, t_len, D_ATT), rows.reshape(bsz, t_len, N_ROWCOLS),
                              win.reshape(bsz, t_len, N_WINCOLS), gates.reshape(bsz, t_len, LANES), wc)
    y = _nsa_out(x2, o.reshape(bsz * t_len, D_ATT), sz, w_out, ln_g, ln_b, row_tile=512)
    keep = min(WINDOW, t_len)
    rows_out = rows_t.reshape(bsz, 4, N_KV, HEAD_DIM, t_len).transpose(0, 4, 1, 2, 3)
    win_out = win_t[:, :, t_len - keep:].reshape(bsz, 2, N_KV, HEAD_DIM, keep).transpose(0, 4, 1, 2, 3)
    return y.reshape(bsz, t_len, d), rows_out, win_out


def _nsa_layer_sample(x, cache, win_buf, page_table, w_all, wc, w_out, ln_g, ln_b):
    bsz, n_new, d = x.shape
    x2 = x.reshape(bsz * n_new, d)
    q, rows, win, sz, gates = _nsa_proj(x2, w_all, row_tile=bsz * n_new)
    q5 = q.reshape(bsz, n_new, N_KV, HPG, HEAD_DIM).transpose(0, 2, 1, 3, 4)
    eye = jnp.eye(N_KV, dtype=q.dtype)
    qbd = (q5[:, :, :, :, None, :] * eye[None, :, None, None, :, None]).reshape(
        bsz, N_KV * n_new * HPG, N_KV * HEAD_DIM)
    g5 = gates[:, :N_GATES].reshape(bsz, n_new, 3, N_KV, HPG).transpose(0, 3, 1, 4, 2)
    g_rows = jnp.pad(g5.reshape(bsz, N_KV * n_new * HPG, 3), ((0, 0), (0, 0), (0, LANES - 3)))
    rows3 = rows.reshape(bsz, n_new, N_ROWCOLS)
    win3 = win.reshape(bsz, n_new, N_WINCOLS)
    n_wb = win_buf.shape[1]
    gd = N_KV * HEAD_DIM
    new_t = jnp.stack([rows3[:, :, 2 * gd:3 * gd], rows3[:, :, 3 * gd:4 * gd],
                       win3[:, :, 0:gd], win3[:, :, gd:2 * gd]], axis=1).transpose(0, 1, 3, 2)
    new_t = jnp.pad(new_t, ((0, 0), (0, 0), (0, 0), (0, LANES - n_new))).astype(BF16)
    o_rows = _nsa_sample_attention(
        page_table, cache.transpose(0, 2, 3, 4, 1), qbd, new_t, win_buf.transpose(0, 2, 3, 4, 1),
        g_rows, wc, n_new=n_new)
    o6 = o_rows.reshape(bsz, N_KV, n_new, HPG, N_KV, HEAD_DIM)
    o = jnp.stack([o6[:, g, :, :, g, :] for g in range(N_KV)], axis=2).reshape(bsz * n_new, D_ATT)
    y = _nsa_out(x2, o, sz, w_out, ln_g, ln_b, row_tile=bsz * n_new)
    all_win = jnp.concatenate([win_buf, win3.reshape(bsz, n_new, 2, N_KV, HEAD_DIM)], axis=1)
    return (y.reshape(bsz, n_new, d), rows3.reshape(bsz, n_new, 4, N_KV, HEAD_DIM),
            all_win[:, all_win.shape[1] - n_wb:])


def _time_major(a):
    bsz, k, d = a.shape
    return a.transpose(1, 0, 2).reshape(1, k * bsz, d)


def _batch_major(a, bsz):
    _, kb, d = a.shape
    return a.reshape(kb // bsz, bsz, d).transpose(1, 0, 2)


def kernel(x_prompt, x_sample, cache_nsa_kv, cache_nsa_win, state_conv_a, state_lru_h, state_lru_conv, page_table,
           ln_g, ln_b, a_w_in, a_conv_w, a_w_out, b_w_in, b_w_cmp, b_w_out,
           c_w_in, c_conv_w, c_conv_b, c_w_a, c_b_a, c_w_x, c_b_x, c_lam, c_w_out):
    bp, t_len, d = x_prompt.shape
    bs, n_new, _ = x_sample.shape
    yp = x_prompt
    ys = _time_major(x_sample)
    conv_a_p, conv_a_s = [], []
    rows_p, rows_s, win_p, win_s = [], [], [], []
    h_p, h_s, cc_p, cc_s = [], [], [], []
    row2 = lambda v: v.reshape(1, -1)
    for i in range(DEPTH):
        kind, j = i % 3, i // 3
        g, b = row2(ln_g[i]), row2(ln_b[i])
        if kind == 0:
            w_in, w_out = a_w_in[j].astype(BF16), a_w_out[j].astype(BF16)
            taps = a_conv_w.shape[1]
            yp, sp = _conv_layer(yp, jnp.zeros((bp, taps - 1, d), F32), w_in, a_conv_w[j], w_out, g, b,
                                 stride=1, row_tile=512)
            ys, ss = _conv_layer(ys, _time_major(state_conv_a[j]), w_in, a_conv_w[j], w_out, g, b,
                                 stride=bs, row_tile=n_new * bs)
            conv_a_p.append(sp)
            conv_a_s.append(_batch_major(ss, bs))
        elif kind == 1:
            w_all = _nsa_weights(b_w_in[j])
            wc = _cmp_weight_tile(b_w_cmp[j])
            w_out = b_w_out[j].astype(BF16)
            yp, rp, wp = _nsa_layer_prompt(yp, w_all, wc, w_out, g, b)
            ys_b, rs, ws = _nsa_layer_sample(_batch_major(ys, bs), cache_nsa_kv[j], cache_nsa_win[j], page_table,
                                             w_all, wc, w_out, g, b)
            ys = _time_major(ys_b)
            rows_p.append(rp)
            rows_s.append(rs)
            win_p.append(wp)
            win_s.append(ws)
        else:
            w_in, w_out = c_w_in[j].astype(BF16), c_w_out[j].astype(BF16)
            w_a, w_x = c_w_a[j].astype(BF16), c_w_x[j].astype(BF16)
            taps = c_conv_w.shape[1]
            args = (w_in, c_conv_w[j], row2(c_conv_b[j]), w_a, row2(c_b_a[j]), w_x, row2(c_b_x[j]),
                    row2(c_lam[j]), w_out, g, b)
            yp, hp, cp = _lru_layer(yp, jnp.zeros((bp, 1, d), F32), jnp.zeros((bp, taps - 1, d), F32), *args,
                                    stride=1, row_tile=256)
            ys, hs, cs = _lru_layer(ys, state_lru_h[j].reshape(1, bs, d), _time_major(state_lru_conv[j]), *args,
                                    stride=bs, row_tile=n_new * bs)
            h_p.append(hp.reshape(bp, d))
            h_s.append(hs.reshape(bs, d))
            cc_p.append(cp)
            cc_s.append(_batch_major(cs, bs))
    return (yp, _batch_major(ys, bs), jnp.stack(conv_a_p), jnp.stack(conv_a_s), jnp.stack(rows_p), jnp.stack(rows_s),
            jnp.stack(win_p), jnp.stack(win_s), jnp.stack(h_p), jnp.stack(h_s), jnp.stack(cc_p), jnp.stack(cc_s))
```

```python
import functools

import jax
import jax.numpy as jnp
from jax import lax
from jax.experimental import pallas as pl
from jax.experimental.pallas import tpu as pltpu

F32 = jnp.float32
BF16 = jnp.bfloat16

DEPTH = 4
N_HEADS = 16
N_KV = 2
HPG = N_HEADS // N_KV
HEAD_DIM = 64
CMP_BLOCK = 32
SLC_BLOCK = 64
TOP_N = 16
N_LOCAL = 2
WINDOW = 512
PAGE_SIZE = 128
NEG_INF = -1e30
FORCE = 1e9
RG_C = 8.0
N_RG_BLOCKS = 4
ALPHA = (2 * DEPTH) ** 0.25
LN_EPS = 1e-5
LOG2E = 1.4426950408889634

LANES = 128
SUBLANES = 8
VMEM_LIMIT = 56 * 1024 * 1024


def _dot(a, b):
    return jnp.dot(a, b, preferred_element_type=F32)


def _dot_nt(a, b):
    return lax.dot_general(a, b, (((1,), (1,)), ((), ())), preferred_element_type=F32)


def _silu(z):
    return z * jax.nn.sigmoid(z)


def _ln_residual(x, y, g, b):
    r = ALPHA * x + y
    mu = jnp.mean(r, axis=-1, keepdims=True)
    c = r - mu
    var = jnp.mean(c * c, axis=-1, keepdims=True)
    return c * lax.rsqrt(var + LN_EPS) * g + b


def _slope(head):
    return 2.0 ** (-8.0 * (head + 1) / N_HEADS)


def _prev_rows(u, carry, k):
    n_carry = carry.shape[0]
    rolled = pltpu.roll(u, k, 0)
    row = lax.broadcasted_iota(jnp.int32, (SUBLANES, 1), 0)
    head = rolled[0:SUBLANES, :]
    for i in range(k):
        head = jnp.where(row == i, carry[n_carry - k + i:n_carry - k + i + 1, :], head)
    return jnp.concatenate([head, rolled[SUBLANES:, :]], axis=0)


def _conv_layer_kernel(x_ref, buf_ref, win_ref, cw_ref, wout_ref, g_ref, b_ref,
                       y_ref, nbuf_ref, carry_ref, *, stride, col_chunk):
    rows, d = x_ref.shape
    taps = cw_ref.shape[0]
    x = x_ref[...]
    xb = x.astype(BF16)
    if stride == 1:
        @pl.when(pl.program_id(1) == 0)
        def _():
            carry_ref[...] = buf_ref[...]
    acc = jnp.zeros((rows, d), F32)
    for j in range(d // col_chunk):
        lo = j * col_chunk
        sl = slice(lo, lo + col_chunk)
        h, bg, cg, z = [_dot(xb, win_ref[:, part * d + lo:part * d + lo + col_chunk]) for part in range(4)]
        u = cg * h
        conv = u * cw_ref[taps - 1:taps, sl]
        if stride == 1:
            carry = carry_ref[:, sl]
            for k in range(1, taps):
                conv = conv + _prev_rows(u, carry, k) * cw_ref[taps - 1 - k:taps - k, sl]
            new_buf = u[rows - (taps - 1):rows, :]
            carry_ref[:, sl] = new_buf
        else:
            ext = jnp.concatenate([buf_ref[:, sl], u], axis=0)
            for k in range(1, taps):
                s0 = (taps - 1 - k) * stride
                conv = conv + ext[s0:s0 + rows, :] * cw_ref[taps - 1 - k:taps - k, sl]
            new_buf = ext[rows:rows + (taps - 1) * stride, :]
        nbuf_ref[:, sl] = new_buf
        m = _silu(z) * bg * conv
        acc = acc + _dot(m.astype(BF16), wout_ref[sl, :])
    y_ref[...] = _ln_residual(x, acc, g_ref[...], b_ref[...])


def _conv_layer(x, buf, w_in, conv_w, w_out, ln_g, ln_b, *, layer, stride, row_tile):
    bsz, t_len, d = x.shape
    n_buf = buf.shape[1]
    taps = conv_w.shape[1]
    grid = (bsz, t_len // row_tile)
    kern = functools.partial(_conv_layer_kernel, stride=stride, col_chunk=512)
    return pl.pallas_call(
        kern,
        grid=grid,
        in_specs=[
            pl.BlockSpec((None, row_tile, d), lambda b, t: (b, t, 0)),
            pl.BlockSpec((None, n_buf, d), lambda b, t: (b, 0, 0)),
            pl.BlockSpec((None, d, 4 * d), lambda b, t: (layer, 0, 0)),
            pl.BlockSpec((None, taps, d), lambda b, t: (layer, 0, 0)),
            pl.BlockSpec((None, d, d), lambda b, t: (layer, 0, 0)),
            pl.BlockSpec((1, d), lambda b, t: (0, 0)),
            pl.BlockSpec((1, d), lambda b, t: (0, 0)),
        ],
        out_specs=[
            pl.BlockSpec((None, row_tile, d), lambda b, t: (b, t, 0)),
            pl.BlockSpec((None, n_buf, d), lambda b, t: (b, 0, 0)),
        ],
        out_shape=[
            jax.ShapeDtypeStruct((bsz, t_len, d), F32),
            jax.ShapeDtypeStruct((bsz, n_buf, d), F32),
        ],
        scratch_shapes=[pltpu.VMEM((taps - 1, d), F32)],
        compiler_params=pltpu.CompilerParams(
            dimension_semantics=("arbitrary", "arbitrary"), vmem_limit_bytes=VMEM_LIMIT),
        name="conv_layer",
    )(x, buf, w_in, conv_w, w_out, ln_g, ln_b)


def _scan_rows(a_list, b_list, h_prev_list):
    rows, cols = a_list[0].shape
    groups = rows // SUBLANES
    a3 = [a.reshape(groups, SUBLANES, cols) for a in a_list]
    b3 = [b.reshape(groups, SUBLANES, cols) for b in b_list]
    sub = lax.broadcasted_iota(jnp.int32, (1, SUBLANES, 1), 1)
    s = 1
    while s < SUBLANES:
        live = sub >= s
        for n, (a, b) in enumerate(zip(a3, b3)):
            a_sh = jnp.where(live, pltpu.roll(a, s, 1), 1.0)
            b_sh = jnp.where(live, pltpu.roll(b, s, 1), 0.0)
            b3[n] = a * b_sh + b
            a3[n] = a * a_sh
        s *= 2
    carries = list(h_prev_list)
    outs = [[] for _ in a3]
    for j in range(groups):
        for n in range(len(a3)):
            h = a3[n][j] * carries[n] + b3[n][j]
            carries[n] = h[SUBLANES - 1:SUBLANES, :]
            outs[n].append(h)
    return [jnp.concatenate(o, axis=0) for o in outs]


def _lru_layer_kernel(x_ref, h0_ref, buf_ref, win_ref, cw_ref, cb_ref, wa_ref, ba_ref, wx_ref, bx_ref,
                      lam_ref, wout_ref, g_ref, b_ref,
                      y_ref, hlast_ref, nbuf_ref, carry_ref, hcarry_ref, *, stride):
    rows, d = x_ref.shape
    taps = cw_ref.shape[0]
    n_blocks, blk, _ = wa_ref.shape
    x = x_ref[...]
    xb = x.astype(BF16)
    if stride == 1:
        @pl.when(pl.program_id(1) == 0)
        def _():
            carry_ref[...] = buf_ref[...]
            hcarry_ref[...] = h0_ref[...]
    slices = [slice(n * blk, (n + 1) * blk) for n in range(n_blocks)]
    ucs = []
    for sl in slices:
        u = _dot(xb, win_ref[:, sl])
        uc = u * cw_ref[taps - 1:taps, sl] + cb_ref[:, sl]
        if stride == 1:
            carry = carry_ref[:, sl]
            for k in range(1, taps):
                uc = uc + _prev_rows(u, carry, k) * cw_ref[taps - 1 - k:taps - k, sl]
            new_buf = u[rows - (taps - 1):rows, :]
            carry_ref[:, sl] = new_buf
        else:
            ext = jnp.concatenate([buf_ref[:, sl], u], axis=0)
            for k in range(1, taps):
                s0 = (taps - 1 - k) * stride
                uc = uc + ext[s0:s0 + rows, :] * cw_ref[taps - 1 - k:taps - k, sl]
            new_buf = ext[rows:rows + (taps - 1) * stride, :]
        nbuf_ref[:, sl] = new_buf
        ucs.append(uc)
    a_list, b_list = [], []
    for n, sl in enumerate(slices):
        ucb = ucs[n].astype(BF16)
        r = jax.nn.sigmoid(_dot(ucb, wa_ref[n]) + ba_ref[:, sl])
        i = jax.nn.sigmoid(_dot(ucb, wx_ref[n]) + bx_ref[:, sl])
        nl = -lam_ref[:, sl]
        softplus = jnp.maximum(nl, 0.0) + jnp.log1p(jnp.exp(-jnp.abs(nl)))
        log_a = (-RG_C * softplus) * r
        a = jnp.exp(log_a)
        y = -jnp.tanh(log_a) * (a * a + 1.0)
        a_list.append(a)
        b_list.append(jnp.exp2(0.5 * jnp.log2(y)) * (i * ucs[n]))
    if stride == 1:
        hs = _scan_rows(a_list, b_list, [hcarry_ref[:, sl] for sl in slices])
        for n, sl in enumerate(slices):
            hcarry_ref[:, sl] = hs[n][rows - 1:rows, :]
            hlast_ref[:, sl] = hs[n][rows - 1:rows, :]
    else:
        hs = []
        for n, sl in enumerate(slices):
            h_t = h0_ref[:, sl]
            steps = []
            for t in range(rows // stride):
                rs = slice(t * stride, (t + 1) * stride)
                h_t = a_list[n][rs, :] * h_t + b_list[n][rs, :]
                steps.append(h_t)
            hs.append(jnp.concatenate(steps, axis=0))
            hlast_ref[:, sl] = h_t
    acc = jnp.zeros((rows, d), F32)
    for n, sl in enumerate(slices):
        z = _dot(xb, win_ref[:, d + n * blk:d + (n + 1) * blk])
        acc = acc + _dot((_silu(z) * hs[n]).astype(BF16), wout_ref[sl, :])
    y_ref[...] = _ln_residual(x, acc, g_ref[...], b_ref[...])


def _lru_layer(x, h0, buf, w_in, conv_w, conv_b, w_a, b_a, w_x, b_x, lam, w_out, ln_g, ln_b, *, stride, row_tile):
    bsz, t_len, d = x.shape
    n_buf = buf.shape[1]
    n_h = h0.shape[1]
    taps = conv_w.shape[0]
    nb, blk, _ = w_a.shape
    grid = (bsz, t_len // row_tile)
    const2 = lambda b, t: (0, 0)
    kern = functools.partial(_lru_layer_kernel, stride=stride)
    return pl.pallas_call(
        kern,
        grid=grid,
        in_specs=[
            pl.BlockSpec((None, row_tile, d), lambda b, t: (b, t, 0)),
            pl.BlockSpec((None, n_h, d), lambda b, t: (b, 0, 0)),
            pl.BlockSpec((None, n_buf, d), lambda b, t: (b, 0, 0)),
            pl.BlockSpec((d, 2 * d), const2),
            pl.BlockSpec((taps, d), const2),
            pl.BlockSpec((1, d), const2),
            pl.BlockSpec((nb, blk, blk), lambda b, t: (0, 0, 0)),
            pl.BlockSpec((1, d), const2),
            pl.BlockSpec((nb, blk, blk), lambda b, t: (0, 0, 0)),
            pl.BlockSpec((1, d), const2),
            pl.BlockSpec((1, d), const2),
            pl.BlockSpec((d, d), const2),
            pl.BlockSpec((1, d), const2),
            pl.BlockSpec((1, d), const2),
        ],
        out_specs=[
            pl.BlockSpec((None, row_tile, d), lambda b, t: (b, t, 0)),
            pl.BlockSpec((None, n_h, d), lambda b, t: (b, 0, 0)),
            pl.BlockSpec((None, n_buf, d), lambda b, t: (b, 0, 0)),
        ],
        out_shape=[
            jax.ShapeDtypeStruct((bsz, t_len, d), F32),
            jax.ShapeDtypeStruct((bsz, n_h, d), F32),
            jax.ShapeDtypeStruct((bsz, n_buf, d), F32),
        ],
        scratch_shapes=[pltpu.VMEM((taps - 1, d), F32), pltpu.VMEM((1, d), F32)],
        compiler_params=pltpu.CompilerParams(
            dimension_semantics=("arbitrary", "arbitrary"), vmem_limit_bytes=VMEM_LIMIT),
        name="lru_layer",
    )(x, h0, buf, w_in, conv_w, conv_b, w_a, b_a, w_x, b_x, lam, w_out, ln_g, ln_b)


D_ATT = N_HEADS * HEAD_DIM
N_ROWCOLS = 4 * N_KV * HEAD_DIM
N_WINCOLS = 2 * N_KV * HEAD_DIM
N_GATES = 3 * N_HEADS


def _nsa_proj_kernel(x_ref, w_ref, q_ref, rows_ref, win_ref, sz_ref, gates_ref, *t_refs):
    xb = x_ref[...].astype(BF16)
    c0 = 0
    q_ref[...] = (_dot(xb, w_ref[:, c0:c0 + D_ATT]) * (HEAD_DIM ** -0.5 * LOG2E)).astype(BF16)
    c0 += D_ATT
    rows = _dot(xb, w_ref[:, c0:c0 + N_ROWCOLS])
    rows_ref[...] = rows
    c0 += N_ROWCOLS
    win = _dot(xb, w_ref[:, c0:c0 + N_WINCOLS])
    win_ref[...] = win
    c0 += N_WINCOLS
    sz_ref[...] = _silu(_dot(xb, w_ref[:, c0:c0 + D_ATT])).astype(sz_ref.dtype)
    c0 += D_ATT
    gates_ref[...] = jax.nn.sigmoid(_dot(xb, w_ref[:, c0:c0 + LANES]))
    if t_refs:
        rows_t_ref, win_t_ref = t_refs
        rows_t_ref[...] = rows.T
        win_t_ref[...] = win.T


def _nsa_proj(x2, w_all, *, row_tile, seq_len=None):
    n, d = x2.shape
    n_cols = w_all.shape[1]
    widths = (D_ATT, N_ROWCOLS, N_WINCOLS, D_ATT, LANES)
    dtypes = (BF16, F32, F32, BF16, F32)
    out_specs = [pl.BlockSpec((row_tile, w), lambda i: (i, 0)) for w in widths]
    out_shape = [jax.ShapeDtypeStruct((n, w), dt) for w, dt in zip(widths, dtypes)]
    if seq_len is not None:
        tiles = seq_len // row_tile
        for w in (N_ROWCOLS, N_WINCOLS):
            out_specs.append(pl.BlockSpec((None, w, row_tile), lambda i: (i // tiles, 0, i % tiles)))
            out_shape.append(jax.ShapeDtypeStruct((n // seq_len, w, seq_len), F32))
    return pl.pallas_call(
        _nsa_proj_kernel,
        grid=(n // row_tile,),
        in_specs=[pl.BlockSpec((row_tile, d), lambda i: (i, 0)),
                  pl.BlockSpec((d, n_cols), lambda i: (0, 0))],
        out_specs=out_specs,
        out_shape=out_shape,
        compiler_params=pltpu.CompilerParams(
            dimension_semantics=("arbitrary",), vmem_limit_bytes=VMEM_LIMIT),
        name="nsa_proj",
    )(x2, w_all)


def _nsa_out_kernel(x_ref, o_ref, sz_ref, w_ref, g_ref, b_ref, y_ref):
    m = (o_ref[...].astype(F32) * sz_ref[...].astype(F32)).astype(BF16)
    y_ref[...] = _ln_residual(x_ref[...], _dot(m, w_ref[...]), g_ref[...], b_ref[...])


def _nsa_out(x2, o2, sz2, w_out, ln_g, ln_b, *, row_tile):
    n, d = x2.shape
    row = lambda i: (i, 0)
    const = lambda i: (0, 0)
    return pl.pallas_call(
        _nsa_out_kernel,
        grid=(n // row_tile,),
        in_specs=[pl.BlockSpec((row_tile, d), row), pl.BlockSpec((row_tile, d), row),
                  pl.BlockSpec((row_tile, d), row), pl.BlockSpec((d, d), const),
                  pl.BlockSpec((1, d), const), pl.BlockSpec((1, d), const)],
        out_specs=pl.BlockSpec((row_tile, d), row),
        out_shape=jax.ShapeDtypeStruct((n, d), F32),
        compiler_params=pltpu.CompilerParams(
            dimension_semantics=("arbitrary",), vmem_limit_bytes=VMEM_LIMIT),
        name="nsa_out",
    )(x2, o2, sz2, w_out, ln_g, ln_b)


def _dup_half(slab, g, lo):
    rolled = pltpu.roll(slab, HEAD_DIM, 1)
    return jnp.where(lo, slab, rolled) if g == 0 else jnp.where(lo, rolled, slab)


def _select_blocks(score, nl, real, n_slc, n_sel, axis, stride):
    rank = jnp.zeros(score.shape, jnp.int32)
    for i in range(n_slc):
        j = stride * i
        cand = score[:, j:j + 1] if axis == 1 else score[j:j + 1, :]
        beats = (cand > score) | ((cand == score) & (nl > j))
        rank = rank + beats.astype(jnp.int32)
    return real & (rank < n_sel)


def _block_scores(imp2, nl, tq, n_slc, stride):
    blk = nl >> (stride - 1)
    real = ((nl & (stride - 1)) == 0) & (blk < n_slc)
    cur = tq >> 6
    forced = (blk == 0) | ((blk <= cur) & (blk > cur - N_LOCAL))
    score = jnp.where(forced, FORCE, jnp.where(blk <= cur, imp2, NEG_INF))
    return jnp.where(real, score, -jnp.inf), real


def _alibi_query_cols(head, n_rows):
    lane = lax.broadcasted_iota(jnp.int32, (1, LANES), 1)
    rem = jnp.full((1, LANES), _slope(head) * LOG2E, F32)
    row = jnp.zeros((1, LANES), F32)
    for i in range(3):
        piece = rem.astype(BF16).astype(F32)
        rem = rem - piece
        row = jnp.where(lane == i, 16.0 * piece, jnp.where(lane == 3 + i, piece, row))
    return jnp.broadcast_to(row, (n_rows, LANES)).astype(BF16)


def _alibi_key_cols(n_rows, period):
    r = lax.broadcasted_iota(jnp.int32, (n_rows, LANES), 0) & (period - 1)
    lane = lax.broadcasted_iota(jnp.int32, (n_rows, LANES), 1)
    return jnp.where(lane < 3, (r >> 4).astype(F32), jnp.where(lane < 6, (r & 15).astype(F32), 0.0))


def _nsa_prompt_kernel(q_ref, rows_ref, win_ref, gates_ref, wc_ref, o_ref,
                       ka, vta, kwa, vwta, kcd, vct_lo, vct_hi, qa_ref, selb_ref, acc_ref):
    blk = q_ref.shape[0]
    t_len = rows_ref.shape[0]
    n_cmp = t_len // CMP_BLOCK
    n_slc = -(-t_len // SLC_BLOCK)
    n_sel = min(TOP_N, n_slc)
    n_blk = t_len // blk
    slc_per_blk = blk // SLC_BLOCK
    n_wblk = WINDOW // blk + 1
    v_rows = vta.shape[2]
    qi = pl.program_id(1)
    t0 = qi * blk
    lo = lax.broadcasted_iota(jnp.int32, (1, LANES), 1) < HEAD_DIM

    @pl.when(qi == 0)
    def _build():
        lo_t = lax.broadcasted_iota(jnp.int32, (t_len, LANES), 1) < HEAD_DIM
        lo_c = lax.broadcasted_iota(jnp.int32, (LANES, LANES), 1) < HEAD_DIM
        lo_r = lax.broadcasted_iota(jnp.int32, (LANES, 1), 0) < HEAD_DIM
        vrow = lax.broadcasted_iota(jnp.int32, (v_rows, 1), 0)
        w2 = jnp.concatenate([wc_ref[...]] * 2, axis=0)
        cw = rows_ref[:, 0:2 * LANES].reshape(n_slc, SLC_BLOCK, 2 * LANES) * w2[None]
        kvc = jnp.concatenate([jnp.sum(cw[:, 0:CMP_BLOCK, :], axis=1), jnp.sum(cw[:, CMP_BLOCK:, :], axis=1),
                               jnp.zeros((LANES - n_cmp, 2 * LANES), F32)], axis=0)
        key_cols = _alibi_key_cols(t_len, blk).astype(BF16)

        def values_t(v):
            vt = v.T[0:v_rows, :]
            return jnp.where(vrow < HEAD_DIM, vt, jnp.where(vrow == HEAD_DIM, 1.0, 0.0)).astype(BF16)

        for g in range(N_KV):
            kd = _dup_half(rows_ref[:, 2 * LANES:3 * LANES], g, lo_t).astype(BF16)
            ka[g] = jnp.concatenate([kd, key_cols], axis=1)
            vt = values_t(_dup_half(rows_ref[:, 3 * LANES:4 * LANES], g, lo_t))
            kd = _dup_half(win_ref[:, 0:LANES], g, lo_t).astype(BF16)
            kwa[g] = jnp.concatenate([kd, key_cols], axis=1)
            vwt = values_t(_dup_half(win_ref[:, LANES:2 * LANES], g, lo_t))
            for c in range(n_blk):
                vta[g, c] = vt[:, c * blk:(c + 1) * blk]
                vwta[g, c] = vwt[:, c * blk:(c + 1) * blk]
            kcd[g] = _dup_half(kvc[:, 0:LANES], g, lo_c).astype(BF16)
            vct = _dup_half(kvc[:, LANES:2 * LANES], g, lo_c).T
            vct_lo[g] = jnp.where(lo_r, vct, 0.0).astype(BF16)
            vct_hi[g] = jnp.where(lo_r, 0.0, vct).astype(BF16)

    tq = t0 + lax.broadcasted_iota(jnp.int32, (1, blk), 1)
    row_c = lax.broadcasted_iota(jnp.int32, (LANES, 1), 0)
    cmp_blk = jnp.where(row_c < n_slc, 2 * row_c, 2 * (row_c - n_slc) + 1)
    cmp_end = cmp_blk * CMP_BLOCK + (CMP_BLOCK - 1)
    valid_c = (cmp_end <= tq) & (row_c < n_cmp)
    dist_c = (tq - cmp_end).astype(F32)
    slc_r = lax.broadcasted_iota(jnp.int32, (n_slc, 1), 0)

    lr = lax.broadcasted_iota(jnp.int32, (1, blk), 1) - lax.broadcasted_iota(jnp.int32, (blk, 1), 0)
    causal_bias = jnp.where(lr >= 0, 0.0, NEG_INF)

    w_start = pl.multiple_of(jnp.maximum(t0 - WINDOW, 0), blk)
    w_blk = w_start // blk
    w_shift, w_bias = [], []
    for i in range(n_wblk):
        shift = t0 - w_start - i * blk
        dist = lr + shift
        w_shift.append(shift.astype(F32))
        w_bias.append(jnp.where((dist >= 0) & (dist < WINDOW), 0.0, NEG_INF))

    gates_t = gates_ref[...].T

    for hh in range(N_HEADS):
        q2 = q_ref[:, (hh // 2) * LANES:(hh // 2 + 1) * LANES]
        zero = jnp.zeros_like(q2)
        qm = jnp.where(lo, q2, zero) if hh % 2 == 0 else jnp.where(lo, zero, q2)
        qa_ref[hh] = jnp.concatenate([qm, _alibi_query_cols(hh, blk)], axis=1)

    o_cmp = []
    for g in range(N_KV):
        imp = jnp.zeros((LANES, blk), F32)
        for j in range(HPG // 2):
            oc = jnp.zeros((LANES, blk), F32)
            for hh, v_ref in ((g * HPG + 2 * j, vct_lo), (g * HPG + 2 * j + 1, vct_hi)):
                s = _dot_nt(kcd[g], qa_ref[hh, :, 0:LANES]) - (_slope(hh) * LOG2E) * dist_c
                s = jnp.where(valid_c, s, NEG_INF)
                e = jnp.exp2(s - jnp.max(s, axis=0, keepdims=True))
                p = jnp.where(valid_c, e, 0.0) / jnp.sum(e, axis=0, keepdims=True)
                imp = imp + p
                oc = oc + _dot(v_ref[g], p.astype(BF16))
            o_cmp.append(oc)

        imp2 = imp[0:n_slc, :] + imp[n_slc:2 * n_slc, :]
        score, real = _block_scores(imp2, slc_r, tq, n_slc, stride=1)
        sel = _select_blocks(score, slc_r, real, n_slc, n_sel, axis=0, stride=1)
        sel_bias = jnp.where(sel, 0.0, NEG_INF)
        for c in range(n_blk):
            rows = sel_bias[c * slc_per_blk:(c + 1) * slc_per_blk, :]
            selb_ref[g, c] = jnp.concatenate([rows] * (SUBLANES // slc_per_blk), axis=0)

    acc_ref[...] = jnp.zeros(acc_ref.shape, F32)

    def key_block(c, ms, extra_bias):
        off = pl.multiple_of(c * blk, blk)
        shift = (t0 - off).astype(F32)
        keys, biases = [], []
        for g in range(N_KV):
            keys.append(ka[g, pl.ds(off, blk), :])
            sel_rows = selb_ref[g, c]
            bias = jnp.concatenate([jnp.broadcast_to(sel_rows[m:m + 1, :], (SLC_BLOCK, blk))
                                    for m in range(slc_per_blk)], axis=0)
            biases.append(bias if extra_bias is None else bias + extra_bias)
        scores, new_ms, probs = [], [], []
        for hh in range(N_HEADS):
            s = _dot_nt(keys[hh // HPG], qa_ref[hh]) + biases[hh // HPG]
            c_h = (_slope(hh) * LOG2E) * shift
            new_ms.append(jnp.maximum(ms[hh], jnp.max(s, axis=0, keepdims=True) - c_h))
            scores.append(s)
        for hh in range(N_HEADS):
            c_h = (_slope(hh) * LOG2E) * shift
            probs.append(jnp.exp2((scores[hh] - (new_ms[hh] + c_h)).astype(BF16)))
        for hh in range(N_HEADS):
            acc_ref[hh] = acc_ref[hh] * jnp.exp2(ms[hh] - new_ms[hh]) + _dot(vta[hh // HPG, c], probs[hh])
        return tuple(new_ms)

    ms = lax.fori_loop(0, qi, lambda c, ms: key_block(c, ms, None),
                       (jnp.full((1, blk), NEG_INF, F32),) * N_HEADS)
    key_block(qi, ms, causal_bias)

    for g in range(N_KV):
        k_win = [kwa[g, pl.ds(pl.multiple_of(w_start + i * blk, blk), blk), :] for i in range(n_wblk)]
        w_scores, w_max = [], []
        for hh in range(g * HPG, (g + 1) * HPG):
            slope = _slope(hh) * LOG2E
            tiles, m_w = [], None
            for i in range(n_wblk):
                s = _dot_nt(k_win[i], qa_ref[hh]) + w_bias[i]
                m_i = jnp.max(s, axis=0, keepdims=True) - slope * w_shift[i]
                m_w = m_i if m_w is None else jnp.maximum(m_w, m_i)
                tiles.append(s)
            w_scores.append(tiles)
            w_max.append(m_w)
        o_wins = []
        for h in range(HPG):
            slope = _slope(g * HPG + h) * LOG2E
            acc_w = jnp.zeros((v_rows, blk), F32)
            for i in range(n_wblk):
                p = jnp.exp2((w_scores[h][i] - (w_max[h] + slope * w_shift[i])).astype(BF16))
                acc_w = acc_w + _dot(vwta[g, w_blk + i], p)
            o_wins.append(acc_w[0:HEAD_DIM, :] / acc_w[HEAD_DIM:HEAD_DIM + 1, :])

        for j in range(HPG // 2):
            outs = []
            for hh in (g * HPG + 2 * j, g * HPG + 2 * j + 1):
                o_win = o_wins[hh - g * HPG]
                acc = acc_ref[hh]
                o_sel = acc[0:HEAD_DIM, :] / acc[HEAD_DIM:HEAD_DIM + 1, :]
                half = (hh % 2) * HEAD_DIM
                gate = lambda c: gates_t[c * N_HEADS + hh:c * N_HEADS + hh + 1, :]
                o_c = o_cmp[g * (HPG // 2) + j][half:half + HEAD_DIM, :]
                outs.append(gate(0) * o_c + gate(1) * o_sel + gate(2) * o_win)
            c0 = g * HPG * HEAD_DIM + j * LANES
            o_ref[:, c0:c0 + LANES] = jnp.concatenate(outs, axis=0).T.astype(o_ref.dtype)


def _block_expand_matrix(n_rows, n_keys):
    n = jnp.arange(n_rows, dtype=jnp.int32)[:, None]
    p = jnp.arange(n_keys, dtype=jnp.int32)[None, :]
    return (n == 2 * (p // SLC_BLOCK)).astype(BF16)


def _nsa_prompt_attention(q, rows, win, gates, wc, *, q_tile=256):
    bsz, t_len, _ = q.shape
    assert WINDOW % q_tile == 0 and q_tile % (2 * LANES) == 0 and t_len >= WINDOW + q_tile
    assert t_len // CMP_BLOCK <= LANES and t_len % SLC_BLOCK == 0 and SUBLANES % (q_tile // SLC_BLOCK) == 0
    v_rows = HEAD_DIM + 16
    k_scratch = pltpu.VMEM((N_KV, t_len, 2 * LANES), BF16)
    v_scratch = pltpu.VMEM((N_KV, t_len // q_tile, v_rows, q_tile), BF16)
    c_scratch = pltpu.VMEM((N_KV, LANES, LANES), BF16)
    return pl.pallas_call(
        _nsa_prompt_kernel,
        grid=(bsz, t_len // q_tile),
        in_specs=[
            pl.BlockSpec((None, q_tile, D_ATT), lambda b, i: (b, i, 0)),
            pl.BlockSpec((None, t_len, N_ROWCOLS), lambda b, i: (b, 0, 0)),
            pl.BlockSpec((None, t_len, N_WINCOLS), lambda b, i: (b, 0, 0)),
            pl.BlockSpec((None, q_tile, LANES), lambda b, i: (b, i, 0)),
            pl.BlockSpec((CMP_BLOCK, 2 * LANES), lambda b, i: (0, 0)),
        ],
        out_specs=pl.BlockSpec((None, q_tile, D_ATT), lambda b, i: (b, i, 0)),
        out_shape=jax.ShapeDtypeStruct((bsz, t_len, D_ATT), BF16),
        scratch_shapes=[k_scratch, v_scratch, k_scratch, v_scratch,
                        c_scratch, c_scratch, c_scratch,
                        pltpu.VMEM((N_HEADS, q_tile, 2 * LANES), BF16),
                        pltpu.VMEM((N_KV, t_len // q_tile, SUBLANES, q_tile), F32),
                        pltpu.VMEM((N_HEADS, v_rows, q_tile), F32)],
        compiler_params=pltpu.CompilerParams(
            dimension_semantics=("arbitrary", "arbitrary"), vmem_limit_bytes=VMEM_LIMIT),
        name="nsa_prompt_attention",
    )(q, rows, win, gates, wc)


def _nsa_sample_kernel(pt_ref, *refs, pages_per_step, n_steps, n_new):
    page_refs = refs[:pages_per_step]
    (qbd_ref, new_ref, winbuf_ref, gates_ref, wc_ref, e_ref, o_ref, s_ref, vst_ref, kcs, vcs) = refs[pages_per_step:]
    del pt_ref
    step = pl.program_id(1)
    keys_per_step = pages_per_step * PAGE_SIZE
    p_len = n_steps * keys_per_step
    n_keys = p_len + LANES
    n_cmp = p_len // CMP_BLOCK
    n_slc = -(-(p_len + n_new) // SLC_BLOCK)
    n_sel = min(TOP_N, n_slc)
    cmp_per_step = keys_per_step // CMP_BLOCK
    n_rows = N_KV * n_new * HPG
    gd = N_KV * HEAD_DIM

    def slab(r, kind):
        return r[kind].reshape(gd, r.shape[-1])

    cm = jnp.concatenate([jnp.concatenate([slab(r, 0).T, slab(r, 1).T], axis=1) for r in page_refs], axis=0)
    kvc = jnp.sum(cm.reshape(cmp_per_step, CMP_BLOCK, 2 * LANES) * wc_ref[...][None], axis=1)
    c_off = pl.multiple_of(step * cmp_per_step, cmp_per_step)
    kcs[pl.ds(c_off, cmp_per_step), :] = kvc[:, 0:LANES].astype(BF16)
    vcs[pl.ds(c_off, cmp_per_step), :] = kvc[:, LANES:2 * LANES].astype(BF16)
    s_ref[step] = _dot(qbd_ref[...], jnp.concatenate([slab(r, 2).astype(BF16) for r in page_refs], axis=1))
    vst_ref[step] = jnp.concatenate([slab(r, 3).astype(BF16) for r in page_refs], axis=1)

    @pl.when(step == n_steps - 1)
    def _attend():
        n_wb = winbuf_ref.shape[-1]
        w_keys = n_wb + LANES
        qbd = qbd_ref[...]
        r = lax.broadcasted_iota(jnp.int32, (n_rows, 1), 0)
        head = (r // (n_new * HPG)) * HPG + (r % HPG)
        slope = jnp.exp2((head + 1).astype(F32) * (-8.0 / N_HEADS)) * LOG2E
        tq = p_len + (r // HPG) % n_new

        def attend(s, values_t, dist, bias):
            s = s - slope * dist.astype(F32) + bias
            e = jnp.exp2(s - jnp.max(s, axis=-1, keepdims=True))
            eb = e.astype(BF16)
            o = sum(_dot_nt(eb[:, ks], vt) for ks, vt in values_t)
            return o / jnp.sum(e, axis=-1, keepdims=True)

        nl = lax.broadcasted_iota(jnp.int32, (n_rows, n_cmp), 1)
        cmp_end = nl * CMP_BLOCK + (CMP_BLOCK - 1)
        valid_c = cmp_end <= tq
        s = _dot_nt(qbd, kcs[...]) - slope * (tq - cmp_end).astype(F32)
        s = jnp.where(valid_c, s, NEG_INF)
        e = jnp.exp2(s - jnp.max(s, axis=-1, keepdims=True))
        p = jnp.where(valid_c, e, 0.0) / jnp.sum(e, axis=-1, keepdims=True)
        o_cmp = _dot(p.astype(BF16), vcs[...])

        n_gt = N_KV * n_new
        imp = jnp.sum(p.reshape(n_gt, HPG, n_cmp), axis=1)
        imp2 = imp + pltpu.roll(imp, n_cmp - 1, 1)
        w_sel = -(-2 * n_slc // LANES) * LANES
        imp2 = jnp.concatenate([imp2, jnp.zeros((n_gt, w_sel - n_cmp), F32)], axis=1)
        nl8 = lax.broadcasted_iota(jnp.int32, (n_gt, w_sel), 1)
        r8 = lax.broadcasted_iota(jnp.int32, (n_gt, 1), 0)
        score, real = _block_scores(imp2, nl8, p_len + r8 % n_new, n_slc, stride=2)
        sel = _select_blocks(score, nl8, real, n_slc, n_sel, axis=1, stride=2)
        selb = jnp.where(sel, 1.0, 0.0)
        selb = jnp.broadcast_to(selb[:, None, :], (n_gt, HPG, w_sel)).reshape(n_rows, w_sel).astype(BF16)
        keys_per_tile = e_ref.shape[1]
        hit = [_dot(selb[:, j * LANES:(j + 1) * LANES], e_ref[...]) for j in range(p_len // keys_per_tile)]
        hit.append(jnp.broadcast_to(selb[:, 2 * (p_len // SLC_BLOCK):2 * (p_len // SLC_BLOCK) + 1].astype(F32),
                                    (n_rows, LANES)))
        hit = jnp.concatenate(hit, axis=1)
        dist = tq - lax.broadcasted_iota(jnp.int32, (n_rows, n_keys), 1)
        new_k, new_v, new_kw, new_vw = (new_ref[i] for i in range(4))
        s_sel = jnp.concatenate([s_ref[i] for i in range(n_steps)] + [_dot(qbd, new_k)], axis=1)
        values_t = [(slice(i * keys_per_step, (i + 1) * keys_per_step), vst_ref[i]) for i in range(n_steps)]
        o_sel = attend(s_sel, values_t + [(slice(p_len, n_keys), new_v)], dist,
                       jnp.where((hit > 0.5) & (dist >= 0), 0.0, NEG_INF))

        dist_w = tq - (p_len - n_wb) - lax.broadcasted_iota(jnp.int32, (n_rows, w_keys), 1)
        s_win = jnp.concatenate([_dot(qbd, slab(winbuf_ref, 0).astype(BF16)), _dot(qbd, new_kw)], axis=1)
        o_win = attend(s_win, [(slice(0, n_wb), slab(winbuf_ref, 1).astype(BF16)), (slice(n_wb, w_keys), new_vw)],
                       dist_w, jnp.where((dist_w >= 0) & (dist_w < WINDOW), 0.0, NEG_INF))

        o_ref[...] = (gates_ref[:, 0:1] * o_cmp + gates_ref[:, 1:2] * o_sel + gates_ref[:, 2:3] * o_win)


def _nsa_sample_attention(page_table, cache_t, qbd, new_t, winbuf_t, gates, wc, *, n_new, pages_per_step=64):
    bsz, n_pages = page_table.shape
    n_steps = n_pages // pages_per_step
    keys_per_step = pages_per_step * PAGE_SIZE
    p_len = n_pages * PAGE_SIZE
    n_rows = qbd.shape[1]
    n_wb = winbuf_t.shape[-1]
    n_cmp = p_len // CMP_BLOCK
    gd = N_KV * HEAD_DIM
    keys_per_tile = (LANES // 2) * SLC_BLOCK
    assert p_len % keys_per_tile == 0 and n_new <= SLC_BLOCK
    e_mat = _block_expand_matrix(LANES, keys_per_tile)
    kern = functools.partial(_nsa_sample_kernel, pages_per_step=pages_per_step, n_steps=n_steps, n_new=n_new)

    def page_spec(k):
        return pl.BlockSpec((None, 4, N_KV, HEAD_DIM, PAGE_SIZE),
                            lambda b, s, pt: (pt[b, s * pages_per_step + k], 0, 0, 0, 0))

    per_b = lambda b, s, pt: (b, 0, 0)
    const = lambda b, s, pt: (0, 0)
    grid_spec = pltpu.PrefetchScalarGridSpec(
        num_scalar_prefetch=1,
        grid=(bsz, n_steps),
        in_specs=[page_spec(k) for k in range(pages_per_step)] + [
            pl.BlockSpec((None, n_rows, LANES), per_b),
            pl.BlockSpec((None, 4, gd, LANES), lambda b, s, pt: (b, 0, 0, 0)),
            pl.BlockSpec((None, 2, N_KV, HEAD_DIM, n_wb), lambda b, s, pt: (b, 0, 0, 0, 0)),
            pl.BlockSpec((None, n_rows, LANES), per_b),
            pl.BlockSpec((CMP_BLOCK, 2 * LANES), const),
            pl.BlockSpec((LANES, keys_per_tile), const),
        ],
        out_specs=pl.BlockSpec((None, n_rows, LANES), per_b),
        scratch_shapes=[
            pltpu.VMEM((n_steps, n_rows, keys_per_step), F32),
            pltpu.VMEM((n_steps, gd, keys_per_step), BF16),
            pltpu.VMEM((n_cmp, LANES), BF16), pltpu.VMEM((n_cmp, LANES), BF16),
        ],
    )
    return pl.pallas_call(
        kern,
        grid_spec=grid_spec,
        out_shape=jax.ShapeDtypeStruct((bsz, n_rows, LANES), F32),
        compiler_params=pltpu.CompilerParams(
            dimension_semantics=("arbitrary", "arbitrary"), vmem_limit_bytes=VMEM_LIMIT),
        name="nsa_sample_attention",
    )(page_table, *([cache_t] * pages_per_step), qbd, new_t, winbuf_t, gates, wc, e_mat)


def _nsa_weights(w_in):
    d = w_in.shape[0]
    c_kv = D_ATT
    c_g = c_kv + N_ROWCOLS + N_WINCOLS
    c_z = c_g + N_GATES
    pad = jnp.zeros((d, LANES - N_GATES), w_in.dtype)
    return jnp.concatenate([w_in[:, :c_g], w_in[:, c_z:], w_in[:, c_g:c_z], pad], axis=1).astype(BF16)


def _cmp_weight_tile(w_cmp):
    return w_cmp.transpose(1, 0, 2, 3).reshape(CMP_BLOCK, 2 * N_KV * HEAD_DIM)


def _nsa_layer_prompt(x, w_all, wc, w_out, ln_g, ln_b):
    bsz, t_len, d = x.shape
    x2 = x.reshape(bsz * t_len, d)
    q, rows, win, sz, gates, rows_t, win_t = _nsa_proj(x2, w_all, row_tile=512, seq_len=t_len)
    o = _nsa_prompt_attention(q.reshape(bsz, t_len, D_ATT), rows.reshape(bsz, t_len, N_ROWCOLS),
                              win.reshape(bsz, t_len, N_WINCOLS), gates.reshape(bsz, t_len, LANES), wc)
    y = _nsa_out(x2, o.reshape(bsz * t_len, D_ATT), sz, w_out, ln_g, ln_b, row_tile=512)
    keep = min(WINDOW, t_len)
    rows_out = rows_t.reshape(bsz, 4, N_KV, HEAD_DIM, t_len).transpose(0, 4, 1, 2, 3)
    win_out = win_t[:, :, t_len - keep:].reshape(bsz, 2, N_KV, HEAD_DIM, keep).transpose(0, 4, 1, 2, 3)
    return y.reshape(bsz, t_len, d), rows_out, win_out


def _nsa_layer_sample(x, cache, win_buf, page_table, w_all, wc, w_out, ln_g, ln_b):
    bsz, n_new, d = x.shape
    x2 = x.reshape(bsz * n_new, d)
    q, rows, win, sz, gates = _nsa_proj(x2, w_all, row_tile=bsz * n_new)
    q5 = q.reshape(bsz, n_new, N_KV, HPG, HEAD_DIM).transpose(0, 2, 1, 3, 4)
    eye = jnp.eye(N_KV, dtype=q.dtype)
    qbd = (q5[:, :, :, :, None, :] * eye[None, :, None, None, :, None]).reshape(
        bsz, N_KV * n_new * HPG, N_KV * HEAD_DIM)
    g5 = gates[:, :N_GATES].reshape(bsz, n_new, 3, N_KV, HPG).transpose(0, 3, 1, 4, 2)
    g_rows = jnp.pad(g5.reshape(bsz, N_KV * n_new * HPG, 3), ((0, 0), (0, 0), (0, LANES - 3)))
    rows3 = rows.reshape(bsz, n_new, N_ROWCOLS)
    win3 = win.reshape(bsz, n_new, N_WINCOLS)
    n_wb = win_buf.shape[1]
    gd = N_KV * HEAD_DIM
    new_t = jnp.stack([rows3[:, :, 2 * gd:3 * gd], rows3[:, :, 3 * gd:4 * gd],
                       win3[:, :, 0:gd], win3[:, :, gd:2 * gd]], axis=1).transpose(0, 1, 3, 2)
    new_t = jnp.pad(new_t, ((0, 0), (0, 0), (0, 0), (0, LANES - n_new))).astype(BF16)
    o_rows = _nsa_sample_attention(
        page_table, cache.transpose(0, 2, 3, 4, 1), qbd, new_t, win_buf.transpose(0, 2, 3, 4, 1),
        g_rows, wc, n_new=n_new)
    o6 = o_rows.reshape(bsz, N_KV, n_new, HPG, N_KV, HEAD_DIM)
    o = jnp.stack([o6[:, g, :, :, g, :] for g in range(N_KV)], axis=2).reshape(bsz * n_new, D_ATT)
    y = _nsa_out(x2, o, sz, w_out, ln_g, ln_b, row_tile=bsz * n_new)
    all_win = jnp.concatenate([win_buf, win3.reshape(bsz, n_new, 2, N_KV, HEAD_DIM)], axis=1)
    return (y.reshape(bsz, n_new, d), rows3.reshape(bsz, n_new, 4, N_KV, HEAD_DIM),
            all_win[:, all_win.shape[1] - n_wb:])


def _time_major(a):
    bsz, k, d = a.shape
    return a.transpose(1, 0, 2).reshape(1, k * bsz, d)


def _batch_major(a, bsz):
    _, kb, d = a.shape
    return a.reshape(kb // bsz, bsz, d).transpose(1, 0, 2)


def kernel(x_prompt, x_sample, cache_nsa_kv, cache_nsa_win, state_conv_a, state_lru_h, state_lru_conv, page_table,
           ln_g, ln_b, a_w_in, a_conv_w, a_w_out, b_w_in, b_w_cmp, b_w_out,
           c_w_in, c_conv_w, c_conv_b, c_w_a, c_b_a, c_w_x, c_b_x, c_lam, c_w_out):
    bp, t_len, d = x_prompt.shape
    bs, n_new, _ = x_sample.shape
    yp = x_prompt
    ys = _time_major(x_sample)
    conv_a_p, conv_a_s = [], []
    rows_p, rows_s, win_p, win_s = [], [], [], []
    h_p, h_s, cc_p, cc_s = [], [], [], []
    row2 = lambda v: v.reshape(1, -1)
    a_w_in_b, a_w_out_b = a_w_in.astype(BF16), a_w_out.astype(BF16)
    for i in range(DEPTH):
        kind, j = i % 3, i // 3
        g, b = row2(ln_g[i]), row2(ln_b[i])
        if kind == 0:
            taps = a_conv_w.shape[1]
            yp, sp = _conv_layer(yp, jnp.zeros((bp, taps - 1, d), F32), a_w_in_b, a_conv_w, a_w_out_b, g, b,
                                 layer=j, stride=1, row_tile=512)
            ys, ss = _conv_layer(ys, _time_major(state_conv_a[j]), a_w_in_b, a_conv_w, a_w_out_b, g, b,
                                 layer=j, stride=bs, row_tile=n_new * bs)
            conv_a_p.append(sp)
            conv_a_s.append(_batch_major(ss, bs))
        elif kind == 1:
            w_all = _nsa_weights(b_w_in[j])
            wc = _cmp_weight_tile(b_w_cmp[j])
            w_out = b_w_out[j].astype(BF16)
            yp, rp, wp = _nsa_layer_prompt(yp, w_all, wc, w_out, g, b)
            ys_b, rs, ws = _nsa_layer_sample(_batch_major(ys, bs), cache_nsa_kv[j], cache_nsa_win[j], page_table,
                                             w_all, wc, w_out, g, b)
            ys = _time_major(ys_b)
            rows_p.append(rp)
            rows_s.append(rs)
            win_p.append(wp)
            win_s.append(ws)
        else:
            w_in, w_out = c_w_in[j].astype(BF16), c_w_out[j].astype(BF16)
            w_a, w_x = c_w_a[j].astype(BF16), c_w_x[j].astype(BF16)
            taps = c_conv_w.shape[1]
            args = (w_in, c_conv_w[j], row2(c_conv_b[j]), w_a, row2(c_b_a[j]), w_x, row2(c_b_x[j]),
                    row2(c_lam[j]), w_out, g, b)
            yp, hp, cp = _lru_layer(yp, jnp.zeros((bp, 1, d), F32), jnp.zeros((bp, taps - 1, d), F32), *args,
                                    stride=1, row_tile=256)
            ys, hs, cs = _lru_layer(ys, state_lru_h[j].reshape(1, bs, d), _time_major(state_lru_conv[j]), *args,
                                    stride=bs, row_tile=n_new * bs)
            h_p.append(hp.reshape(bp, d))
            h_s.append(hs.reshape(bs, d))
            cc_p.append(cp)
            cc_s.append(_batch_major(cs, bs))
    return (yp, _batch_major(ys, bs), jnp.stack(conv_a_p), jnp.stack(conv_a_s), jnp.stack(rows_p), jnp.stack(rows_s),
            jnp.stack(win_p), jnp.stack(win_s), jnp.stack(h_p), jnp.stack(h_s), jnp.stack(cc_p), jnp.stack(cc_s))
```

```python
import functools

import jax
import jax.numpy as jnp
from jax import lax
from jax.experimental import pallas as pl
from jax.experimental.pallas import tpu as pltpu

F32 = jnp.float32
BF16 = jnp.bfloat16

DEPTH = 4
N_HEADS = 16
N_KV = 2
HPG = N_HEADS // N_KV
HEAD_DIM = 64
CMP_BLOCK = 32
SLC_BLOCK = 64
TOP_N = 16
N_LOCAL = 2
WINDOW = 512
PAGE_SIZE = 128
NEG_INF = -1e30
FORCE = 1e9
RG_C = 8.0
N_RG_BLOCKS = 4
ALPHA = (2 * DEPTH) ** 0.25
LN_EPS = 1e-5
LOG2E = 1.4426950408889634

LANES = 128
SUBLANES = 8
VMEM_LIMIT = 56 * 1024 * 1024


def _dot(a, b):
    return jnp.dot(a, b, preferred_element_type=F32)


def _dot_nt(a, b):
    return lax.dot_general(a, b, (((1,), (1,)), ((), ())), preferred_element_type=F32)


def _silu(z):
    return z * jax.nn.sigmoid(z)


def _ln_residual(x, y, g, b):
    r = ALPHA * x + y
    mu = jnp.mean(r, axis=-1, keepdims=True)
    c = r - mu
    var = jnp.mean(c * c, axis=-1, keepdims=True)
    return c * lax.rsqrt(var + LN_EPS) * g + b


def _slope(head):
    return 2.0 ** (-8.0 * (head + 1) / N_HEADS)


def _prev_rows(u, carry, k):
    n_carry = carry.shape[0]
    rolled = pltpu.roll(u, k, 0)
    row = lax.broadcasted_iota(jnp.int32, (SUBLANES, 1), 0)
    head = rolled[0:SUBLANES, :]
    for i in range(k):
        head = jnp.where(row == i, carry[n_carry - k + i:n_carry - k + i + 1, :], head)
    return jnp.concatenate([head, rolled[SUBLANES:, :]], axis=0)


def _conv_layer_kernel(x_ref, buf_ref, win_ref, cw_ref, wout_ref, g_ref, b_ref,
                       y_ref, nbuf_ref, carry_ref, *, stride, col_chunk):
    rows, d = x_ref.shape
    taps = cw_ref.shape[0]
    x = x_ref[...]
    xb = x.astype(BF16)
    if stride == 1:
        @pl.when(pl.program_id(1) == 0)
        def _():
            carry_ref[...] = buf_ref[...]
    acc = jnp.zeros((rows, d), F32)
    for j in range(d // col_chunk):
        lo = j * col_chunk
        sl = slice(lo, lo + col_chunk)
        h, bg, cg, z = [_dot(xb, win_ref[:, part * d + lo:part * d + lo + col_chunk]) for part in range(4)]
        u = cg * h
        conv = u * cw_ref[taps - 1:taps, sl]
        if stride == 1:
            carry = carry_ref[:, sl]
            for k in range(1, taps):
                conv = conv + _prev_rows(u, carry, k) * cw_ref[taps - 1 - k:taps - k, sl]
            new_buf = u[rows - (taps - 1):rows, :]
            carry_ref[:, sl] = new_buf
        else:
            ext = jnp.concatenate([buf_ref[:, sl], u], axis=0)
            for k in range(1, taps):
                s0 = (taps - 1 - k) * stride
                conv = conv + ext[s0:s0 + rows, :] * cw_ref[taps - 1 - k:taps - k, sl]
            new_buf = ext[rows:rows + (taps - 1) * stride, :]
        nbuf_ref[:, sl] = new_buf
        m = _silu(z) * bg * conv
        acc = acc + _dot(m.astype(BF16), wout_ref[sl, :])
    y_ref[...] = _ln_residual(x, acc, g_ref[...], b_ref[...])


def _conv_layer(x, buf, w_in, conv_w, w_out, ln_g, ln_b, *, layer, stride, row_tile):
    bsz, t_len, d = x.shape
    n_buf = buf.shape[1]
    taps = conv_w.shape[1]
    grid = (bsz, t_len // row_tile)
    kern = functools.partial(_conv_layer_kernel, stride=stride, col_chunk=512)
    return pl.pallas_call(
        kern,
        grid=grid,
        in_specs=[
            pl.BlockSpec((None, row_tile, d), lambda b, t: (b, t, 0)),
            pl.BlockSpec((None, n_buf, d), lambda b, t: (b, 0, 0)),
            pl.BlockSpec((None, d, 4 * d), lambda b, t: (layer, 0, 0)),
            pl.BlockSpec((None, taps, d), lambda b, t: (layer, 0, 0)),
            pl.BlockSpec((None, d, d), lambda b, t: (layer, 0, 0)),
            pl.BlockSpec((1, d), lambda b, t: (0, 0)),
            pl.BlockSpec((1, d), lambda b, t: (0, 0)),
        ],
        out_specs=[
            pl.BlockSpec((None, row_tile, d), lambda b, t: (b, t, 0)),
            pl.BlockSpec((None, n_buf, d), lambda b, t: (b, 0, 0)),
        ],
        out_shape=[
            jax.ShapeDtypeStruct((bsz, t_len, d), F32),
            jax.ShapeDtypeStruct((bsz, n_buf, d), F32),
        ],
        scratch_shapes=[pltpu.VMEM((taps - 1, d), F32)],
        compiler_params=pltpu.CompilerParams(
            dimension_semantics=("arbitrary", "arbitrary"), vmem_limit_bytes=VMEM_LIMIT),
        name="conv_layer",
    )(x, buf, w_in, conv_w, w_out, ln_g, ln_b)


def _scan_rows(a_list, b_list, h_prev_list):
    rows, cols = a_list[0].shape
    groups = rows // SUBLANES
    a3 = [a.reshape(groups, SUBLANES, cols) for a in a_list]
    b3 = [b.reshape(groups, SUBLANES, cols) for b in b_list]
    sub = lax.broadcasted_iota(jnp.int32, (1, SUBLANES, 1), 1)
    s = 1
    while s < SUBLANES:
        live = sub >= s
        for n, (a, b) in enumerate(zip(a3, b3)):
            a_sh = jnp.where(live, pltpu.roll(a, s, 1), 1.0)
            b_sh = jnp.where(live, pltpu.roll(b, s, 1), 0.0)
            b3[n] = a * b_sh + b
            a3[n] = a * a_sh
        s *= 2
    carries = list(h_prev_list)
    outs = [[] for _ in a3]
    for j in range(groups):
        for n in range(len(a3)):
            h = a3[n][j] * carries[n] + b3[n][j]
            carries[n] = h[SUBLANES - 1:SUBLANES, :]
            outs[n].append(h)
    return [jnp.concatenate(o, axis=0) for o in outs]


def _lru_layer_kernel(x_ref, h0_ref, buf_ref, win_ref, cw_ref, cb_ref, wa_ref, ba_ref, wx_ref, bx_ref,
                      lam_ref, wout_ref, g_ref, b_ref,
                      y_ref, hlast_ref, nbuf_ref, carry_ref, hcarry_ref, *, stride, n_sub):
    d = x_ref.shape[1]
    rows = x_ref.shape[0] // n_sub
    taps = cw_ref.shape[0]
    n_blocks, blk, _ = wa_ref.shape
    slices = [slice(n * blk, (n + 1) * blk) for n in range(n_blocks)]
    if stride == 1:
        @pl.when(pl.program_id(1) == 0)
        def _():
            carry_ref[...] = buf_ref[...]
            hcarry_ref[...] = h0_ref[...]
    else:
        assert n_sub == 1

    def conv_stage(s):
        x = x_ref[s * rows:(s + 1) * rows, :]
        xb = x.astype(BF16)
        ucs = []
        for sl in slices:
            u = _dot(xb, win_ref[:, sl])
            uc = u * cw_ref[taps - 1:taps, sl] + cb_ref[:, sl]
            if stride == 1:
                carry = carry_ref[:, sl]
                for k in range(1, taps):
                    uc = uc + _prev_rows(u, carry, k) * cw_ref[taps - 1 - k:taps - k, sl]
                new_buf = u[rows - (taps - 1):rows, :]
                carry_ref[:, sl] = new_buf
            else:
                ext = jnp.concatenate([buf_ref[:, sl], u], axis=0)
                for k in range(1, taps):
                    s0 = (taps - 1 - k) * stride
                    uc = uc + ext[s0:s0 + rows, :] * cw_ref[taps - 1 - k:taps - k, sl]
                new_buf = ext[rows:rows + (taps - 1) * stride, :]
            nbuf_ref[:, sl] = new_buf
            ucs.append(uc)
        return x, xb, ucs

    def recurrence_stage(ucs):
        a_list, b_list = [], []
        for n, sl in enumerate(slices):
            ucb = ucs[n].astype(BF16)
            r = jax.nn.sigmoid(_dot(ucb, wa_ref[n]) + ba_ref[:, sl])
            i = jax.nn.sigmoid(_dot(ucb, wx_ref[n]) + bx_ref[:, sl])
            nl = -lam_ref[:, sl]
            softplus = jnp.maximum(nl, 0.0) + jnp.log1p(jnp.exp(-jnp.abs(nl)))
            log_a = (-RG_C * softplus) * r
            a = jnp.exp(log_a)
            y = -jnp.tanh(log_a) * (a * a + 1.0)
            a_list.append(a)
            b_list.append(jnp.exp2(0.5 * jnp.log2(y)) * (i * ucs[n]))
        if stride == 1:
            hs = _scan_rows(a_list, b_list, [hcarry_ref[:, sl] for sl in slices])
            for n, sl in enumerate(slices):
                hcarry_ref[:, sl] = hs[n][rows - 1:rows, :]
                hlast_ref[:, sl] = hs[n][rows - 1:rows, :]
            return hs
        hs = []
        for n, sl in enumerate(slices):
            h_t = h0_ref[:, sl]
            steps = []
            for t in range(rows // stride):
                rs = slice(t * stride, (t + 1) * stride)
                h_t = a_list[n][rs, :] * h_t + b_list[n][rs, :]
                steps.append(h_t)
            hs.append(jnp.concatenate(steps, axis=0))
            hlast_ref[:, sl] = h_t
        return hs

    def output_stage(s, x, xb, hs):
        acc = jnp.zeros((rows, d), F32)
        for n, sl in enumerate(slices):
            z = _dot(xb, win_ref[:, d + n * blk:d + (n + 1) * blk])
            acc = acc + _dot((_silu(z) * hs[n]).astype(BF16), wout_ref[sl, :])
        y_ref[s * rows:(s + 1) * rows, :] = _ln_residual(x, acc, g_ref[...], b_ref[...])

    staged = conv_stage(0)
    for s in range(n_sub):
        x, xb, ucs = staged
        if s + 1 < n_sub:
            staged = conv_stage(s + 1)
        output_stage(s, x, xb, recurrence_stage(ucs))


def _lru_layer(x, h0, buf, w_in, conv_w, conv_b, w_a, b_a, w_x, b_x, lam, w_out, ln_g, ln_b, *, stride, row_tile):
    bsz, t_len, d = x.shape
    n_buf = buf.shape[1]
    n_h = h0.shape[1]
    taps = conv_w.shape[0]
    nb, blk, _ = w_a.shape
    grid = (bsz, t_len // row_tile)
    const2 = lambda b, t: (0, 0)
    kern = functools.partial(_lru_layer_kernel, stride=stride, n_sub=2 if stride == 1 else 1)
    return pl.pallas_call(
        kern,
        grid=grid,
        in_specs=[
            pl.BlockSpec((None, row_tile, d), lambda b, t: (b, t, 0)),
            pl.BlockSpec((None, n_h, d), lambda b, t: (b, 0, 0)),
            pl.BlockSpec((None, n_buf, d), lambda b, t: (b, 0, 0)),
            pl.BlockSpec((d, 2 * d), const2),
            pl.BlockSpec((taps, d), const2),
            pl.BlockSpec((1, d), const2),
            pl.BlockSpec((nb, blk, blk), lambda b, t: (0, 0, 0)),
            pl.BlockSpec((1, d), const2),
            pl.BlockSpec((nb, blk, blk), lambda b, t: (0, 0, 0)),
            pl.BlockSpec((1, d), const2),
            pl.BlockSpec((1, d), const2),
            pl.BlockSpec((d, d), const2),
            pl.BlockSpec((1, d), const2),
            pl.BlockSpec((1, d), const2),
        ],
        out_specs=[
            pl.BlockSpec((None, row_tile, d), lambda b, t: (b, t, 0)),
            pl.BlockSpec((None, n_h, d), lambda b, t: (b, 0, 0)),
            pl.BlockSpec((None, n_buf, d), lambda b, t: (b, 0, 0)),
        ],
        out_shape=[
            jax.ShapeDtypeStruct((bsz, t_len, d), F32),
            jax.ShapeDtypeStruct((bsz, n_h, d), F32),
            jax.ShapeDtypeStruct((bsz, n_buf, d), F32),
        ],
        scratch_shapes=[pltpu.VMEM((taps - 1, d), F32), pltpu.VMEM((1, d), F32)],
        compiler_params=pltpu.CompilerParams(
            dimension_semantics=("arbitrary", "arbitrary"), vmem_limit_bytes=VMEM_LIMIT),
        name="lru_layer",
    )(x, h0, buf, w_in, conv_w, conv_b, w_a, b_a, w_x, b_x, lam, w_out, ln_g, ln_b)


D_ATT = N_HEADS * HEAD_DIM
N_ROWCOLS = 4 * N_KV * HEAD_DIM
N_WINCOLS = 2 * N_KV * HEAD_DIM
N_GATES = 3 * N_HEADS


def _nsa_proj_kernel(x_ref, wqkv_ref, wz_ref, wg_ref, q_ref, rows_ref, win_ref, sz_ref, gates_ref, *t_refs):
    xb = x_ref[...].astype(BF16)
    c0 = 0
    q_ref[...] = (_dot(xb, wqkv_ref[:, c0:c0 + D_ATT]) * (HEAD_DIM ** -0.5 * LOG2E)).astype(BF16)
    c0 += D_ATT
    rows = _dot(xb, wqkv_ref[:, c0:c0 + N_ROWCOLS])
    rows_ref[...] = rows
    c0 += N_ROWCOLS
    win = _dot(xb, wqkv_ref[:, c0:c0 + N_WINCOLS])
    win_ref[...] = win
    sz_ref[...] = _silu(_dot(xb, wz_ref[...])).astype(sz_ref.dtype)
    gates_ref[...] = jax.nn.sigmoid(_dot(xb, wg_ref[...]))
    if t_refs:
        rows_t_ref, win_t_ref = t_refs
        rows_t_ref[...] = rows.T
        win_t_ref[...] = win.T


def _nsa_proj(x2, weights, *, row_tile, seq_len=None):
    n, d = x2.shape
    widths = (D_ATT, N_ROWCOLS, N_WINCOLS, D_ATT, LANES)
    dtypes = (BF16, F32, F32, BF16, F32)
    out_specs = [pl.BlockSpec((row_tile, w), lambda i: (i, 0)) for w in widths]
    out_shape = [jax.ShapeDtypeStruct((n, w), dt) for w, dt in zip(widths, dtypes)]
    if seq_len is not None:
        tiles = seq_len // row_tile
        for w in (N_ROWCOLS, N_WINCOLS):
            out_specs.append(pl.BlockSpec((None, w, row_tile), lambda i: (i // tiles, 0, i % tiles)))
            out_shape.append(jax.ShapeDtypeStruct((n // seq_len, w, seq_len), F32))
    return pl.pallas_call(
        _nsa_proj_kernel,
        grid=(n // row_tile,),
        in_specs=[pl.BlockSpec((row_tile, d), lambda i: (i, 0))]
        + [pl.BlockSpec(w.shape, lambda i: (0, 0)) for w in weights],
        out_specs=out_specs,
        out_shape=out_shape,
        compiler_params=pltpu.CompilerParams(
            dimension_semantics=("arbitrary",), vmem_limit_bytes=VMEM_LIMIT),
        name="nsa_proj",
    )(x2, *weights)


def _nsa_out_kernel(x_ref, o_ref, sz_ref, w_ref, g_ref, b_ref, y_ref):
    m = (o_ref[...].astype(F32) * sz_ref[...].astype(F32)).astype(BF16)
    y_ref[...] = _ln_residual(x_ref[...], _dot(m, w_ref[...]), g_ref[...], b_ref[...])


def _nsa_out(x2, o2, sz2, w_out, ln_g, ln_b, *, row_tile):
    n, d = x2.shape
    row = lambda i: (i, 0)
    const = lambda i: (0, 0)
    return pl.pallas_call(
        _nsa_out_kernel,
        grid=(n // row_tile,),
        in_specs=[pl.BlockSpec((row_tile, d), row), pl.BlockSpec((row_tile, d), row),
                  pl.BlockSpec((row_tile, d), row), pl.BlockSpec((d, d), const),
                  pl.BlockSpec((1, d), const), pl.BlockSpec((1, d), const)],
        out_specs=pl.BlockSpec((row_tile, d), row),
        out_shape=jax.ShapeDtypeStruct((n, d), F32),
        compiler_params=pltpu.CompilerParams(
            dimension_semantics=("arbitrary",), vmem_limit_bytes=VMEM_LIMIT),
        name="nsa_out",
    )(x2, o2, sz2, w_out, ln_g, ln_b)


def _dup_half(slab, g, lo):
    rolled = pltpu.roll(slab, HEAD_DIM, 1)
    return jnp.where(lo, slab, rolled) if g == 0 else jnp.where(lo, rolled, slab)


def _select_blocks(score, nl, real, n_slc, n_sel, axis, stride):
    rank = jnp.zeros(score.shape, jnp.int32)
    for i in range(n_slc):
        j = stride * i
        cand = score[:, j:j + 1] if axis == 1 else score[j:j + 1, :]
        beats = (cand > score) | ((cand == score) & (nl > j))
        rank = rank + beats.astype(jnp.int32)
    return real & (rank < n_sel)


def _block_scores(imp2, nl, tq, n_slc, stride):
    blk = nl >> (stride - 1)
    real = ((nl & (stride - 1)) == 0) & (blk < n_slc)
    cur = tq >> 6
    forced = (blk == 0) | ((blk <= cur) & (blk > cur - N_LOCAL))
    score = jnp.where(forced, FORCE, jnp.where(blk <= cur, imp2, NEG_INF))
    return jnp.where(real, score, -jnp.inf), real


def _alibi_query_cols(head, n_rows):
    lane = lax.broadcasted_iota(jnp.int32, (1, LANES), 1)
    rem = jnp.full((1, LANES), _slope(head) * LOG2E, F32)
    row = jnp.zeros((1, LANES), F32)
    for i in range(3):
        piece = rem.astype(BF16).astype(F32)
        rem = rem - piece
        row = jnp.where(lane == i, 16.0 * piece, jnp.where(lane == 3 + i, piece, row))
    return jnp.broadcast_to(row, (n_rows, LANES)).astype(BF16)


def _alibi_key_cols(n_rows, period):
    r = lax.broadcasted_iota(jnp.int32, (n_rows, LANES), 0) & (period - 1)
    lane = lax.broadcasted_iota(jnp.int32, (n_rows, LANES), 1)
    return jnp.where(lane < 3, (r >> 4).astype(F32), jnp.where(lane < 6, (r & 15).astype(F32), 0.0))


def _nsa_prompt_kernel(q_ref, rows_ref, win_ref, gates_ref, wc_ref, o_ref,
                       ka, vta, kwa, vwta, kcd, vct_lo, vct_hi, qa_ref, selb_ref, acc_ref):
    blk = q_ref.shape[0]
    t_len = rows_ref.shape[0]
    n_cmp = t_len // CMP_BLOCK
    n_slc = -(-t_len // SLC_BLOCK)
    n_sel = min(TOP_N, n_slc)
    n_blk = t_len // blk
    slc_per_blk = blk // SLC_BLOCK
    n_wblk = WINDOW // blk + 1
    v_rows = vta.shape[2]
    qi = pl.program_id(1)
    t0 = qi * blk
    lo = lax.broadcasted_iota(jnp.int32, (1, LANES), 1) < HEAD_DIM

    @pl.when(qi == 0)
    def _build():
        lo_t = lax.broadcasted_iota(jnp.int32, (t_len, LANES), 1) < HEAD_DIM
        lo_c = lax.broadcasted_iota(jnp.int32, (LANES, LANES), 1) < HEAD_DIM
        lo_r = lax.broadcasted_iota(jnp.int32, (LANES, 1), 0) < HEAD_DIM
        vrow = lax.broadcasted_iota(jnp.int32, (v_rows, 1), 0)
        w2 = jnp.concatenate([wc_ref[...]] * 2, axis=0)
        cw = rows_ref[:, 0:2 * LANES].reshape(n_slc, SLC_BLOCK, 2 * LANES) * w2[None]
        kvc = jnp.concatenate([jnp.sum(cw[:, 0:CMP_BLOCK, :], axis=1), jnp.sum(cw[:, CMP_BLOCK:, :], axis=1),
                               jnp.zeros((LANES - n_cmp, 2 * LANES), F32)], axis=0)
        key_cols = _alibi_key_cols(t_len, blk).astype(BF16)

        def values_t(v):
            vt = v.T[0:v_rows, :]
            return jnp.where(vrow < HEAD_DIM, vt, jnp.where(vrow == HEAD_DIM, 1.0, 0.0)).astype(BF16)

        for g in range(N_KV):
            kd = _dup_half(rows_ref[:, 2 * LANES:3 * LANES], g, lo_t).astype(BF16)
            ka[g] = jnp.concatenate([kd, key_cols], axis=1)
            vt = values_t(_dup_half(rows_ref[:, 3 * LANES:4 * LANES], g, lo_t))
            kd = _dup_half(win_ref[:, 0:LANES], g, lo_t).astype(BF16)
            kwa[g] = jnp.concatenate([kd, key_cols], axis=1)
            vwt = values_t(_dup_half(win_ref[:, LANES:2 * LANES], g, lo_t))
            for c in range(n_blk):
                vta[g, c] = vt[:, c * blk:(c + 1) * blk]
                vwta[g, c] = vwt[:, c * blk:(c + 1) * blk]
            kcd[g] = _dup_half(kvc[:, 0:LANES], g, lo_c).astype(BF16)
            vct = _dup_half(kvc[:, LANES:2 * LANES], g, lo_c).T
            vct_lo[g] = jnp.where(lo_r, vct, 0.0).astype(BF16)
            vct_hi[g] = jnp.where(lo_r, 0.0, vct).astype(BF16)

    tq = t0 + lax.broadcasted_iota(jnp.int32, (1, blk), 1)
    row_c = lax.broadcasted_iota(jnp.int32, (LANES, 1), 0)
    cmp_blk = jnp.where(row_c < n_slc, 2 * row_c, 2 * (row_c - n_slc) + 1)
    cmp_end = cmp_blk * CMP_BLOCK + (CMP_BLOCK - 1)
    valid_c = (cmp_end <= tq) & (row_c < n_cmp)
    dist_c = (tq - cmp_end).astype(F32)
    slc_r = lax.broadcasted_iota(jnp.int32, (n_slc, 1), 0)

    lr = lax.broadcasted_iota(jnp.int32, (1, blk), 1) - lax.broadcasted_iota(jnp.int32, (blk, 1), 0)
    causal_bias = jnp.where(lr >= 0, 0.0, NEG_INF)

    w_start = pl.multiple_of(jnp.maximum(t0 - WINDOW, 0), blk)
    w_blk = w_start // blk
    w_shift, w_bias = [], []
    for i in range(n_wblk):
        shift = t0 - w_start - i * blk
        dist = lr + shift
        w_shift.append(shift.astype(F32))
        w_bias.append(jnp.where((dist >= 0) & (dist < WINDOW), 0.0, NEG_INF))

    gates_t = gates_ref[...].T

    for hh in range(N_HEADS):
        q2 = q_ref[:, (hh // 2) * LANES:(hh // 2 + 1) * LANES]
        zero = jnp.zeros_like(q2)
        qm = jnp.where(lo, q2, zero) if hh % 2 == 0 else jnp.where(lo, zero, q2)
        qa_ref[hh] = jnp.concatenate([qm, _alibi_query_cols(hh, blk)], axis=1)

    o_cmp = []
    for g in range(N_KV):
        heads = range(g * HPG, (g + 1) * HPG)
        scores = [_dot_nt(kcd[g], qa_ref[hh, :, 0:LANES]) for hh in heads]
        imp = jnp.zeros((LANES, blk), F32)
        probs = []
        for hh, s in zip(heads, scores):
            s = jnp.where(valid_c, s - (_slope(hh) * LOG2E) * dist_c, NEG_INF)
            e = jnp.exp2(s - jnp.max(s, axis=0, keepdims=True))
            p = jnp.where(valid_c, e, 0.0) / jnp.sum(e, axis=0, keepdims=True)
            imp = imp + p
            probs.append(p.astype(BF16))
        for j in range(HPG // 2):
            o_cmp.append(_dot(vct_lo[g], probs[2 * j]) + _dot(vct_hi[g], probs[2 * j + 1]))

        imp2 = imp[0:n_slc, :] + imp[n_slc:2 * n_slc, :]
        score, real = _block_scores(imp2, slc_r, tq, n_slc, stride=1)
        sel = _select_blocks(score, slc_r, real, n_slc, n_sel, axis=0, stride=1)
        sel_bias = jnp.where(sel, 0.0, NEG_INF)
        for c in range(n_blk):
            rows = sel_bias[c * slc_per_blk:(c + 1) * slc_per_blk, :]
            selb_ref[g, c] = jnp.concatenate([rows] * (SUBLANES // slc_per_blk), axis=0)

    acc_ref[...] = jnp.zeros(acc_ref.shape, F32)

    def key_block(c, ms, extra_bias):
        off = pl.multiple_of(c * blk, blk)
        shift = (t0 - off).astype(F32)
        keys, biases = [], []
        for g in range(N_KV):
            keys.append(ka[g, pl.ds(off, blk), :])
            sel_rows = selb_ref[g, c]
            bias = jnp.concatenate([jnp.broadcast_to(sel_rows[m:m + 1, :], (SLC_BLOCK, blk))
                                    for m in range(slc_per_blk)], axis=0)
            biases.append(bias if extra_bias is None else bias + extra_bias)
        scores, new_ms, probs = [], [], []
        for hh in range(N_HEADS):
            s = _dot_nt(keys[hh // HPG], qa_ref[hh]) + biases[hh // HPG]
            c_h = (_slope(hh) * LOG2E) * shift
            new_ms.append(jnp.maximum(ms[hh], jnp.max(s, axis=0, keepdims=True) - c_h))
            scores.append(s)
        for hh in range(N_HEADS):
            c_h = (_slope(hh) * LOG2E) * shift
            probs.append(jnp.exp2((scores[hh] - (new_ms[hh] + c_h)).astype(BF16)))
        for hh in range(N_HEADS):
            acc_ref[hh] = acc_ref[hh] * jnp.exp2(ms[hh] - new_ms[hh]) + _dot(vta[hh // HPG, c], probs[hh])
        return tuple(new_ms)

    ms = lax.fori_loop(0, qi, lambda c, ms: key_block(c, ms, None),
                       (jnp.full((1, blk), NEG_INF, F32),) * N_HEADS)
    key_block(qi, ms, causal_bias)

    for g in range(N_KV):
        k_win = [kwa[g, pl.ds(pl.multiple_of(w_start + i * blk, blk), blk), :] for i in range(n_wblk)]
        w_scores, w_max = [], []
        for hh in range(g * HPG, (g + 1) * HPG):
            slope = _slope(hh) * LOG2E
            tiles, m_w = [], None
            for i in range(n_wblk):
                s = _dot_nt(k_win[i], qa_ref[hh]) + w_bias[i]
                m_i = jnp.max(s, axis=0, keepdims=True) - slope * w_shift[i]
                m_w = m_i if m_w is None else jnp.maximum(m_w, m_i)
                tiles.append(s)
            w_scores.append(tiles)
            w_max.append(m_w)
        o_wins = []
        for h in range(HPG):
            slope = _slope(g * HPG + h) * LOG2E
            acc_w = jnp.zeros((v_rows, blk), F32)
            for i in range(n_wblk):
                p = jnp.exp2((w_scores[h][i] - (w_max[h] + slope * w_shift[i])).astype(BF16))
                acc_w = acc_w + _dot(vwta[g, w_blk + i], p)
            o_wins.append(acc_w[0:HEAD_DIM, :] / acc_w[HEAD_DIM:HEAD_DIM + 1, :])

        for j in range(HPG // 2):
            outs = []
            for hh in (g * HPG + 2 * j, g * HPG + 2 * j + 1):
                o_win = o_wins[hh - g * HPG]
                acc = acc_ref[hh]
                o_sel = acc[0:HEAD_DIM, :] / acc[HEAD_DIM:HEAD_DIM + 1, :]
                half = (hh % 2) * HEAD_DIM
                gate = lambda c: gates_t[c * N_HEADS + hh:c * N_HEADS + hh + 1, :]
                o_c = o_cmp[g * (HPG // 2) + j][half:half + HEAD_DIM, :]
                outs.append(gate(0) * o_c + gate(1) * o_sel + gate(2) * o_win)
            c0 = g * HPG * HEAD_DIM + j * LANES
            o_ref[:, c0:c0 + LANES] = jnp.concatenate(outs, axis=0).T.astype(o_ref.dtype)


def _block_expand_matrix(n_rows, n_keys):
    n = jnp.arange(n_rows, dtype=jnp.int32)[:, None]
    p = jnp.arange(n_keys, dtype=jnp.int32)[None, :]
    return (n == 2 * (p // SLC_BLOCK)).astype(BF16)


def _nsa_prompt_attention(q, rows, win, gates, wc, *, q_tile=256):
    bsz, t_len, _ = q.shape
    assert WINDOW % q_tile == 0 and q_tile % (2 * LANES) == 0 and t_len >= WINDOW + q_tile
    assert t_len // CMP_BLOCK <= LANES and t_len % SLC_BLOCK == 0 and SUBLANES % (q_tile // SLC_BLOCK) == 0
    v_rows = HEAD_DIM + 16
    k_scratch = pltpu.VMEM((N_KV, t_len, 2 * LANES), BF16)
    v_scratch = pltpu.VMEM((N_KV, t_len // q_tile, v_rows, q_tile), BF16)
    c_scratch = pltpu.VMEM((N_KV, LANES, LANES), BF16)
    return pl.pallas_call(
        _nsa_prompt_kernel,
        grid=(bsz, t_len // q_tile),
        in_specs=[
            pl.BlockSpec((None, q_tile, D_ATT), lambda b, i: (b, i, 0)),
            pl.BlockSpec((None, t_len, N_ROWCOLS), lambda b, i: (b, 0, 0)),
            pl.BlockSpec((None, t_len, N_WINCOLS), lambda b, i: (b, 0, 0)),
            pl.BlockSpec((None, q_tile, LANES), lambda b, i: (b, i, 0)),
            pl.BlockSpec((CMP_BLOCK, 2 * LANES), lambda b, i: (0, 0)),
        ],
        out_specs=pl.BlockSpec((None, q_tile, D_ATT), lambda b, i: (b, i, 0)),
        out_shape=jax.ShapeDtypeStruct((bsz, t_len, D_ATT), BF16),
        scratch_shapes=[k_scratch, v_scratch, k_scratch, v_scratch,
                        c_scratch, c_scratch, c_scratch,
                        pltpu.VMEM((N_HEADS, q_tile, 2 * LANES), BF16),
                        pltpu.VMEM((N_KV, t_len // q_tile, SUBLANES, q_tile), F32),
                        pltpu.VMEM((N_HEADS, v_rows, q_tile), F32)],
        compiler_params=pltpu.CompilerParams(
            dimension_semantics=("arbitrary", "arbitrary"), vmem_limit_bytes=VMEM_LIMIT),
        name="nsa_prompt_attention",
    )(q, rows, win, gates, wc)


def _nsa_sample_kernel(pt_ref, *refs, pages_per_step, n_steps, n_new):
    page_refs = refs[:pages_per_step]
    (qbd_ref, new_ref, winbuf_ref, gates_ref, wc_ref, e_ref, o_ref, s_ref, vst_ref, kcs, vcs) = refs[pages_per_step:]
    del pt_ref
    step = pl.program_id(1)
    keys_per_step = pages_per_step * PAGE_SIZE
    p_len = n_steps * keys_per_step
    n_keys = p_len + LANES
    n_cmp = p_len // CMP_BLOCK
    n_slc = -(-(p_len + n_new) // SLC_BLOCK)
    n_sel = min(TOP_N, n_slc)
    cmp_per_step = keys_per_step // CMP_BLOCK
    n_rows = N_KV * n_new * HPG
    gd = N_KV * HEAD_DIM

    def slab(r, kind):
        return r[kind].reshape(gd, r.shape[-1])

    cm = jnp.concatenate([jnp.concatenate([slab(r, 0).T, slab(r, 1).T], axis=1) for r in page_refs], axis=0)
    kvc = jnp.sum(cm.reshape(cmp_per_step, CMP_BLOCK, 2 * LANES) * wc_ref[...][None], axis=1)
    c_off = pl.multiple_of(step * cmp_per_step, cmp_per_step)
    kcs[pl.ds(c_off, cmp_per_step), :] = kvc[:, 0:LANES].astype(BF16)
    vcs[pl.ds(c_off, cmp_per_step), :] = kvc[:, LANES:2 * LANES].astype(BF16)
    s_ref[step] = _dot(qbd_ref[...], jnp.concatenate([slab(r, 2).astype(BF16) for r in page_refs], axis=1))
    vst_ref[step] = jnp.concatenate([slab(r, 3).astype(BF16) for r in page_refs], axis=1)

    @pl.when(step == n_steps - 1)
    def _attend():
        n_wb = winbuf_ref.shape[-1]
        w_keys = n_wb + LANES
        qbd = qbd_ref[...]
        r = lax.broadcasted_iota(jnp.int32, (n_rows, 1), 0)
        head = (r // (n_new * HPG)) * HPG + (r % HPG)
        slope = jnp.exp2((head + 1).astype(F32) * (-8.0 / N_HEADS)) * LOG2E
        tq = p_len + (r // HPG) % n_new

        def attend(s, values_t, dist, bias):
            s = s - slope * dist.astype(F32) + bias
            e = jnp.exp2(s - jnp.max(s, axis=-1, keepdims=True))
            eb = e.astype(BF16)
            o = sum(_dot_nt(eb[:, ks], vt) for ks, vt in values_t)
            return o / jnp.sum(e, axis=-1, keepdims=True)

        nl = lax.broadcasted_iota(jnp.int32, (n_rows, n_cmp), 1)
        cmp_end = nl * CMP_BLOCK + (CMP_BLOCK - 1)
        valid_c = cmp_end <= tq
        s = _dot_nt(qbd, kcs[...]) - slope * (tq - cmp_end).astype(F32)
        s = jnp.where(valid_c, s, NEG_INF)
        e = jnp.exp2(s - jnp.max(s, axis=-1, keepdims=True))
        p = jnp.where(valid_c, e, 0.0) / jnp.sum(e, axis=-1, keepdims=True)
        o_cmp = _dot(p.astype(BF16), vcs[...])

        n_gt = N_KV * n_new
        imp = jnp.sum(p.reshape(n_gt, HPG, n_cmp), axis=1)
        imp2 = imp + pltpu.roll(imp, n_cmp - 1, 1)
        w_sel = -(-2 * n_slc // LANES) * LANES
        imp2 = jnp.concatenate([imp2, jnp.zeros((n_gt, w_sel - n_cmp), F32)], axis=1)
        nl8 = lax.broadcasted_iota(jnp.int32, (n_gt, w_sel), 1)
        r8 = lax.broadcasted_iota(jnp.int32, (n_gt, 1), 0)
        score, real = _block_scores(imp2, nl8, p_len + r8 % n_new, n_slc, stride=2)
        sel = _select_blocks(score, nl8, real, n_slc, n_sel, axis=1, stride=2)
        selb = jnp.where(sel, 1.0, 0.0)
        selb = jnp.broadcast_to(selb[:, None, :], (n_gt, HPG, w_sel)).reshape(n_rows, w_sel).astype(BF16)
        keys_per_tile = e_ref.shape[1]
        hit = [_dot(selb[:, j * LANES:(j + 1) * LANES], e_ref[...]) for j in range(p_len // keys_per_tile)]
        hit.append(jnp.broadcast_to(selb[:, 2 * (p_len // SLC_BLOCK):2 * (p_len // SLC_BLOCK) + 1].astype(F32),
                                    (n_rows, LANES)))
        hit = jnp.concatenate(hit, axis=1)
        dist = tq - lax.broadcasted_iota(jnp.int32, (n_rows, n_keys), 1)
        new_k, new_v, new_kw, new_vw = (new_ref[i] for i in range(4))
        s_sel = jnp.concatenate([s_ref[i] for i in range(n_steps)] + [_dot(qbd, new_k)], axis=1)
        values_t = [(slice(i * keys_per_step, (i + 1) * keys_per_step), vst_ref[i]) for i in range(n_steps)]
        o_sel = attend(s_sel, values_t + [(slice(p_len, n_keys), new_v)], dist,
                       jnp.where((hit > 0.5) & (dist >= 0), 0.0, NEG_INF))

        dist_w = tq - (p_len - n_wb) - lax.broadcasted_iota(jnp.int32, (n_rows, w_keys), 1)
        s_win = jnp.concatenate([_dot(qbd, slab(winbuf_ref, 0).astype(BF16)), _dot(qbd, new_kw)], axis=1)
        o_win = attend(s_win, [(slice(0, n_wb), slab(winbuf_ref, 1).astype(BF16)), (slice(n_wb, w_keys), new_vw)],
                       dist_w, jnp.where((dist_w >= 0) & (dist_w < WINDOW), 0.0, NEG_INF))

        o_ref[...] = (gates_ref[:, 0:1] * o_cmp + gates_ref[:, 1:2] * o_sel + gates_ref[:, 2:3] * o_win)


def _nsa_sample_attention(page_table, cache_t, qbd, new_t, winbuf_t, gates, wc, *, n_new, pages_per_step=64):
    bsz, n_pages = page_table.shape
    n_steps = n_pages // pages_per_step
    keys_per_step = pages_per_step * PAGE_SIZE
    p_len = n_pages * PAGE_SIZE
    n_rows = qbd.shape[1]
    n_wb = winbuf_t.shape[-1]
    n_cmp = p_len // CMP_BLOCK
    gd = N_KV * HEAD_DIM
    keys_per_tile = (LANES // 2) * SLC_BLOCK
    assert p_len % keys_per_tile == 0 and n_new <= SLC_BLOCK
    e_mat = _block_expand_matrix(LANES, keys_per_tile)
    kern = functools.partial(_nsa_sample_kernel, pages_per_step=pages_per_step, n_steps=n_steps, n_new=n_new)

    def page_spec(k):
        return pl.BlockSpec((None, 4, N_KV, HEAD_DIM, PAGE_SIZE),
                            lambda b, s, pt: (pt[b, s * pages_per_step + k], 0, 0, 0, 0))

    per_b = lambda b, s, pt: (b, 0, 0)
    const = lambda b, s, pt: (0, 0)
    grid_spec = pltpu.PrefetchScalarGridSpec(
        num_scalar_prefetch=1,
        grid=(bsz, n_steps),
        in_specs=[page_spec(k) for k in range(pages_per_step)] + [
            pl.BlockSpec((None, n_rows, LANES), per_b),
            pl.BlockSpec((None, 4, gd, LANES), lambda b, s, pt: (b, 0, 0, 0)),
            pl.BlockSpec((None, 2, N_KV, HEAD_DIM, n_wb), lambda b, s, pt: (b, 0, 0, 0, 0)),
            pl.BlockSpec((None, n_rows, LANES), per_b),
            pl.BlockSpec((CMP_BLOCK, 2 * LANES), const),
            pl.BlockSpec((LANES, keys_per_tile), const),
        ],
        out_specs=pl.BlockSpec((None, n_rows, LANES), per_b),
        scratch_shapes=[
            pltpu.VMEM((n_steps, n_rows, keys_per_step), F32),
            pltpu.VMEM((n_steps, gd, keys_per_step), BF16),
            pltpu.VMEM((n_cmp, LANES), BF16), pltpu.VMEM((n_cmp, LANES), BF16),
        ],
    )
    return pl.pallas_call(
        kern,
        grid_spec=grid_spec,
        out_shape=jax.ShapeDtypeStruct((bsz, n_rows, LANES), F32),
        compiler_params=pltpu.CompilerParams(
            dimension_semantics=("arbitrary", "arbitrary"), vmem_limit_bytes=VMEM_LIMIT),
        name="nsa_sample_attention",
    )(page_table, *([cache_t] * pages_per_step), qbd, new_t, winbuf_t, gates, wc, e_mat)


def _nsa_weights(w_in):
    c_g = D_ATT + N_ROWCOLS + N_WINCOLS
    c_z = c_g + N_GATES
    w_gates = jnp.pad(w_in[:, c_g:c_z], ((0, 0), (0, LANES - N_GATES)))
    return w_in[:, :c_g].astype(BF16), w_in[:, c_z:].astype(BF16), w_gates.astype(BF16)


def _cmp_weight_tile(w_cmp):
    return w_cmp.transpose(1, 0, 2, 3).reshape(CMP_BLOCK, 2 * N_KV * HEAD_DIM)


def _nsa_layer_prompt(x, w_all, wc, w_out, ln_g, ln_b):
    bsz, t_len, d = x.shape
    x2 = x.reshape(bsz * t_len, d)
    q, rows, win, sz, gates, rows_t, win_t = _nsa_proj(x2, w_all, row_tile=512, seq_len=t_len)
    o = _nsa_prompt_attention(q.reshape(bsz, t_len, D_ATT), rows.reshape(bsz, t_len, N_ROWCOLS),
                              win.reshape(bsz, t_len, N_WINCOLS), gates.reshape(bsz, t_len, LANES), wc)
    y = _nsa_out(x2, o.reshape(bsz * t_len, D_ATT), sz, w_out, ln_g, ln_b, row_tile=512)
    keep = min(WINDOW, t_len)
    rows_out = rows_t.reshape(bsz, 4, N_KV, HEAD_DIM, t_len).transpose(0, 4, 1, 2, 3)
    win_out = win_t[:, :, t_len - keep:].reshape(bsz, 2, N_KV, HEAD_DIM, keep).transpose(0, 4, 1, 2, 3)
    return y.reshape(bsz, t_len, d), rows_out, win_out


def _nsa_layer_sample(x, cache, win_buf, page_table, w_all, wc, w_out, ln_g, ln_b):
    bsz, n_new, d = x.shape
    x2 = x.reshape(bsz * n_new, d)
    q, rows, win, sz, gates = _nsa_proj(x2, w_all, row_tile=bsz * n_new)
    q5 = q.reshape(bsz, n_new, N_KV, HPG, HEAD_DIM).transpose(0, 2, 1, 3, 4)
    eye = jnp.eye(N_KV, dtype=q.dtype)
    qbd = (q5[:, :, :, :, None, :] * eye[None, :, None, None, :, None]).reshape(
        bsz, N_KV * n_new * HPG, N_KV * HEAD_DIM)
    g5 = gates[:, :N_GATES].reshape(bsz, n_new, 3, N_KV, HPG).transpose(0, 3, 1, 4, 2)
    g_rows = jnp.pad(g5.reshape(bsz, N_KV * n_new * HPG, 3), ((0, 0), (0, 0), (0, LANES - 3)))
    rows3 = rows.reshape(bsz, n_new, N_ROWCOLS)
    win3 = win.reshape(bsz, n_new, N_WINCOLS)
    n_wb = win_buf.shape[1]
    gd = N_KV * HEAD_DIM
    new_t = jnp.stack([rows3[:, :, 2 * gd:3 * gd], rows3[:, :, 3 * gd:4 * gd],
                       win3[:, :, 0:gd], win3[:, :, gd:2 * gd]], axis=1).transpose(0, 1, 3, 2)
    new_t = jnp.pad(new_t, ((0, 0), (0, 0), (0, 0), (0, LANES - n_new))).astype(BF16)
    o_rows = _nsa_sample_attention(
        page_table, cache.transpose(0, 2, 3, 4, 1), qbd, new_t, win_buf.transpose(0, 2, 3, 4, 1),
        g_rows, wc, n_new=n_new)
    o6 = o_rows.reshape(bsz, N_KV, n_new, HPG, N_KV, HEAD_DIM)
    o = jnp.stack([o6[:, g, :, :, g, :] for g in range(N_KV)], axis=2).reshape(bsz * n_new, D_ATT)
    y = _nsa_out(x2, o, sz, w_out, ln_g, ln_b, row_tile=bsz * n_new)
    all_win = jnp.concatenate([win_buf, win3.reshape(bsz, n_new, 2, N_KV, HEAD_DIM)], axis=1)
    return (y.reshape(bsz, n_new, d), rows3.reshape(bsz, n_new, 4, N_KV, HEAD_DIM),
            all_win[:, all_win.shape[1] - n_wb:])


def _time_major(a):
    bsz, k, d = a.shape
    return a.transpose(1, 0, 2).reshape(1, k * bsz, d)


def _batch_major(a, bsz):
    _, kb, d = a.shape
    return a.reshape(kb // bsz, bsz, d).transpose(1, 0, 2)


def kernel(x_prompt, x_sample, cache_nsa_kv, cache_nsa_win, state_conv_a, state_lru_h, state_lru_conv, page_table,
           ln_g, ln_b, a_w_in, a_conv_w, a_w_out, b_w_in, b_w_cmp, b_w_out,
           c_w_in, c_conv_w, c_conv_b, c_w_a, c_b_a, c_w_x, c_b_x, c_lam, c_w_out):
    bp, t_len, d = x_prompt.shape
    bs, n_new, _ = x_sample.shape
    yp = x_prompt
    ys = _time_major(x_sample)
    conv_a_p, conv_a_s = [], []
    rows_p, rows_s, win_p, win_s = [], [], [], []
    h_p, h_s, cc_p, cc_s = [], [], [], []
    row2 = lambda v: v.reshape(1, -1)
    a_w_in_b, a_w_out_b = a_w_in.astype(BF16), a_w_out.astype(BF16)
    for i in range(DEPTH):
        kind, j = i % 3, i // 3
        g, b = row2(ln_g[i]), row2(ln_b[i])
        if kind == 0:
            taps = a_conv_w.shape[1]
            yp, sp = _conv_layer(yp, jnp.zeros((bp, taps - 1, d), F32), a_w_in_b, a_conv_w, a_w_out_b, g, b,
                                 layer=j, stride=1, row_tile=512)
            ys, ss = _conv_layer(ys, _time_major(state_conv_a[j]), a_w_in_b, a_conv_w, a_w_out_b, g, b,
                                 layer=j, stride=bs, row_tile=n_new * bs)
            conv_a_p.append(sp)
            conv_a_s.append(_batch_major(ss, bs))
        elif kind == 1:
            w_all = _nsa_weights(b_w_in[j])
            wc = _cmp_weight_tile(b_w_cmp[j])
            w_out = b_w_out[j].astype(BF16)
            yp, rp, wp = _nsa_layer_prompt(yp, w_all, wc, w_out, g, b)
            ys_b, rs, ws = _nsa_layer_sample(_batch_major(ys, bs), cache_nsa_kv[j], cache_nsa_win[j], page_table,
                                             w_all, wc, w_out, g, b)
            ys = _time_major(ys_b)
            rows_p.append(rp)
            rows_s.append(rs)
            win_p.append(wp)
            win_s.append(ws)
        else:
            w_in, w_out = c_w_in[j].astype(BF16), c_w_out[j].astype(BF16)
            w_a, w_x = c_w_a[j].astype(BF16), c_w_x[j].astype(BF16)
            taps = c_conv_w.shape[1]
            args = (w_in, c_conv_w[j], row2(c_conv_b[j]), w_a, row2(c_b_a[j]), w_x, row2(c_b_x[j]),
                    row2(c_lam[j]), w_out, g, b)
            yp, hp, cp = _lru_layer(yp, jnp.zeros((bp, 1, d), F32), jnp.zeros((bp, taps - 1, d), F32), *args,
                                    stride=1, row_tile=512)
            ys, hs, cs = _lru_layer(ys, state_lru_h[j].reshape(1, bs, d), _time_major(state_lru_conv[j]), *args,
                                    stride=bs, row_tile=n_new * bs)
            h_p.append(hp.reshape(bp, d))
            h_s.append(hs.reshape(bs, d))
            cc_p.append(cp)
            cc_s.append(_batch_major(cs, bs))
    return (yp, _batch_major(ys, bs), jnp.stack(conv_a_p), jnp.stack(conv_a_s), jnp.stack(rows_p), jnp.stack(rows_s),
            jnp.stack(win_p), jnp.stack(win_s), jnp.stack(h_p), jnp.stack(h_s), jnp.stack(cc_p), jnp.stack(cc_s))
```

```python
import functools

import jax
import jax.numpy as jnp
from jax import lax
from jax.experimental import pallas as pl
from jax.experimental.pallas import tpu as pltpu

F32 = jnp.float32
BF16 = jnp.bfloat16

DEPTH = 4
N_HEADS = 16
N_KV = 2
HPG = N_HEADS // N_KV
HEAD_DIM = 64
CMP_BLOCK = 32
SLC_BLOCK = 64
TOP_N = 16
N_LOCAL = 2
WINDOW = 512
PAGE_SIZE = 128
NEG_INF = -1e30
FORCE = 1e9
RG_C = 8.0
N_RG_BLOCKS = 4
ALPHA = (2 * DEPTH) ** 0.25
LN_EPS = 1e-5
LOG2E = 1.4426950408889634

LANES = 128
SUBLANES = 8
BF16_SUBLANES = 16
VMEM_LIMIT = 56 * 1024 * 1024

ROW_TILE = 512
NSA_BLOCK = 256
PAGES_PER_STEP = 64
ALIBI_RADIX = 16


def _dot(a, b):
    return jnp.dot(a, b, preferred_element_type=F32)


def _dot_nt(a, b):
    return lax.dot_general(a, b, (((1,), (1,)), ((), ())), preferred_element_type=F32)


def _silu(z):
    return z * jax.nn.sigmoid(z)


def _ln_residual(x, y, g, b):
    r = ALPHA * x + y
    mu = jnp.mean(r, axis=-1, keepdims=True)
    c = r - mu
    var = jnp.mean(c * c, axis=-1, keepdims=True)
    return c * lax.rsqrt(var + LN_EPS) * g + b


def _slope(head):
    return 2.0 ** (-8.0 * (head + 1) / N_HEADS)


def _prev_rows(u, carry, k):
    n_carry = carry.shape[0]
    rolled = pltpu.roll(u, k, 0)
    row = lax.broadcasted_iota(jnp.int32, (SUBLANES, 1), 0)
    head = rolled[0:SUBLANES, :]
    for i in range(k):
        head = jnp.where(row == i, carry[n_carry - k + i:n_carry - k + i + 1, :], head)
    return jnp.concatenate([head, rolled[SUBLANES:, :]], axis=0)


def _conv_layer_kernel(x_ref, buf_ref, win_ref, cw_ref, wout_ref, g_ref, b_ref,
                       y_ref, nbuf_ref, carry_ref, *, stride, col_chunk):
    rows, d = x_ref.shape
    taps = cw_ref.shape[0]
    x = x_ref[...]
    xb = x.astype(BF16)
    if stride == 1:
        @pl.when(pl.program_id(1) == 0)
        def _():
            carry_ref[...] = buf_ref[...]
    acc = jnp.zeros((rows, d), F32)
    for j in range(d // col_chunk):
        lo = j * col_chunk
        sl = slice(lo, lo + col_chunk)
        h, bg, cg, z = [_dot(xb, win_ref[:, part * d + lo:part * d + lo + col_chunk]) for part in range(4)]
        u = cg * h
        conv = u * cw_ref[taps - 1:taps, sl]
        if stride == 1:
            carry = carry_ref[:, sl]
            for k in range(1, taps):
                conv = conv + _prev_rows(u, carry, k) * cw_ref[taps - 1 - k:taps - k, sl]
            new_buf = u[rows - (taps - 1):rows, :]
            carry_ref[:, sl] = new_buf
        else:
            ext = jnp.concatenate([buf_ref[:, sl], u], axis=0)
            for k in range(1, taps):
                s0 = (taps - 1 - k) * stride
                conv = conv + ext[s0:s0 + rows, :] * cw_ref[taps - 1 - k:taps - k, sl]
            new_buf = ext[rows:rows + (taps - 1) * stride, :]
        nbuf_ref[:, sl] = new_buf
        m = _silu(z) * bg * conv
        acc = acc + _dot(m.astype(BF16), wout_ref[sl, :])
    y_ref[...] = _ln_residual(x, acc, g_ref[...], b_ref[...])


def _conv_layer(x, buf, w_in, conv_w, w_out, ln_g, ln_b, *, layer, stride, row_tile):
    bsz, t_len, d = x.shape
    n_buf = buf.shape[1]
    taps = conv_w.shape[1]
    grid = (bsz, t_len // row_tile)
    kern = functools.partial(_conv_layer_kernel, stride=stride, col_chunk=512)
    return pl.pallas_call(
        kern,
        grid=grid,
        in_specs=[
            pl.BlockSpec((None, row_tile, d), lambda b, t: (b, t, 0)),
            pl.BlockSpec((None, n_buf, d), lambda b, t: (b, 0, 0)),
            pl.BlockSpec((None, d, 4 * d), lambda b, t: (layer, 0, 0)),
            pl.BlockSpec((None, taps, d), lambda b, t: (layer, 0, 0)),
            pl.BlockSpec((None, d, d), lambda b, t: (layer, 0, 0)),
            pl.BlockSpec((1, d), lambda b, t: (0, 0)),
            pl.BlockSpec((1, d), lambda b, t: (0, 0)),
        ],
        out_specs=[
            pl.BlockSpec((None, row_tile, d), lambda b, t: (b, t, 0)),
            pl.BlockSpec((None, n_buf, d), lambda b, t: (b, 0, 0)),
        ],
        out_shape=[
            jax.ShapeDtypeStruct((bsz, t_len, d), F32),
            jax.ShapeDtypeStruct((bsz, n_buf, d), F32),
        ],
        scratch_shapes=[pltpu.VMEM((taps - 1, d), F32)],
        compiler_params=pltpu.CompilerParams(
            dimension_semantics=("arbitrary", "arbitrary"), vmem_limit_bytes=VMEM_LIMIT),
        name="conv_layer",
    )(x, buf, w_in, conv_w, w_out, ln_g, ln_b)


def _scan_rows(a_list, b_list, h_prev_list):
    rows, cols = a_list[0].shape
    groups = rows // SUBLANES
    a3 = [a.reshape(groups, SUBLANES, cols) for a in a_list]
    b3 = [b.reshape(groups, SUBLANES, cols) for b in b_list]
    sub = lax.broadcasted_iota(jnp.int32, (1, SUBLANES, 1), 1)
    s = 1
    while s < SUBLANES:
        live = sub >= s
        for n, (a, b) in enumerate(zip(a3, b3)):
            a_sh = jnp.where(live, pltpu.roll(a, s, 1), 1.0)
            b_sh = jnp.where(live, pltpu.roll(b, s, 1), 0.0)
            b3[n] = a * b_sh + b
            a3[n] = a * a_sh
        s *= 2
    carries = list(h_prev_list)
    outs = [[] for _ in a3]
    for j in range(groups):
        for n in range(len(a3)):
            h = a3[n][j] * carries[n] + b3[n][j]
            carries[n] = h[SUBLANES - 1:SUBLANES, :]
            outs[n].append(h)
    return [jnp.concatenate(o, axis=0) for o in outs]


def _lru_layer_kernel(x_ref, h0_ref, buf_ref, win_ref, cw_ref, cb_ref, wa_ref, ba_ref, wx_ref, bx_ref,
                      lam_ref, wout_ref, g_ref, b_ref,
                      y_ref, hlast_ref, nbuf_ref, carry_ref, hcarry_ref, *, stride, n_sub):
    d = x_ref.shape[1]
    rows = x_ref.shape[0] // n_sub
    taps = cw_ref.shape[0]
    n_blocks, blk, _ = wa_ref.shape
    slices = [slice(n * blk, (n + 1) * blk) for n in range(n_blocks)]
    if stride == 1:
        @pl.when(pl.program_id(1) == 0)
        def _():
            carry_ref[...] = buf_ref[...]
            hcarry_ref[...] = h0_ref[...]

    def conv_stage(s):
        x = x_ref[s * rows:(s + 1) * rows, :]
        xb = x.astype(BF16)
        ucs = []
        for sl in slices:
            u = _dot(xb, win_ref[:, sl])
            uc = u * cw_ref[taps - 1:taps, sl] + cb_ref[:, sl]
            if stride == 1:
                carry = carry_ref[:, sl]
                for k in range(1, taps):
                    uc = uc + _prev_rows(u, carry, k) * cw_ref[taps - 1 - k:taps - k, sl]
                new_buf = u[rows - (taps - 1):rows, :]
                carry_ref[:, sl] = new_buf
            else:
                ext = jnp.concatenate([buf_ref[:, sl], u], axis=0)
                for k in range(1, taps):
                    s0 = (taps - 1 - k) * stride
                    uc = uc + ext[s0:s0 + rows, :] * cw_ref[taps - 1 - k:taps - k, sl]
                new_buf = ext[rows:rows + (taps - 1) * stride, :]
            nbuf_ref[:, sl] = new_buf
            ucs.append(uc)
        return x, xb, ucs

    def recurrence_stage(ucs):
        a_list, b_list = [], []
        for n, sl in enumerate(slices):
            ucb = ucs[n].astype(BF16)
            r = jax.nn.sigmoid(_dot(ucb, wa_ref[n]) + ba_ref[:, sl])
            i = jax.nn.sigmoid(_dot(ucb, wx_ref[n]) + bx_ref[:, sl])
            nl = -lam_ref[:, sl]
            softplus = jnp.maximum(nl, 0.0) + jnp.log1p(jnp.exp(-jnp.abs(nl)))
            log_a = (-RG_C * softplus) * r
            a = jnp.exp(log_a)
            y = -jnp.tanh(log_a) * (a * a + 1.0)
            a_list.append(a)
            b_list.append(jnp.exp2(0.5 * jnp.log2(y)) * (i * ucs[n]))
        if stride == 1:
            hs = _scan_rows(a_list, b_list, [hcarry_ref[:, sl] for sl in slices])
            for n, sl in enumerate(slices):
                hcarry_ref[:, sl] = hs[n][rows - 1:rows, :]
                hlast_ref[:, sl] = hs[n][rows - 1:rows, :]
            return hs
        hs = []
        for n, sl in enumerate(slices):
            h_t = h0_ref[:, sl]
            steps = []
            for t in range(rows // stride):
                rs = slice(t * stride, (t + 1) * stride)
                h_t = a_list[n][rs, :] * h_t + b_list[n][rs, :]
                steps.append(h_t)
            hs.append(jnp.concatenate(steps, axis=0))
            hlast_ref[:, sl] = h_t
        return hs

    def output_stage(s, x, xb, hs):
        acc = jnp.zeros((rows, d), F32)
        for n, sl in enumerate(slices):
            z = _dot(xb, win_ref[:, d + n * blk:d + (n + 1) * blk])
            acc = acc + _dot((_silu(z) * hs[n]).astype(BF16), wout_ref[sl, :])
        y_ref[s * rows:(s + 1) * rows, :] = _ln_residual(x, acc, g_ref[...], b_ref[...])

    staged = conv_stage(0)
    for s in range(n_sub):
        x, xb, ucs = staged
        if s + 1 < n_sub:
            staged = conv_stage(s + 1)
        output_stage(s, x, xb, recurrence_stage(ucs))


def _lru_layer(x, h0, buf, w_in, conv_w, conv_b, w_a, b_a, w_x, b_x, lam, w_out, ln_g, ln_b, *, stride, row_tile):
    bsz, t_len, d = x.shape
    n_buf = buf.shape[1]
    n_h = h0.shape[1]
    taps = conv_w.shape[0]
    nb, blk, _ = w_a.shape
    grid = (bsz, t_len // row_tile)
    const2 = lambda b, t: (0, 0)
    n_sub = 2 if stride == 1 and row_tile % (2 * SUBLANES) == 0 else 1
    kern = functools.partial(_lru_layer_kernel, stride=stride, n_sub=n_sub)
    return pl.pallas_call(
        kern,
        grid=grid,
        in_specs=[
            pl.BlockSpec((None, row_tile, d), lambda b, t: (b, t, 0)),
            pl.BlockSpec((None, n_h, d), lambda b, t: (b, 0, 0)),
            pl.BlockSpec((None, n_buf, d), lambda b, t: (b, 0, 0)),
            pl.BlockSpec((d, 2 * d), const2),
            pl.BlockSpec((taps, d), const2),
            pl.BlockSpec((1, d), const2),
            pl.BlockSpec((nb, blk, blk), lambda b, t: (0, 0, 0)),
            pl.BlockSpec((1, d), const2),
            pl.BlockSpec((nb, blk, blk), lambda b, t: (0, 0, 0)),
            pl.BlockSpec((1, d), const2),
            pl.BlockSpec((1, d), const2),
            pl.BlockSpec((d, d), const2),
            pl.BlockSpec((1, d), const2),
            pl.BlockSpec((1, d), const2),
        ],
        out_specs=[
            pl.BlockSpec((None, row_tile, d), lambda b, t: (b, t, 0)),
            pl.BlockSpec((None, n_h, d), lambda b, t: (b, 0, 0)),
            pl.BlockSpec((None, n_buf, d), lambda b, t: (b, 0, 0)),
        ],
        out_shape=[
            jax.ShapeDtypeStruct((bsz, t_len, d), F32),
            jax.ShapeDtypeStruct((bsz, n_h, d), F32),
            jax.ShapeDtypeStruct((bsz, n_buf, d), F32),
        ],
        scratch_shapes=[pltpu.VMEM((taps - 1, d), F32), pltpu.VMEM((1, d), F32)],
        compiler_params=pltpu.CompilerParams(
            dimension_semantics=("arbitrary", "arbitrary"), vmem_limit_bytes=VMEM_LIMIT),
        name="lru_layer",
    )(x, h0, buf, w_in, conv_w, conv_b, w_a, b_a, w_x, b_x, lam, w_out, ln_g, ln_b)


D_ATT = N_HEADS * HEAD_DIM
N_ROWCOLS = 4 * N_KV * HEAD_DIM
N_WINCOLS = 2 * N_KV * HEAD_DIM
N_GATES = 3 * N_HEADS


def _nsa_proj_kernel(x_ref, wqkv_ref, wz_ref, wg_ref, q_ref, rows_ref, win_ref, sz_ref, gates_ref, *t_refs):
    xb = x_ref[...].astype(BF16)
    c0 = 0
    q_ref[...] = (_dot(xb, wqkv_ref[:, c0:c0 + D_ATT]) * (HEAD_DIM ** -0.5 * LOG2E)).astype(BF16)
    c0 += D_ATT
    rows = _dot(xb, wqkv_ref[:, c0:c0 + N_ROWCOLS])
    rows_ref[...] = rows
    c0 += N_ROWCOLS
    win = _dot(xb, wqkv_ref[:, c0:c0 + N_WINCOLS])
    win_ref[...] = win
    sz_ref[...] = _silu(_dot(xb, wz_ref[...])).astype(sz_ref.dtype)
    gates_ref[...] = jax.nn.sigmoid(_dot(xb, wg_ref[...]))
    if t_refs:
        rows_t_ref, win_t_ref = t_refs
        rows_t_ref[...] = rows.T
        win_t_ref[...] = win.T


def _nsa_proj(x2, weights, *, row_tile, seq_len=None):
    n, d = x2.shape
    widths = (D_ATT, N_ROWCOLS, N_WINCOLS, D_ATT, LANES)
    dtypes = (BF16, F32, F32, BF16, F32)
    out_specs = [pl.BlockSpec((row_tile, w), lambda i: (i, 0)) for w in widths]
    out_shape = [jax.ShapeDtypeStruct((n, w), dt) for w, dt in zip(widths, dtypes)]
    if seq_len is not None:
        tiles = seq_len // row_tile
        for w in (N_ROWCOLS, N_WINCOLS):
            out_specs.append(pl.BlockSpec((None, w, row_tile), lambda i: (i // tiles, 0, i % tiles)))
            out_shape.append(jax.ShapeDtypeStruct((n // seq_len, w, seq_len), F32))
    return pl.pallas_call(
        _nsa_proj_kernel,
        grid=(n // row_tile,),
        in_specs=[pl.BlockSpec((row_tile, d), lambda i: (i, 0))]
        + [pl.BlockSpec(w.shape, lambda i: (0, 0)) for w in weights],
        out_specs=out_specs,
        out_shape=out_shape,
        compiler_params=pltpu.CompilerParams(
            dimension_semantics=("arbitrary",), vmem_limit_bytes=VMEM_LIMIT),
        name="nsa_proj",
    )(x2, *weights)


def _nsa_out_kernel(x_ref, o_ref, sz_ref, w_ref, g_ref, b_ref, y_ref):
    m = (o_ref[...].astype(F32) * sz_ref[...].astype(F32)).astype(BF16)
    y_ref[...] = _ln_residual(x_ref[...], _dot(m, w_ref[...]), g_ref[...], b_ref[...])


def _nsa_out(x2, o2, sz2, w_out, ln_g, ln_b, *, row_tile):
    n, d = x2.shape
    row = lambda i: (i, 0)
    const = lambda i: (0, 0)
    return pl.pallas_call(
        _nsa_out_kernel,
        grid=(n // row_tile,),
        in_specs=[pl.BlockSpec((row_tile, d), row), pl.BlockSpec((row_tile, d), row),
                  pl.BlockSpec((row_tile, d), row), pl.BlockSpec((d, d), const),
                  pl.BlockSpec((1, d), const), pl.BlockSpec((1, d), const)],
        out_specs=pl.BlockSpec((row_tile, d), row),
        out_shape=jax.ShapeDtypeStruct((n, d), F32),
        compiler_params=pltpu.CompilerParams(
            dimension_semantics=("arbitrary",), vmem_limit_bytes=VMEM_LIMIT),
        name="nsa_out",
    )(x2, o2, sz2, w_out, ln_g, ln_b)


def _dup_half(slab, g, lo):
    rolled = pltpu.roll(slab, HEAD_DIM, 1)
    return jnp.where(lo, slab, rolled) if g == 0 else jnp.where(lo, rolled, slab)


def _select_blocks(score, nl, real, n_slc, n_sel, axis, stride):
    rank = jnp.zeros(score.shape, jnp.int32)
    for i in range(n_slc):
        j = stride * i
        cand = score[:, j:j + 1] if axis == 1 else score[j:j + 1, :]
        beats = (cand > score) | ((cand == score) & (nl > j))
        rank = rank + beats.astype(jnp.int32)
    return real & (rank < n_sel)


def _block_scores(imp2, nl, tq, n_slc, stride):
    blk = nl >> (stride - 1)
    real = ((nl & (stride - 1)) == 0) & (blk < n_slc)
    cur = tq >> (SLC_BLOCK.bit_length() - 1)
    forced = (blk == 0) | ((blk <= cur) & (blk > cur - N_LOCAL))
    score = jnp.where(forced, FORCE, jnp.where(blk <= cur, imp2, NEG_INF))
    return jnp.where(real, score, -jnp.inf), real


def _alibi_query_cols(head, n_rows):
    lane = lax.broadcasted_iota(jnp.int32, (1, LANES), 1)
    rem = jnp.full((1, LANES), _slope(head) * LOG2E, F32)
    row = jnp.zeros((1, LANES), F32)
    for i in range(3):
        piece = rem.astype(BF16).astype(F32)
        rem = rem - piece
        row = jnp.where(lane == i, float(ALIBI_RADIX) * piece, jnp.where(lane == 3 + i, piece, row))
    return jnp.broadcast_to(row, (n_rows, LANES)).astype(BF16)


def _alibi_key_cols(n_rows, period):
    r = lax.broadcasted_iota(jnp.int32, (n_rows, LANES), 0) & (period - 1)
    lane = lax.broadcasted_iota(jnp.int32, (n_rows, LANES), 1)
    hi = (r // ALIBI_RADIX).astype(F32)
    lo = (r % ALIBI_RADIX).astype(F32)
    return jnp.where(lane < 3, hi, jnp.where(lane < 6, lo, 0.0))


def _nsa_prompt_kernel(q_ref, rows_ref, win_ref, gates_ref, wc_ref, o_ref,
                       ka, vta, kwa, vwta, kcd, vct_lo, vct_hi, qa_ref, selb_ref, acc_ref):
    blk = q_ref.shape[0]
    t_len = rows_ref.shape[0]
    n_cmp = t_len // CMP_BLOCK
    n_slc = -(-t_len // SLC_BLOCK)
    n_sel = min(TOP_N, n_slc)
    n_blk = t_len // blk
    slc_per_blk = blk // SLC_BLOCK
    n_wblk = WINDOW // blk + 1
    v_rows = vta.shape[2]
    qi = pl.program_id(1)
    t0 = qi * blk
    lo = lax.broadcasted_iota(jnp.int32, (1, LANES), 1) < HEAD_DIM

    @pl.when(qi == 0)
    def _build():
        lo_t = lax.broadcasted_iota(jnp.int32, (t_len, LANES), 1) < HEAD_DIM
        lo_c = lax.broadcasted_iota(jnp.int32, (LANES, LANES), 1) < HEAD_DIM
        lo_r = lax.broadcasted_iota(jnp.int32, (LANES, 1), 0) < HEAD_DIM
        vrow = lax.broadcasted_iota(jnp.int32, (v_rows, 1), 0)
        w2 = jnp.concatenate([wc_ref[...]] * 2, axis=0)
        cw = rows_ref[:, 0:2 * LANES].reshape(n_slc, SLC_BLOCK, 2 * LANES) * w2[None]
        kvc = jnp.concatenate([jnp.sum(cw[:, 0:CMP_BLOCK, :], axis=1), jnp.sum(cw[:, CMP_BLOCK:, :], axis=1),
                               jnp.zeros((LANES - n_cmp, 2 * LANES), F32)], axis=0)
        key_cols = _alibi_key_cols(t_len, blk).astype(BF16)

        def values_t(v):
            vt = v.T[0:v_rows, :]
            return jnp.where(vrow < HEAD_DIM, vt, jnp.where(vrow == HEAD_DIM, 1.0, 0.0)).astype(BF16)

        for g in range(N_KV):
            kd = _dup_half(rows_ref[:, 2 * LANES:3 * LANES], g, lo_t).astype(BF16)
            ka[g] = jnp.concatenate([kd, key_cols], axis=1)
            vt = values_t(_dup_half(rows_ref[:, 3 * LANES:4 * LANES], g, lo_t))
            kd = _dup_half(win_ref[:, 0:LANES], g, lo_t).astype(BF16)
            kwa[g] = jnp.concatenate([kd, key_cols], axis=1)
            vwt = values_t(_dup_half(win_ref[:, LANES:2 * LANES], g, lo_t))
            for c in range(n_blk):
                vta[g, c] = vt[:, c * blk:(c + 1) * blk]
                vwta[g, c] = vwt[:, c * blk:(c + 1) * blk]
            kcd[g] = _dup_half(kvc[:, 0:LANES], g, lo_c).astype(BF16)
            vct = _dup_half(kvc[:, LANES:2 * LANES], g, lo_c).T
            vct_lo[g] = jnp.where(lo_r, vct, 0.0).astype(BF16)
            vct_hi[g] = jnp.where(lo_r, 0.0, vct).astype(BF16)

    tq = t0 + lax.broadcasted_iota(jnp.int32, (1, blk), 1)
    row_c = lax.broadcasted_iota(jnp.int32, (LANES, 1), 0)
    cmp_blk = jnp.where(row_c < n_slc, 2 * row_c, 2 * (row_c - n_slc) + 1)
    cmp_end = cmp_blk * CMP_BLOCK + (CMP_BLOCK - 1)
    valid_c = (cmp_end <= tq) & (row_c < n_cmp)
    dist_c = (tq - cmp_end).astype(F32)
    slc_r = lax.broadcasted_iota(jnp.int32, (n_slc, 1), 0)

    lr = lax.broadcasted_iota(jnp.int32, (1, blk), 1) - lax.broadcasted_iota(jnp.int32, (blk, 1), 0)
    causal_bias = jnp.where(lr >= 0, 0.0, NEG_INF)

    w_start = pl.multiple_of(jnp.maximum(t0 - WINDOW, 0), blk)
    w_blk = w_start // blk
    w_shift, w_bias = [], []
    for i in range(n_wblk):
        shift = t0 - w_start - i * blk
        dist = lr + shift
        w_shift.append(shift.astype(F32))
        w_bias.append(jnp.where((dist >= 0) & (dist < WINDOW), 0.0, NEG_INF))

    gates_t = gates_ref[...].T

    for hh in range(N_HEADS):
        q2 = q_ref[:, (hh // 2) * LANES:(hh // 2 + 1) * LANES]
        zero = jnp.zeros_like(q2)
        qm = jnp.where(lo, q2, zero) if hh % 2 == 0 else jnp.where(lo, zero, q2)
        qa_ref[hh] = jnp.concatenate([qm, _alibi_query_cols(hh, blk)], axis=1)

    o_cmp = []
    for g in range(N_KV):
        heads = range(g * HPG, (g + 1) * HPG)
        scores = [_dot_nt(kcd[g], qa_ref[hh, :, 0:LANES]) for hh in heads]
        imp = jnp.zeros((LANES, blk), F32)
        probs = []
        for hh, s in zip(heads, scores):
            s = jnp.where(valid_c, s - (_slope(hh) * LOG2E) * dist_c, NEG_INF)
            e = jnp.exp2(s - jnp.max(s, axis=0, keepdims=True))
            p = jnp.where(valid_c, e, 0.0) / jnp.sum(e, axis=0, keepdims=True)
            imp = imp + p
            probs.append(p.astype(BF16))
        for j in range(HPG // 2):
            o_cmp.append(_dot(vct_lo[g], probs[2 * j]) + _dot(vct_hi[g], probs[2 * j + 1]))

        imp2 = imp[0:n_slc, :] + imp[n_slc:2 * n_slc, :]
        score, real = _block_scores(imp2, slc_r, tq, n_slc, stride=1)
        sel = _select_blocks(score, slc_r, real, n_slc, n_sel, axis=0, stride=1)
        sel_bias = jnp.where(sel, 0.0, NEG_INF)
        for c in range(n_blk):
            rows = sel_bias[c * slc_per_blk:(c + 1) * slc_per_blk, :]
            selb_ref[g, c] = jnp.concatenate([rows] * (SUBLANES // slc_per_blk), axis=0)

    acc_ref[...] = jnp.zeros(acc_ref.shape, F32)

    def key_block(c, ms, extra_bias):
        off = pl.multiple_of(c * blk, blk)
        shift = (t0 - off).astype(F32)
        keys, biases = [], []
        for g in range(N_KV):
            keys.append(ka[g, pl.ds(off, blk), :])
            sel_rows = selb_ref[g, c]
            bias = jnp.concatenate([jnp.broadcast_to(sel_rows[m:m + 1, :], (SLC_BLOCK, blk))
                                    for m in range(slc_per_blk)], axis=0)
            biases.append(bias if extra_bias is None else bias + extra_bias)
        scores, new_ms, probs = [], [], []
        for hh in range(N_HEADS):
            s = _dot_nt(keys[hh // HPG], qa_ref[hh]) + biases[hh // HPG]
            c_h = (_slope(hh) * LOG2E) * shift
            new_ms.append(jnp.maximum(ms[hh], jnp.max(s, axis=0, keepdims=True) - c_h))
            scores.append(s)
        for hh in range(N_HEADS):
            c_h = (_slope(hh) * LOG2E) * shift
            probs.append(jnp.exp2((scores[hh] - (new_ms[hh] + c_h)).astype(BF16)))
        for hh in range(N_HEADS):
            acc_ref[hh] = acc_ref[hh] * jnp.exp2(ms[hh] - new_ms[hh]) + _dot(vta[hh // HPG, c], probs[hh])
        return tuple(new_ms)

    ms = lax.fori_loop(0, qi, lambda c, ms: key_block(c, ms, None),
                       (jnp.full((1, blk), NEG_INF, F32),) * N_HEADS)
    key_block(qi, ms, causal_bias)

    k_win = [[kwa[g, pl.ds(pl.multiple_of(w_start + i * blk, blk), blk), :] for i in range(n_wblk)]
             for g in range(N_KV)]
    w_scores, w_max = [], []
    for hh in range(N_HEADS):
        slope = _slope(hh) * LOG2E
        tiles, m_w = [], None
        for i in range(n_wblk):
            s = _dot_nt(k_win[hh // HPG][i], qa_ref[hh]) + w_bias[i]
            m_i = jnp.max(s, axis=0, keepdims=True) - slope * w_shift[i]
            m_w = m_i if m_w is None else jnp.maximum(m_w, m_i)
            tiles.append(s)
        w_scores.append(tiles)
        w_max.append(m_w)
    o_wins = []
    for hh in range(N_HEADS):
        slope = _slope(hh) * LOG2E
        acc_w = jnp.zeros((v_rows, blk), F32)
        for i in range(n_wblk):
            p = jnp.exp2((w_scores[hh][i] - (w_max[hh] + slope * w_shift[i])).astype(BF16))
            acc_w = acc_w + _dot(vwta[hh // HPG, w_blk + i], p)
        o_wins.append(acc_w[0:HEAD_DIM, :] / acc_w[HEAD_DIM:HEAD_DIM + 1, :])

    for g in range(N_KV):
        for j in range(HPG // 2):
            outs = []
            for hh in (g * HPG + 2 * j, g * HPG + 2 * j + 1):
                o_win = o_wins[hh]
                acc = acc_ref[hh]
                o_sel = acc[0:HEAD_DIM, :] / acc[HEAD_DIM:HEAD_DIM + 1, :]
                half = (hh % 2) * HEAD_DIM
                gate = lambda c: gates_t[c * N_HEADS + hh:c * N_HEADS + hh + 1, :]
                o_c = o_cmp[g * (HPG // 2) + j][half:half + HEAD_DIM, :]
                outs.append(gate(0) * o_c + gate(1) * o_sel + gate(2) * o_win)
            c0 = g * HPG * HEAD_DIM + j * LANES
            o_ref[:, c0:c0 + LANES] = jnp.concatenate(outs, axis=0).T.astype(o_ref.dtype)


def _block_expand_matrix(n_rows, n_keys):
    n = jnp.arange(n_rows, dtype=jnp.int32)[:, None]
    p = jnp.arange(n_keys, dtype=jnp.int32)[None, :]
    return (n == 2 * (p // SLC_BLOCK)).astype(BF16)


def _nsa_prompt_attention(q, rows, win, gates, wc, *, q_tile=NSA_BLOCK):
    bsz, t_len, _ = q.shape
    assert WINDOW % q_tile == 0 and q_tile % (2 * LANES) == 0 and t_len >= WINDOW + q_tile
    assert t_len // CMP_BLOCK <= LANES and t_len % SLC_BLOCK == 0 and SUBLANES % (q_tile // SLC_BLOCK) == 0
    v_rows = HEAD_DIM + BF16_SUBLANES
    k_scratch = pltpu.VMEM((N_KV, t_len, 2 * LANES), BF16)
    v_scratch = pltpu.VMEM((N_KV, t_len // q_tile, v_rows, q_tile), BF16)
    c_scratch = pltpu.VMEM((N_KV, LANES, LANES), BF16)
    return pl.pallas_call(
        _nsa_prompt_kernel,
        grid=(bsz, t_len // q_tile),
        in_specs=[
            pl.BlockSpec((None, q_tile, D_ATT), lambda b, i: (b, i, 0)),
            pl.BlockSpec((None, t_len, N_ROWCOLS), lambda b, i: (b, 0, 0)),
            pl.BlockSpec((None, t_len, N_WINCOLS), lambda b, i: (b, 0, 0)),
            pl.BlockSpec((None, q_tile, LANES), lambda b, i: (b, i, 0)),
            pl.BlockSpec((CMP_BLOCK, 2 * LANES), lambda b, i: (0, 0)),
        ],
        out_specs=pl.BlockSpec((None, q_tile, D_ATT), lambda b, i: (b, i, 0)),
        out_shape=jax.ShapeDtypeStruct((bsz, t_len, D_ATT), BF16),
        scratch_shapes=[k_scratch, v_scratch, k_scratch, v_scratch,
                        c_scratch, c_scratch, c_scratch,
                        pltpu.VMEM((N_HEADS, q_tile, 2 * LANES), BF16),
                        pltpu.VMEM((N_KV, t_len // q_tile, SUBLANES, q_tile), F32),
                        pltpu.VMEM((N_HEADS, v_rows, q_tile), F32)],
        compiler_params=pltpu.CompilerParams(
            dimension_semantics=("arbitrary", "arbitrary"), vmem_limit_bytes=VMEM_LIMIT),
        name="nsa_prompt_attention",
    )(q, rows, win, gates, wc)


def _nsa_sample_kernel(pt_ref, *refs, pages_per_step, n_steps, n_new):
    page_refs = refs[:pages_per_step]
    (qbd_ref, new_ref, winbuf_ref, gates_ref, wc_ref, e_ref, o_ref, s_ref, vst_ref, kcs, vcs) = refs[pages_per_step:]
    del pt_ref
    step = pl.program_id(1)
    keys_per_step = pages_per_step * PAGE_SIZE
    p_len = n_steps * keys_per_step
    n_keys = p_len + LANES
    n_cmp = p_len // CMP_BLOCK
    n_slc = -(-(p_len + n_new) // SLC_BLOCK)
    n_sel = min(TOP_N, n_slc)
    cmp_per_step = keys_per_step // CMP_BLOCK
    n_rows = N_KV * n_new * HPG
    gd = N_KV * HEAD_DIM

    def slab(r, kind):
        return r[kind].reshape(gd, r.shape[-1])

    cm = jnp.concatenate([jnp.concatenate([slab(r, 0).T, slab(r, 1).T], axis=1) for r in page_refs], axis=0)
    kvc = jnp.sum(cm.reshape(cmp_per_step, CMP_BLOCK, 2 * LANES) * wc_ref[...][None], axis=1)
    c_off = pl.multiple_of(step * cmp_per_step, cmp_per_step)
    kcs[pl.ds(c_off, cmp_per_step), :] = kvc[:, 0:LANES].astype(BF16)
    vcs[pl.ds(c_off, cmp_per_step), :] = kvc[:, LANES:2 * LANES].astype(BF16)
    s_ref[step] = _dot(qbd_ref[...], jnp.concatenate([slab(r, 2).astype(BF16) for r in page_refs], axis=1))
    vst_ref[step] = jnp.concatenate([slab(r, 3).astype(BF16) for r in page_refs], axis=1)

    @pl.when(step == n_steps - 1)
    def _attend():
        n_wb = winbuf_ref.shape[-1]
        w_keys = n_wb + LANES
        qbd = qbd_ref[...]
        r = lax.broadcasted_iota(jnp.int32, (n_rows, 1), 0)
        head = (r // (n_new * HPG)) * HPG + (r % HPG)
        slope = jnp.exp2((head + 1).astype(F32) * (-8.0 / N_HEADS)) * LOG2E
        tq = p_len + (r // HPG) % n_new

        def attend(s, values_t, dist, bias):
            s = s - slope * dist.astype(F32) + bias
            e = jnp.exp2(s - jnp.max(s, axis=-1, keepdims=True))
            eb = e.astype(BF16)
            o = sum(_dot_nt(eb[:, ks], vt) for ks, vt in values_t)
            return o / jnp.sum(e, axis=-1, keepdims=True)

        nl = lax.broadcasted_iota(jnp.int32, (n_rows, n_cmp), 1)
        cmp_end = nl * CMP_BLOCK + (CMP_BLOCK - 1)
        valid_c = cmp_end <= tq
        s = _dot_nt(qbd, kcs[...]) - slope * (tq - cmp_end).astype(F32)
        s = jnp.where(valid_c, s, NEG_INF)
        e = jnp.exp2(s - jnp.max(s, axis=-1, keepdims=True))
        p = jnp.where(valid_c, e, 0.0) / jnp.sum(e, axis=-1, keepdims=True)
        o_cmp = _dot(p.astype(BF16), vcs[...])

        n_gt = N_KV * n_new
        imp = jnp.sum(p.reshape(n_gt, HPG, n_cmp), axis=1)
        imp2 = imp + pltpu.roll(imp, n_cmp - 1, 1)
        w_sel = -(-2 * n_slc // LANES) * LANES
        imp2 = jnp.concatenate([imp2, jnp.zeros((n_gt, w_sel - n_cmp), F32)], axis=1)
        nl8 = lax.broadcasted_iota(jnp.int32, (n_gt, w_sel), 1)
        r8 = lax.broadcasted_iota(jnp.int32, (n_gt, 1), 0)
        score, real = _block_scores(imp2, nl8, p_len + r8 % n_new, n_slc, stride=2)
        sel = _select_blocks(score, nl8, real, n_slc, n_sel, axis=1, stride=2)
        selb = jnp.where(sel, 1.0, 0.0)
        selb = jnp.broadcast_to(selb[:, None, :], (n_gt, HPG, w_sel)).reshape(n_rows, w_sel).astype(BF16)
        keys_per_tile = e_ref.shape[1]
        hit = [_dot(selb[:, j * LANES:(j + 1) * LANES], e_ref[...]) for j in range(p_len // keys_per_tile)]
        hit.append(jnp.broadcast_to(selb[:, 2 * (p_len // SLC_BLOCK):2 * (p_len // SLC_BLOCK) + 1].astype(F32),
                                    (n_rows, LANES)))
        hit = jnp.concatenate(hit, axis=1)
        dist = tq - lax.broadcasted_iota(jnp.int32, (n_rows, n_keys), 1)
        new_k, new_v, new_kw, new_vw = (new_ref[i] for i in range(4))
        s_sel = jnp.concatenate([s_ref[i] for i in range(n_steps)] + [_dot(qbd, new_k)], axis=1)
        values_t = [(slice(i * keys_per_step, (i + 1) * keys_per_step), vst_ref[i]) for i in range(n_steps)]
        o_sel = attend(s_sel, values_t + [(slice(p_len, n_keys), new_v)], dist,
                       jnp.where((hit > 0.5) & (dist >= 0), 0.0, NEG_INF))

        dist_w = tq - (p_len - n_wb) - lax.broadcasted_iota(jnp.int32, (n_rows, w_keys), 1)
        s_win = jnp.concatenate([_dot(qbd, slab(winbuf_ref, 0).astype(BF16)), _dot(qbd, new_kw)], axis=1)
        o_win = attend(s_win, [(slice(0, n_wb), slab(winbuf_ref, 1).astype(BF16)), (slice(n_wb, w_keys), new_vw)],
                       dist_w, jnp.where((dist_w >= 0) & (dist_w < WINDOW), 0.0, NEG_INF))

        o_ref[...] = (gates_ref[:, 0:1] * o_cmp + gates_ref[:, 1:2] * o_sel + gates_ref[:, 2:3] * o_win)


def _nsa_sample_attention(page_table, cache_t, qbd, new_t, winbuf_t, gates, wc, *, n_new,
                          pages_per_step=PAGES_PER_STEP):
    bsz, n_pages = page_table.shape
    n_steps = n_pages // pages_per_step
    keys_per_step = pages_per_step * PAGE_SIZE
    p_len = n_pages * PAGE_SIZE
    n_rows = qbd.shape[1]
    n_wb = winbuf_t.shape[-1]
    n_cmp = p_len // CMP_BLOCK
    gd = N_KV * HEAD_DIM
    keys_per_tile = (LANES // 2) * SLC_BLOCK
    assert p_len % keys_per_tile == 0 and n_new <= SLC_BLOCK
    e_mat = _block_expand_matrix(LANES, keys_per_tile)
    kern = functools.partial(_nsa_sample_kernel, pages_per_step=pages_per_step, n_steps=n_steps, n_new=n_new)

    def page_spec(k):
        return pl.BlockSpec((None, 4, N_KV, HEAD_DIM, PAGE_SIZE),
                            lambda b, s, pt: (pt[b, s * pages_per_step + k], 0, 0, 0, 0))

    per_b = lambda b, s, pt: (b, 0, 0)
    const = lambda b, s, pt: (0, 0)
    grid_spec = pltpu.PrefetchScalarGridSpec(
        num_scalar_prefetch=1,
        grid=(bsz, n_steps),
        in_specs=[page_spec(k) for k in range(pages_per_step)] + [
            pl.BlockSpec((None, n_rows, LANES), per_b),
            pl.BlockSpec((None, 4, gd, LANES), lambda b, s, pt: (b, 0, 0, 0)),
            pl.BlockSpec((None, 2, N_KV, HEAD_DIM, n_wb), lambda b, s, pt: (b, 0, 0, 0, 0)),
            pl.BlockSpec((None, n_rows, LANES), per_b),
            pl.BlockSpec((CMP_BLOCK, 2 * LANES), const),
            pl.BlockSpec((LANES, keys_per_tile), const),
        ],
        out_specs=pl.BlockSpec((None, n_rows, LANES), per_b),
        scratch_shapes=[
            pltpu.VMEM((n_steps, n_rows, keys_per_step), F32),
            pltpu.VMEM((n_steps, gd, keys_per_step), BF16),
            pltpu.VMEM((n_cmp, LANES), BF16), pltpu.VMEM((n_cmp, LANES), BF16),
        ],
    )
    return pl.pallas_call(
        kern,
        grid_spec=grid_spec,
        out_shape=jax.ShapeDtypeStruct((bsz, n_rows, LANES), F32),
        compiler_params=pltpu.CompilerParams(
            dimension_semantics=("arbitrary", "arbitrary"), vmem_limit_bytes=VMEM_LIMIT),
        name="nsa_sample_attention",
    )(page_table, *([cache_t] * pages_per_step), qbd, new_t, winbuf_t, gates, wc, e_mat)


def _nsa_weights(w_in):
    c_g = D_ATT + N_ROWCOLS + N_WINCOLS
    c_z = c_g + N_GATES
    w_gates = jnp.pad(w_in[:, c_g:c_z], ((0, 0), (0, LANES - N_GATES)))
    return w_in[:, :c_g].astype(BF16), w_in[:, c_z:].astype(BF16), w_gates.astype(BF16)


def _cmp_weight_tile(w_cmp):
    return w_cmp.transpose(1, 0, 2, 3).reshape(CMP_BLOCK, 2 * N_KV * HEAD_DIM)


def _nsa_layer_prompt(x, w_all, wc, w_out, ln_g, ln_b):
    bsz, t_len, d = x.shape
    x2 = x.reshape(bsz * t_len, d)
    q, rows, win, sz, gates, rows_t, win_t = _nsa_proj(x2, w_all, row_tile=ROW_TILE, seq_len=t_len)
    o = _nsa_prompt_attention(q.reshape(bsz, t_len, D_ATT), rows.reshape(bsz, t_len, N_ROWCOLS),
                              win.reshape(bsz, t_len, N_WINCOLS), gates.reshape(bsz, t_len, LANES), wc)
    y = _nsa_out(x2, o.reshape(bsz * t_len, D_ATT), sz, w_out, ln_g, ln_b, row_tile=ROW_TILE)
    keep = min(WINDOW, t_len)
    rows_out = rows_t.reshape(bsz, 4, N_KV, HEAD_DIM, t_len).transpose(0, 4, 1, 2, 3)
    win_out = win_t[:, :, t_len - keep:].reshape(bsz, 2, N_KV, HEAD_DIM, keep).transpose(0, 4, 1, 2, 3)
    return y.reshape(bsz, t_len, d), rows_out, win_out


def _nsa_layer_sample(x, cache, win_buf, page_table, w_all, wc, w_out, ln_g, ln_b):
    bsz, n_new, d = x.shape
    x2 = x.reshape(bsz * n_new, d)
    q, rows, win, sz, gates = _nsa_proj(x2, w_all, row_tile=bsz * n_new)
    q5 = q.reshape(bsz, n_new, N_KV, HPG, HEAD_DIM).transpose(0, 2, 1, 3, 4)
    eye = jnp.eye(N_KV, dtype=q.dtype)
    qbd = (q5[:, :, :, :, None, :] * eye[None, :, None, None, :, None]).reshape(
        bsz, N_KV * n_new * HPG, N_KV * HEAD_DIM)
    g5 = gates[:, :N_GATES].reshape(bsz, n_new, 3, N_KV, HPG).transpose(0, 3, 1, 4, 2)
    g_rows = jnp.pad(g5.reshape(bsz, N_KV * n_new * HPG, 3), ((0, 0), (0, 0), (0, LANES - 3)))
    rows3 = rows.reshape(bsz, n_new, N_ROWCOLS)
    win3 = win.reshape(bsz, n_new, N_WINCOLS)
    n_wb = win_buf.shape[1]
    gd = N_KV * HEAD_DIM
    new_t = jnp.stack([rows3[:, :, 2 * gd:3 * gd], rows3[:, :, 3 * gd:4 * gd],
                       win3[:, :, 0:gd], win3[:, :, gd:2 * gd]], axis=1).transpose(0, 1, 3, 2)
    new_t = jnp.pad(new_t, ((0, 0), (0, 0), (0, 0), (0, LANES - n_new))).astype(BF16)
    o_rows = _nsa_sample_attention(
        page_table, cache.transpose(0, 2, 3, 4, 1), qbd, new_t, win_buf.transpose(0, 2, 3, 4, 1),
        g_rows, wc, n_new=n_new)
    o6 = o_rows.reshape(bsz, N_KV, n_new, HPG, N_KV, HEAD_DIM)
    o = jnp.stack([o6[:, g, :, :, g, :] for g in range(N_KV)], axis=2).reshape(bsz * n_new, D_ATT)
    y = _nsa_out(x2, o, sz, w_out, ln_g, ln_b, row_tile=bsz * n_new)
    all_win = jnp.concatenate([win_buf, win3.reshape(bsz, n_new, 2, N_KV, HEAD_DIM)], axis=1)
    return (y.reshape(bsz, n_new, d), rows3.reshape(bsz, n_new, 4, N_KV, HEAD_DIM),
            all_win[:, all_win.shape[1] - n_wb:])


def _time_major(a):
    bsz, k, d = a.shape
    return a.transpose(1, 0, 2).reshape(1, k * bsz, d)


def _batch_major(a, bsz):
    _, kb, d = a.shape
    return a.reshape(kb // bsz, bsz, d).transpose(1, 0, 2)


def kernel(x_prompt, x_sample, cache_nsa_kv, cache_nsa_win, state_conv_a, state_lru_h, state_lru_conv, page_table,
           ln_g, ln_b, a_w_in, a_conv_w, a_w_out, b_w_in, b_w_cmp, b_w_out,
           c_w_in, c_conv_w, c_conv_b, c_w_a, c_b_a, c_w_x, c_b_x, c_lam, c_w_out):
    bp, t_len, d = x_prompt.shape
    bs, n_new, _ = x_sample.shape
    yp = x_prompt
    ys = _time_major(x_sample)
    conv_a_p, conv_a_s = [], []
    rows_p, rows_s, win_p, win_s = [], [], [], []
    h_p, h_s, cc_p, cc_s = [], [], [], []
    row2 = lambda v: v.reshape(1, -1)
    a_w_in_b, a_w_out_b = a_w_in.astype(BF16), a_w_out.astype(BF16)
    for i in range(DEPTH):
        kind, j = i % 3, i // 3
        g, b = row2(ln_g[i]), row2(ln_b[i])
        if kind == 0:
            taps = a_conv_w.shape[1]
            yp, sp = _conv_layer(yp, jnp.zeros((bp, taps - 1, d), F32), a_w_in_b, a_conv_w, a_w_out_b, g, b,
                                 layer=j, stride=1, row_tile=ROW_TILE)
            ys, ss = _conv_layer(ys, _time_major(state_conv_a[j]), a_w_in_b, a_conv_w, a_w_out_b, g, b,
                                 layer=j, stride=bs, row_tile=n_new * bs)
            conv_a_p.append(sp)
            conv_a_s.append(_batch_major(ss, bs))
        elif kind == 1:
            w_all = _nsa_weights(b_w_in[j])
            wc = _cmp_weight_tile(b_w_cmp[j])
            w_out = b_w_out[j].astype(BF16)
            yp, rp, wp = _nsa_layer_prompt(yp, w_all, wc, w_out, g, b)
            ys_b, rs, ws = _nsa_layer_sample(_batch_major(ys, bs), cache_nsa_kv[j], cache_nsa_win[j], page_table,
                                             w_all, wc, w_out, g, b)
            ys = _time_major(ys_b)
            rows_p.append(rp)
            rows_s.append(rs)
            win_p.append(wp)
            win_s.append(ws)
        else:
            w_in, w_out = c_w_in[j].astype(BF16), c_w_out[j].astype(BF16)
            w_a, w_x = c_w_a[j].astype(BF16), c_w_x[j].astype(BF16)
            taps = c_conv_w.shape[1]
            args = (w_in, c_conv_w[j], row2(c_conv_b[j]), w_a, row2(c_b_a[j]), w_x, row2(c_b_x[j]),
                    row2(c_lam[j]), w_out, g, b)
            yp, hp, cp = _lru_layer(yp, jnp.zeros((bp, 1, d), F32), jnp.zeros((bp, taps - 1, d), F32), *args,
                                    stride=1, row_tile=ROW_TILE)
            ys, hs, cs = _lru_layer(ys, state_lru_h[j].reshape(1, bs, d), _time_major(state_lru_conv[j]), *args,
                                    stride=bs, row_tile=n_new * bs)
            h_p.append(hp.reshape(bp, d))
            h_s.append(hs.reshape(bs, d))
            cc_p.append(cp)
            cc_s.append(_batch_major(cs, bs))
    return (yp, _batch_major(ys, bs), jnp.stack(conv_a_p), jnp.stack(conv_a_s), jnp.stack(rows_p), jnp.stack(rows_s),
            jnp.stack(win_p), jnp.stack(win_s), jnp.stack(h_p), jnp.stack(h_s), jnp.stack(cc_p), jnp.stack(cc_s))
```

```python
import functools

import jax
import jax.numpy as jnp
from jax import lax
from jax.experimental import pallas as pl
from jax.experimental.pallas import tpu as pltpu

F32 = jnp.float32
BF16 = jnp.bfloat16

DEPTH = 4
N_HEADS = 16
N_KV = 2
HPG = N_HEADS // N_KV
HEAD_DIM = 64
CMP_BLOCK = 32
SLC_BLOCK = 64
TOP_N = 16
N_LOCAL = 2
WINDOW = 512
PAGE_SIZE = 128
NEG_INF = -1e30
FORCE = 1e9
RG_C = 8.0
N_RG_BLOCKS = 4
ALPHA = (2 * DEPTH) ** 0.25
LN_EPS = 1e-5
LOG2E = 1.4426950408889634

LANES = 128
SUBLANES = 8
BF16_SUBLANES = 16
VMEM_LIMIT = 56 * 1024 * 1024

ROW_TILE = 512
NSA_BLOCK = 256
PAGES_PER_STEP = 64
ALIBI_RADIX = 16


def _dot(a, b):
    return jnp.dot(a, b, preferred_element_type=F32)


def _dot_nt(a, b):
    return lax.dot_general(a, b, (((1,), (1,)), ((), ())), preferred_element_type=F32)


def _silu(z):
    return z * jax.nn.sigmoid(z)


def _ln_residual(x, y, g, b):
    r = ALPHA * x + y
    mu = jnp.mean(r, axis=-1, keepdims=True)
    c = r - mu
    var = jnp.mean(c * c, axis=-1, keepdims=True)
    return c * lax.rsqrt(var + LN_EPS) * g + b


def _slope(head):
    return 2.0 ** (-8.0 * (head + 1) / N_HEADS)


def _prev_rows(u, carry, k):
    n_carry = carry.shape[0]
    rolled = pltpu.roll(u, k, 0)
    row = lax.broadcasted_iota(jnp.int32, (SUBLANES, 1), 0)
    head = rolled[0:SUBLANES, :]
    for i in range(k):
        head = jnp.where(row == i, carry[n_carry - k + i:n_carry - k + i + 1, :], head)
    return jnp.concatenate([head, rolled[SUBLANES:, :]], axis=0)


def _conv_layer_kernel(x_ref, buf_ref, win_ref, cw_ref, wout_ref, g_ref, b_ref,
                       y_ref, nbuf_ref, carry_ref, *, stride, col_chunk, n_sub):
    d = x_ref.shape[1]
    rows = x_ref.shape[0] // n_sub
    taps = cw_ref.shape[0]
    n_chunks = d // col_chunk
    if stride == 1:
        @pl.when(pl.program_id(1) == 0)
        def _():
            carry_ref[...] = buf_ref[...]

    def mixer_chunk(xb, acc, j):
        lo = j * col_chunk
        sl = slice(lo, lo + col_chunk)
        h, bg, cg, z = [_dot(xb, win_ref[:, part * d + lo:part * d + lo + col_chunk]) for part in range(4)]
        u = cg * h
        conv = u * cw_ref[taps - 1:taps, sl]
        if stride == 1:
            carry = carry_ref[:, sl]
            for k in range(1, taps):
                conv = conv + _prev_rows(u, carry, k) * cw_ref[taps - 1 - k:taps - k, sl]
            new_buf = u[rows - (taps - 1):rows, :]
            carry_ref[:, sl] = new_buf
        else:
            ext = jnp.concatenate([buf_ref[:, sl], u], axis=0)
            for k in range(1, taps):
                s0 = (taps - 1 - k) * stride
                conv = conv + ext[s0:s0 + rows, :] * cw_ref[taps - 1 - k:taps - k, sl]
            new_buf = ext[rows:rows + (taps - 1) * stride, :]
        nbuf_ref[:, sl] = new_buf
        m = _silu(z) * bg * conv
        return acc + _dot(m.astype(BF16), wout_ref[sl, :])

    pending = None
    for s in range(n_sub):
        x = x_ref[s * rows:(s + 1) * rows, :]
        xb = x.astype(BF16)
        acc = jnp.zeros((rows, d), F32)
        for j in range(n_chunks):
            acc = mixer_chunk(xb, acc, j)
            if j == 0 and pending is not None:
                ps, px, pacc = pending
                y_ref[ps * rows:(ps + 1) * rows, :] = _ln_residual(px, pacc, g_ref[...], b_ref[...])
        pending = (s, x, acc)
    ps, px, pacc = pending
    y_ref[ps * rows:(ps + 1) * rows, :] = _ln_residual(px, pacc, g_ref[...], b_ref[...])


def _conv_layer(x, buf, w_in, conv_w, w_out, ln_g, ln_b, *, layer, stride, row_tile):
    bsz, t_len, d = x.shape
    n_buf = buf.shape[1]
    taps = conv_w.shape[1]
    grid = (bsz, t_len // row_tile)
    n_sub = 2 if stride == 1 and row_tile % (2 * SUBLANES) == 0 else 1
    kern = functools.partial(_conv_layer_kernel, stride=stride, col_chunk=512, n_sub=n_sub)
    return pl.pallas_call(
        kern,
        grid=grid,
        in_specs=[
            pl.BlockSpec((None, row_tile, d), lambda b, t: (b, t, 0)),
            pl.BlockSpec((None, n_buf, d), lambda b, t: (b, 0, 0)),
            pl.BlockSpec((None, d, 4 * d), lambda b, t: (layer, 0, 0)),
            pl.BlockSpec((None, taps, d), lambda b, t: (layer, 0, 0)),
            pl.BlockSpec((None, d, d), lambda b, t: (layer, 0, 0)),
            pl.BlockSpec((1, d), lambda b, t: (0, 0)),
            pl.BlockSpec((1, d), lambda b, t: (0, 0)),
        ],
        out_specs=[
            pl.BlockSpec((None, row_tile, d), lambda b, t: (b, t, 0)),
            pl.BlockSpec((None, n_buf, d), lambda b, t: (b, 0, 0)),
        ],
        out_shape=[
            jax.ShapeDtypeStruct((bsz, t_len, d), F32),
            jax.ShapeDtypeStruct((bsz, n_buf, d), F32),
        ],
        scratch_shapes=[pltpu.VMEM((taps - 1, d), F32)],
        compiler_params=pltpu.CompilerParams(
            dimension_semantics=("arbitrary", "arbitrary"), vmem_limit_bytes=VMEM_LIMIT),
        name="conv_layer",
    )(x, buf, w_in, conv_w, w_out, ln_g, ln_b)


def _scan_rows(a_list, b_list, h_prev_list):
    rows, cols = a_list[0].shape
    groups = rows // SUBLANES
    a3 = [a.reshape(groups, SUBLANES, cols) for a in a_list]
    b3 = [b.reshape(groups, SUBLANES, cols) for b in b_list]
    sub = lax.broadcasted_iota(jnp.int32, (1, SUBLANES, 1), 1)
    s = 1
    while s < SUBLANES:
        live = sub >= s
        for n, (a, b) in enumerate(zip(a3, b3)):
            a_sh = jnp.where(live, pltpu.roll(a, s, 1), 1.0)
            b_sh = jnp.where(live, pltpu.roll(b, s, 1), 0.0)
            b3[n] = a * b_sh + b
            a3[n] = a * a_sh
        s *= 2
    carries = list(h_prev_list)
    outs = [[] for _ in a3]
    for j in range(groups):
        for n in range(len(a3)):
            h = a3[n][j] * carries[n] + b3[n][j]
            carries[n] = h[SUBLANES - 1:SUBLANES, :]
            outs[n].append(h)
    return [jnp.concatenate(o, axis=0) for o in outs]


def _lru_layer_kernel(x_ref, h0_ref, buf_ref, win_ref, cw_ref, cb_ref, wa_ref, ba_ref, wx_ref, bx_ref,
                      lam_ref, wout_ref, g_ref, b_ref,
                      y_ref, hlast_ref, nbuf_ref, carry_ref, hcarry_ref, *, stride, n_sub):
    d = x_ref.shape[1]
    rows = x_ref.shape[0] // n_sub
    taps = cw_ref.shape[0]
    n_blocks, blk, _ = wa_ref.shape
    slices = [slice(n * blk, (n + 1) * blk) for n in range(n_blocks)]
    if stride == 1:
        @pl.when(pl.program_id(1) == 0)
        def _():
            carry_ref[...] = buf_ref[...]
            hcarry_ref[...] = h0_ref[...]

    def conv_stage(s):
        x = x_ref[s * rows:(s + 1) * rows, :]
        xb = x.astype(BF16)
        ucs = []
        for sl in slices:
            u = _dot(xb, win_ref[:, sl])
            uc = u * cw_ref[taps - 1:taps, sl] + cb_ref[:, sl]
            if stride == 1:
                carry = carry_ref[:, sl]
                for k in range(1, taps):
                    uc = uc + _prev_rows(u, carry, k) * cw_ref[taps - 1 - k:taps - k, sl]
                new_buf = u[rows - (taps - 1):rows, :]
                carry_ref[:, sl] = new_buf
            else:
                ext = jnp.concatenate([buf_ref[:, sl], u], axis=0)
                for k in range(1, taps):
                    s0 = (taps - 1 - k) * stride
                    uc = uc + ext[s0:s0 + rows, :] * cw_ref[taps - 1 - k:taps - k, sl]
                new_buf = ext[rows:rows + (taps - 1) * stride, :]
            nbuf_ref[:, sl] = new_buf
            ucs.append(uc)
        return x, xb, ucs

    def recurrence_stage(ucs):
        a_list, b_list = [], []
        for n, sl in enumerate(slices):
            ucb = ucs[n].astype(BF16)
            r = jax.nn.sigmoid(_dot(ucb, wa_ref[n]) + ba_ref[:, sl])
            i = jax.nn.sigmoid(_dot(ucb, wx_ref[n]) + bx_ref[:, sl])
            nl = -lam_ref[:, sl]
            softplus = jnp.maximum(nl, 0.0) + jnp.log1p(jnp.exp(-jnp.abs(nl)))
            log_a = (-RG_C * softplus) * r
            a = jnp.exp(log_a)
            y = -jnp.tanh(log_a) * (a * a + 1.0)
            a_list.append(a)
            b_list.append(jnp.exp2(0.5 * jnp.log2(y)) * (i * ucs[n]))
        if stride == 1:
            hs = _scan_rows(a_list, b_list, [hcarry_ref[:, sl] for sl in slices])
            for n, sl in enumerate(slices):
                hcarry_ref[:, sl] = hs[n][rows - 1:rows, :]
                hlast_ref[:, sl] = hs[n][rows - 1:rows, :]
            return hs
        hs = []
        for n, sl in enumerate(slices):
            h_t = h0_ref[:, sl]
            steps = []
            for t in range(rows // stride):
                rs = slice(t * stride, (t + 1) * stride)
                h_t = a_list[n][rs, :] * h_t + b_list[n][rs, :]
                steps.append(h_t)
            hs.append(jnp.concatenate(steps, axis=0))
            hlast_ref[:, sl] = h_t
        return hs

    def output_stage(s, x, xb, hs):
        acc = jnp.zeros((rows, d), F32)
        for n, sl in enumerate(slices):
            z = _dot(xb, win_ref[:, d + n * blk:d + (n + 1) * blk])
            acc = acc + _dot((_silu(z) * hs[n]).astype(BF16), wout_ref[sl, :])
        y_ref[s * rows:(s + 1) * rows, :] = _ln_residual(x, acc, g_ref[...], b_ref[...])

    staged = conv_stage(0)
    for s in range(n_sub):
        x, xb, ucs = staged
        if s + 1 < n_sub:
            staged = conv_stage(s + 1)
        output_stage(s, x, xb, recurrence_stage(ucs))


def _lru_layer(x, h0, buf, w_in, conv_w, conv_b, w_a, b_a, w_x, b_x, lam, w_out, ln_g, ln_b, *, stride, row_tile):
    bsz, t_len, d = x.shape
    n_buf = buf.shape[1]
    n_h = h0.shape[1]
    taps = conv_w.shape[0]
    nb, blk, _ = w_a.shape
    grid = (bsz, t_len // row_tile)
    const2 = lambda b, t: (0, 0)
    n_sub = 2 if stride == 1 and row_tile % (2 * SUBLANES) == 0 else 1
    kern = functools.partial(_lru_layer_kernel, stride=stride, n_sub=n_sub)
    return pl.pallas_call(
        kern,
        grid=grid,
        in_specs=[
            pl.BlockSpec((None, row_tile, d), lambda b, t: (b, t, 0)),
            pl.BlockSpec((None, n_h, d), lambda b, t: (b, 0, 0)),
            pl.BlockSpec((None, n_buf, d), lambda b, t: (b, 0, 0)),
            pl.BlockSpec((d, 2 * d), const2),
            pl.BlockSpec((taps, d), const2),
            pl.BlockSpec((1, d), const2),
            pl.BlockSpec((nb, blk, blk), lambda b, t: (0, 0, 0)),
            pl.BlockSpec((1, d), const2),
            pl.BlockSpec((nb, blk, blk), lambda b, t: (0, 0, 0)),
            pl.BlockSpec((1, d), const2),
            pl.BlockSpec((1, d), const2),
            pl.BlockSpec((d, d), const2),
            pl.BlockSpec((1, d), const2),
            pl.BlockSpec((1, d), const2),
        ],
        out_specs=[
            pl.BlockSpec((None, row_tile, d), lambda b, t: (b, t, 0)),
            pl.BlockSpec((None, n_h, d), lambda b, t: (b, 0, 0)),
            pl.BlockSpec((None, n_buf, d), lambda b, t: (b, 0, 0)),
        ],
        out_shape=[
            jax.ShapeDtypeStruct((bsz, t_len, d), F32),
            jax.ShapeDtypeStruct((bsz, n_h, d), F32),
            jax.ShapeDtypeStruct((bsz, n_buf, d), F32),
        ],
        scratch_shapes=[pltpu.VMEM((taps - 1, d), F32), pltpu.VMEM((1, d), F32)],
        compiler_params=pltpu.CompilerParams(
            dimension_semantics=("arbitrary", "arbitrary"), vmem_limit_bytes=VMEM_LIMIT),
        name="lru_layer",
    )(x, h0, buf, w_in, conv_w, conv_b, w_a, b_a, w_x, b_x, lam, w_out, ln_g, ln_b)


D_ATT = N_HEADS * HEAD_DIM
N_ROWCOLS = 4 * N_KV * HEAD_DIM
N_WINCOLS = 2 * N_KV * HEAD_DIM
N_GATES = 3 * N_HEADS


def _nsa_proj_kernel(x_ref, wqkv_ref, wz_ref, wg_ref, q_ref, rows_ref, win_ref, sz_ref, gates_ref, *t_refs):
    xb = x_ref[...].astype(BF16)
    c0 = 0
    q_ref[...] = (_dot(xb, wqkv_ref[:, c0:c0 + D_ATT]) * (HEAD_DIM ** -0.5 * LOG2E)).astype(BF16)
    c0 += D_ATT
    rows = _dot(xb, wqkv_ref[:, c0:c0 + N_ROWCOLS])
    rows_ref[...] = rows.astype(rows_ref.dtype)
    c0 += N_ROWCOLS
    win = _dot(xb, wqkv_ref[:, c0:c0 + N_WINCOLS])
    win_ref[...] = win.astype(win_ref.dtype)
    sz_ref[...] = _silu(_dot(xb, wz_ref[...])).astype(sz_ref.dtype)
    gates_ref[...] = jax.nn.sigmoid(_dot(xb, wg_ref[...]))
    if t_refs:
        rows_t_ref, win_t_ref = t_refs
        rows_t_ref[...] = rows.T
        win_t_ref[...] = win.T


def _nsa_proj(x2, weights, *, row_tile, seq_len=None):
    n, d = x2.shape
    widths = (D_ATT, N_ROWCOLS, N_WINCOLS, D_ATT, LANES)
    kv_dtype = F32 if seq_len is None else BF16
    dtypes = (BF16, kv_dtype, kv_dtype, BF16, F32)
    out_specs = [pl.BlockSpec((row_tile, w), lambda i: (i, 0)) for w in widths]
    out_shape = [jax.ShapeDtypeStruct((n, w), dt) for w, dt in zip(widths, dtypes)]
    if seq_len is not None:
        tiles = seq_len // row_tile
        for w in (N_ROWCOLS, N_WINCOLS):
            out_specs.append(pl.BlockSpec((None, w, row_tile), lambda i: (i // tiles, 0, i % tiles)))
            out_shape.append(jax.ShapeDtypeStruct((n // seq_len, w, seq_len), F32))
    return pl.pallas_call(
        _nsa_proj_kernel,
        grid=(n // row_tile,),
        in_specs=[pl.BlockSpec((row_tile, d), lambda i: (i, 0))]
        + [pl.BlockSpec(w.shape, lambda i: (0, 0)) for w in weights],
        out_specs=out_specs,
        out_shape=out_shape,
        compiler_params=pltpu.CompilerParams(
            dimension_semantics=("arbitrary",), vmem_limit_bytes=VMEM_LIMIT),
        name="nsa_proj",
    )(x2, *weights)


def _nsa_out_kernel(x_ref, o_ref, sz_ref, w_ref, g_ref, b_ref, y_ref):
    m = (o_ref[...].astype(F32) * sz_ref[...].astype(F32)).astype(BF16)
    y_ref[...] = _ln_residual(x_ref[...], _dot(m, w_ref[...]), g_ref[...], b_ref[...])


def _nsa_out(x2, o2, sz2, w_out, ln_g, ln_b, *, row_tile):
    n, d = x2.shape
    row = lambda i: (i, 0)
    const = lambda i: (0, 0)
    return pl.pallas_call(
        _nsa_out_kernel,
        grid=(n // row_tile,),
        in_specs=[pl.BlockSpec((row_tile, d), row), pl.BlockSpec((row_tile, d), row),
                  pl.BlockSpec((row_tile, d), row), pl.BlockSpec((d, d), const),
                  pl.BlockSpec((1, d), const), pl.BlockSpec((1, d), const)],
        out_specs=pl.BlockSpec((row_tile, d), row),
        out_shape=jax.ShapeDtypeStruct((n, d), F32),
        compiler_params=pltpu.CompilerParams(
            dimension_semantics=("arbitrary",), vmem_limit_bytes=VMEM_LIMIT),
        name="nsa_out",
    )(x2, o2, sz2, w_out, ln_g, ln_b)


def _dup_half(slab, g, lo):
    rolled = pltpu.roll(slab, HEAD_DIM, 1)
    return jnp.where(lo, slab, rolled) if g == 0 else jnp.where(lo, rolled, slab)


def _select_blocks(score, nl, real, n_slc, n_sel, axis, stride):
    rank = jnp.zeros(score.shape, jnp.int32)
    for i in range(n_slc):
        j = stride * i
        cand = score[:, j:j + 1] if axis == 1 else score[j:j + 1, :]
        beats = (cand > score) | ((cand == score) & (nl > j))
        rank = rank + beats.astype(jnp.int32)
    return real & (rank < n_sel)


def _block_scores(imp2, nl, tq, n_slc, stride):
    blk = nl >> (stride - 1)
    real = ((nl & (stride - 1)) == 0) & (blk < n_slc)
    cur = tq >> (SLC_BLOCK.bit_length() - 1)
    forced = (blk == 0) | ((blk <= cur) & (blk > cur - N_LOCAL))
    score = jnp.where(forced, FORCE, jnp.where(blk <= cur, imp2, NEG_INF))
    return jnp.where(real, score, -jnp.inf), real


def _alibi_query_cols(head, n_rows):
    lane = lax.broadcasted_iota(jnp.int32, (1, LANES), 1)
    rem = jnp.full((1, LANES), _slope(head) * LOG2E, F32)
    row = jnp.zeros((1, LANES), F32)
    for i in range(3):
        piece = rem.astype(BF16).astype(F32)
        rem = rem - piece
        row = jnp.where(lane == i, float(ALIBI_RADIX) * piece, jnp.where(lane == 3 + i, piece, row))
    return jnp.broadcast_to(row, (n_rows, LANES)).astype(BF16)


def _alibi_key_cols(n_rows, period):
    r = lax.broadcasted_iota(jnp.int32, (n_rows, LANES), 0) & (period - 1)
    lane = lax.broadcasted_iota(jnp.int32, (n_rows, LANES), 1)
    hi = (r // ALIBI_RADIX).astype(F32)
    lo = (r % ALIBI_RADIX).astype(F32)
    return jnp.where(lane < 3, hi, jnp.where(lane < 6, lo, 0.0))


def _nsa_prompt_kernel(q_ref, rows_ref, win_ref, gates_ref, wc_ref, o_ref,
                       ka, vta, kwa, vwta, kcd, vct_lo, vct_hi, qa_ref, selb_ref, acc_ref):
    blk = q_ref.shape[0]
    t_len = rows_ref.shape[0]
    n_cmp = t_len // CMP_BLOCK
    n_slc = -(-t_len // SLC_BLOCK)
    n_sel = min(TOP_N, n_slc)
    n_blk = t_len // blk
    slc_per_blk = blk // SLC_BLOCK
    n_wblk = WINDOW // blk + 1
    v_rows = vta.shape[2]
    qi = pl.program_id(1)
    t0 = qi * blk
    lo = lax.broadcasted_iota(jnp.int32, (1, LANES), 1) < HEAD_DIM

    @pl.when(qi == 0)
    def _build():
        lo_t = lax.broadcasted_iota(jnp.int32, (t_len, LANES), 1) < HEAD_DIM
        lo_c = lax.broadcasted_iota(jnp.int32, (LANES, LANES), 1) < HEAD_DIM
        lo_r = lax.broadcasted_iota(jnp.int32, (LANES, 1), 0) < HEAD_DIM
        vrow = lax.broadcasted_iota(jnp.int32, (v_rows, 1), 0)
        w2 = jnp.concatenate([wc_ref[...]] * 2, axis=0)
        cw = rows_ref[:, 0:2 * LANES].astype(F32).reshape(n_slc, SLC_BLOCK, 2 * LANES) * w2[None]
        kvc = jnp.concatenate([jnp.sum(cw[:, 0:CMP_BLOCK, :], axis=1), jnp.sum(cw[:, CMP_BLOCK:, :], axis=1),
                               jnp.zeros((LANES - n_cmp, 2 * LANES), F32)], axis=0)
        key_cols = _alibi_key_cols(t_len, blk).astype(BF16)

        def values_t(v):
            vt = v.T[0:v_rows, :]
            return jnp.where(vrow < HEAD_DIM, vt, jnp.where(vrow == HEAD_DIM, 1.0, 0.0)).astype(BF16)

        for g in range(N_KV):
            kd = _dup_half(rows_ref[:, 2 * LANES:3 * LANES].astype(F32), g, lo_t).astype(BF16)
            ka[g] = jnp.concatenate([kd, key_cols], axis=1)
            vt = values_t(_dup_half(rows_ref[:, 3 * LANES:4 * LANES].astype(F32), g, lo_t))
            kd = _dup_half(win_ref[:, 0:LANES].astype(F32), g, lo_t).astype(BF16)
            kwa[g] = jnp.concatenate([kd, key_cols], axis=1)
            vwt = values_t(_dup_half(win_ref[:, LANES:2 * LANES].astype(F32), g, lo_t))
            for c in range(n_blk):
                vta[g, c] = vt[:, c * blk:(c + 1) * blk]
                vwta[g, c] = vwt[:, c * blk:(c + 1) * blk]
            kcd[g] = _dup_half(kvc[:, 0:LANES], g, lo_c).astype(BF16)
            vct = _dup_half(kvc[:, LANES:2 * LANES], g, lo_c).T
            vct_lo[g] = jnp.where(lo_r, vct, 0.0).astype(BF16)
            vct_hi[g] = jnp.where(lo_r, 0.0, vct).astype(BF16)

    tq = t0 + lax.broadcasted_iota(jnp.int32, (1, blk), 1)
    row_c = lax.broadcasted_iota(jnp.int32, (LANES, 1), 0)
    cmp_blk = jnp.where(row_c < n_slc, 2 * row_c, 2 * (row_c - n_slc) + 1)
    cmp_end = cmp_blk * CMP_BLOCK + (CMP_BLOCK - 1)
    valid_c = (cmp_end <= tq) & (row_c < n_cmp)
    dist_c = (tq - cmp_end).astype(F32)
    slc_r = lax.broadcasted_iota(jnp.int32, (n_slc, 1), 0)

    lr = lax.broadcasted_iota(jnp.int32, (1, blk), 1) - lax.broadcasted_iota(jnp.int32, (blk, 1), 0)
    causal_bias = jnp.where(lr >= 0, 0.0, NEG_INF)

    w_start = pl.multiple_of(jnp.maximum(t0 - WINDOW, 0), blk)
    w_blk = w_start // blk
    w_shift, w_bias = [], []
    for i in range(n_wblk):
        shift = t0 - w_start - i * blk
        dist = lr + shift
        w_shift.append(shift.astype(F32))
        w_bias.append(jnp.where((dist >= 0) & (dist < WINDOW), 0.0, NEG_INF))

    gates_t = gates_ref[...].T

    for hh in range(N_HEADS):
        q2 = q_ref[:, (hh // 2) * LANES:(hh // 2 + 1) * LANES]
        zero = jnp.zeros_like(q2)
        qm = jnp.where(lo, q2, zero) if hh % 2 == 0 else jnp.where(lo, zero, q2)
        qa_ref[hh] = jnp.concatenate([qm, _alibi_query_cols(hh, blk)], axis=1)

    o_cmp = []
    for g in range(N_KV):
        heads = range(g * HPG, (g + 1) * HPG)
        scores = [_dot_nt(kcd[g], qa_ref[hh, :, 0:LANES]) for hh in heads]
        imp = jnp.zeros((LANES, blk), F32)
        probs = []
        for hh, s in zip(heads, scores):
            s = jnp.where(valid_c, s - (_slope(hh) * LOG2E) * dist_c, NEG_INF)
            e = jnp.exp2(s - jnp.max(s, axis=0, keepdims=True))
            p = jnp.where(valid_c, e, 0.0) / jnp.sum(e, axis=0, keepdims=True)
            imp = imp + p
            probs.append(p.astype(BF16))
        for j in range(HPG // 2):
            o_cmp.append(_dot(vct_lo[g], probs[2 * j]) + _dot(vct_hi[g], probs[2 * j + 1]))

        imp2 = imp[0:n_slc, :] + imp[n_slc:2 * n_slc, :]
        score, real = _block_scores(imp2, slc_r, tq, n_slc, stride=1)
        sel = _select_blocks(score, slc_r, real, n_slc, n_sel, axis=0, stride=1)
        sel_bias = jnp.where(sel, 0.0, NEG_INF)
        for c in range(n_blk):
            rows = sel_bias[c * slc_per_blk:(c + 1) * slc_per_blk, :]
            selb_ref[g, c] = jnp.concatenate([rows] * (SUBLANES // slc_per_blk), axis=0)

    acc_ref[...] = jnp.zeros(acc_ref.shape, F32)

    def key_block(c, ms, extra_bias):
        off = pl.multiple_of(c * blk, blk)
        shift = (t0 - off).astype(F32)
        keys, biases = [], []
        for g in range(N_KV):
            keys.append(ka[g, pl.ds(off, blk), :])
            sel_rows = selb_ref[g, c]
            bias = jnp.concatenate([jnp.broadcast_to(sel_rows[m:m + 1, :], (SLC_BLOCK, blk))
                                    for m in range(slc_per_blk)], axis=0)
            biases.append(bias if extra_bias is None else bias + extra_bias)
        scores, new_ms, probs = [], [], []
        for hh in range(N_HEADS):
            s = _dot_nt(keys[hh // HPG], qa_ref[hh]) + biases[hh // HPG]
            c_h = (_slope(hh) * LOG2E) * shift
            new_ms.append(jnp.maximum(ms[hh], jnp.max(s, axis=0, keepdims=True) - c_h))
            scores.append(s)
        for hh in range(N_HEADS):
            c_h = (_slope(hh) * LOG2E) * shift
            probs.append(jnp.exp2((scores[hh] - (new_ms[hh] + c_h)).astype(BF16)))
        for hh in range(N_HEADS):
            acc_ref[hh] = acc_ref[hh] * jnp.exp2(ms[hh] - new_ms[hh]) + _dot(vta[hh // HPG, c], probs[hh])
        return tuple(new_ms)

    ms = lax.fori_loop(0, qi, lambda c, ms: key_block(c, ms, None),
                       (jnp.full((1, blk), NEG_INF, F32),) * N_HEADS)
    key_block(qi, ms, causal_bias)

    k_win = [[kwa[g, pl.ds(pl.multiple_of(w_start + i * blk, blk), blk), :] for i in range(n_wblk)]
             for g in range(N_KV)]
    w_scores, w_max = [], []
    for hh in range(N_HEADS):
        slope = _slope(hh) * LOG2E
        tiles, m_w = [], None
        for i in range(n_wblk):
            s = _dot_nt(k_win[hh // HPG][i], qa_ref[hh]) + w_bias[i]
            m_i = jnp.max(s, axis=0, keepdims=True) - slope * w_shift[i]
            m_w = m_i if m_w is None else jnp.maximum(m_w, m_i)
            tiles.append(s)
        w_scores.append(tiles)
        w_max.append(m_w)
    o_wins = []
    for hh in range(N_HEADS):
        slope = _slope(hh) * LOG2E
        acc_w = jnp.zeros((v_rows, blk), F32)
        for i in range(n_wblk):
            p = jnp.exp2((w_scores[hh][i] - (w_max[hh] + slope * w_shift[i])).astype(BF16))
            acc_w = acc_w + _dot(vwta[hh // HPG, w_blk + i], p)
        o_wins.append(acc_w[0:HEAD_DIM, :] / acc_w[HEAD_DIM:HEAD_DIM + 1, :])

    for g in range(N_KV):
        for j in range(HPG // 2):
            outs = []
            for hh in (g * HPG + 2 * j, g * HPG + 2 * j + 1):
                o_win = o_wins[hh]
                acc = acc_ref[hh]
                o_sel = acc[0:HEAD_DIM, :] / acc[HEAD_DIM:HEAD_DIM + 1, :]
                half = (hh % 2) * HEAD_DIM
                gate = lambda c: gates_t[c * N_HEADS + hh:c * N_HEADS + hh + 1, :]
                o_c = o_cmp[g * (HPG // 2) + j][half:half + HEAD_DIM, :]
                outs.append(gate(0) * o_c + gate(1) * o_sel + gate(2) * o_win)
            c0 = g * HPG * HEAD_DIM + j * LANES
            o_ref[:, c0:c0 + LANES] = jnp.concatenate(outs, axis=0).T.astype(o_ref.dtype)


def _block_expand_matrix(n_rows, n_keys):
    n = jnp.arange(n_rows, dtype=jnp.int32)[:, None]
    p = jnp.arange(n_keys, dtype=jnp.int32)[None, :]
    return (n == 2 * (p // SLC_BLOCK)).astype(BF16)


def _nsa_prompt_attention(q, rows, win, gates, wc, *, q_tile=NSA_BLOCK):
    bsz, t_len, _ = q.shape
    assert WINDOW % q_tile == 0 and q_tile % (2 * LANES) == 0 and t_len >= WINDOW + q_tile
    assert t_len // CMP_BLOCK <= LANES and t_len % SLC_BLOCK == 0 and SUBLANES % (q_tile // SLC_BLOCK) == 0
    v_rows = HEAD_DIM + BF16_SUBLANES
    k_scratch = pltpu.VMEM((N_KV, t_len, 2 * LANES), BF16)
    v_scratch = pltpu.VMEM((N_KV, t_len // q_tile, v_rows, q_tile), BF16)
    c_scratch = pltpu.VMEM((N_KV, LANES, LANES), BF16)
    return pl.pallas_call(
        _nsa_prompt_kernel,
        grid=(bsz, t_len // q_tile),
        in_specs=[
            pl.BlockSpec((None, q_tile, D_ATT), lambda b, i: (b, i, 0)),
            pl.BlockSpec((None, t_len, N_ROWCOLS), lambda b, i: (b, 0, 0)),
            pl.BlockSpec((None, t_len, N_WINCOLS), lambda b, i: (b, 0, 0)),
            pl.BlockSpec((None, q_tile, LANES), lambda b, i: (b, i, 0)),
            pl.BlockSpec((CMP_BLOCK, 2 * LANES), lambda b, i: (0, 0)),
        ],
        out_specs=pl.BlockSpec((None, q_tile, D_ATT), lambda b, i: (b, i, 0)),
        out_shape=jax.ShapeDtypeStruct((bsz, t_len, D_ATT), BF16),
        scratch_shapes=[k_scratch, v_scratch, k_scratch, v_scratch,
                        c_scratch, c_scratch, c_scratch,
                        pltpu.VMEM((N_HEADS, q_tile, 2 * LANES), BF16),
                        pltpu.VMEM((N_KV, t_len // q_tile, SUBLANES, q_tile), F32),
                        pltpu.VMEM((N_HEADS, v_rows, q_tile), F32)],
        compiler_params=pltpu.CompilerParams(
            dimension_semantics=("arbitrary", "arbitrary"), vmem_limit_bytes=VMEM_LIMIT),
        name="nsa_prompt_attention",
    )(q, rows, win, gates, wc)


def _nsa_sample_kernel(pt_ref, *refs, pages_per_step, n_steps, n_new):
    page_refs = refs[:pages_per_step]
    (qbd_ref, new_ref, winbuf_ref, gates_ref, wc_ref, e_ref, o_ref, s_ref, vst_ref, kcs, vcs) = refs[pages_per_step:]
    del pt_ref
    step = pl.program_id(1)
    keys_per_step = pages_per_step * PAGE_SIZE
    p_len = n_steps * keys_per_step
    n_keys = p_len + LANES
    n_cmp = p_len // CMP_BLOCK
    n_slc = -(-(p_len + n_new) // SLC_BLOCK)
    n_sel = min(TOP_N, n_slc)
    cmp_per_step = keys_per_step // CMP_BLOCK
    n_rows = N_KV * n_new * HPG
    gd = N_KV * HEAD_DIM

    def slab(r, kind):
        return r[kind].reshape(gd, r.shape[-1])

    cm = jnp.concatenate([jnp.concatenate([slab(r, 0).T, slab(r, 1).T], axis=1) for r in page_refs], axis=0)
    kvc = jnp.sum(cm.reshape(cmp_per_step, CMP_BLOCK, 2 * LANES) * wc_ref[...][None], axis=1)
    c_off = pl.multiple_of(step * cmp_per_step, cmp_per_step)
    kcs[pl.ds(c_off, cmp_per_step), :] = kvc[:, 0:LANES].astype(BF16)
    vcs[pl.ds(c_off, cmp_per_step), :] = kvc[:, LANES:2 * LANES].astype(BF16)
    s_ref[step] = _dot(qbd_ref[...], jnp.concatenate([slab(r, 2).astype(BF16) for r in page_refs], axis=1))
    vst_ref[step] = jnp.concatenate([slab(r, 3).astype(BF16) for r in page_refs], axis=1)

    @pl.when(step == n_steps - 1)
    def _attend():
        n_wb = winbuf_ref.shape[-1]
        w_keys = n_wb + LANES
        qbd = qbd_ref[...]
        r = lax.broadcasted_iota(jnp.int32, (n_rows, 1), 0)
        head = (r // (n_new * HPG)) * HPG + (r % HPG)
        slope = jnp.exp2((head + 1).astype(F32) * (-8.0 / N_HEADS)) * LOG2E
        tq = p_len + (r // HPG) % n_new

        def attend(s, values_t, dist, bias):
            s = s - slope * dist.astype(F32) + bias
            e = jnp.exp2(s - jnp.max(s, axis=-1, keepdims=True))
            eb = e.astype(BF16)
            o = sum(_dot_nt(eb[:, ks], vt) for ks, vt in values_t)
            return o / jnp.sum(e, axis=-1, keepdims=True)

        nl = lax.broadcasted_iota(jnp.int32, (n_rows, n_cmp), 1)
        cmp_end = nl * CMP_BLOCK + (CMP_BLOCK - 1)
        valid_c = cmp_end <= tq
        s = _dot_nt(qbd, kcs[...]) - slope * (tq - cmp_end).astype(F32)
        s = jnp.where(valid_c, s, NEG_INF)
        e = jnp.exp2(s - jnp.max(s, axis=-1, keepdims=True))
        p = jnp.where(valid_c, e, 0.0) / jnp.sum(e, axis=-1, keepdims=True)
        o_cmp = _dot(p.astype(BF16), vcs[...])

        n_gt = N_KV * n_new
        imp = jnp.sum(p.reshape(n_gt, HPG, n_cmp), axis=1)
        imp2 = imp + pltpu.roll(imp, n_cmp - 1, 1)
        w_sel = -(-2 * n_slc // LANES) * LANES
        imp2 = jnp.concatenate([imp2, jnp.zeros((n_gt, w_sel - n_cmp), F32)], axis=1)
        nl8 = lax.broadcasted_iota(jnp.int32, (n_gt, w_sel), 1)
        r8 = lax.broadcasted_iota(jnp.int32, (n_gt, 1), 0)
        score, real = _block_scores(imp2, nl8, p_len + r8 % n_new, n_slc, stride=2)
        sel = _select_blocks(score, nl8, real, n_slc, n_sel, axis=1, stride=2)
        selb = jnp.where(sel, 1.0, 0.0)
        selb = jnp.broadcast_to(selb[:, None, :], (n_gt, HPG, w_sel)).reshape(n_rows, w_sel).astype(BF16)
        keys_per_tile = e_ref.shape[1]
        hit = [_dot(selb[:, j * LANES:(j + 1) * LANES], e_ref[...]) for j in range(p_len // keys_per_tile)]
        hit.append(jnp.broadcast_to(selb[:, 2 * (p_len // SLC_BLOCK):2 * (p_len // SLC_BLOCK) + 1].astype(F32),
                                    (n_rows, LANES)))
        hit = jnp.concatenate(hit, axis=1)
        dist = tq - lax.broadcasted_iota(jnp.int32, (n_rows, n_keys), 1)
        new_k, new_v, new_kw, new_vw = (new_ref[i] for i in range(4))
        s_sel = jnp.concatenate([s_ref[i] for i in range(n_steps)] + [_dot(qbd, new_k)], axis=1)
        values_t = [(slice(i * keys_per_step, (i + 1) * keys_per_step), vst_ref[i]) for i in range(n_steps)]
        o_sel = attend(s_sel, values_t + [(slice(p_len, n_keys), new_v)], dist,
                       jnp.where((hit > 0.5) & (dist >= 0), 0.0, NEG_INF))

        dist_w = tq - (p_len - n_wb) - lax.broadcasted_iota(jnp.int32, (n_rows, w_keys), 1)
        s_win = jnp.concatenate([_dot(qbd, slab(winbuf_ref, 0).astype(BF16)), _dot(qbd, new_kw)], axis=1)
        o_win = attend(s_win, [(slice(0, n_wb), slab(winbuf_ref, 1).astype(BF16)), (slice(n_wb, w_keys), new_vw)],
                       dist_w, jnp.where((dist_w >= 0) & (dist_w < WINDOW), 0.0, NEG_INF))

        o_ref[...] = (gates_ref[:, 0:1] * o_cmp + gates_ref[:, 1:2] * o_sel + gates_ref[:, 2:3] * o_win)


def _nsa_sample_attention(page_table, cache_t, qbd, new_t, winbuf_t, gates, wc, *, n_new,
                          pages_per_step=PAGES_PER_STEP):
    bsz, n_pages = page_table.shape
    n_steps = n_pages // pages_per_step
    keys_per_step = pages_per_step * PAGE_SIZE
    p_len = n_pages * PAGE_SIZE
    n_rows = qbd.shape[1]
    n_wb = winbuf_t.shape[-1]
    n_cmp = p_len // CMP_BLOCK
    gd = N_KV * HEAD_DIM
    keys_per_tile = (LANES // 2) * SLC_BLOCK
    assert p_len % keys_per_tile == 0 and n_new <= SLC_BLOCK
    e_mat = _block_expand_matrix(LANES, keys_per_tile)
    kern = functools.partial(_nsa_sample_kernel, pages_per_step=pages_per_step, n_steps=n_steps, n_new=n_new)

    def page_spec(k):
        return pl.BlockSpec((None, 4, N_KV, HEAD_DIM, PAGE_SIZE),
                            lambda b, s, pt: (pt[b, s * pages_per_step + k], 0, 0, 0, 0))

    per_b = lambda b, s, pt: (b, 0, 0)
    const = lambda b, s, pt: (0, 0)
    grid_spec = pltpu.PrefetchScalarGridSpec(
        num_scalar_prefetch=1,
        grid=(bsz, n_steps),
        in_specs=[page_spec(k) for k in range(pages_per_step)] + [
            pl.BlockSpec((None, n_rows, LANES), per_b),
            pl.BlockSpec((None, 4, gd, LANES), lambda b, s, pt: (b, 0, 0, 0)),
            pl.BlockSpec((None, 2, N_KV, HEAD_DIM, n_wb), lambda b, s, pt: (b, 0, 0, 0, 0)),
            pl.BlockSpec((None, n_rows, LANES), per_b),
            pl.BlockSpec((CMP_BLOCK, 2 * LANES), const),
            pl.BlockSpec((LANES, keys_per_tile), const),
        ],
        out_specs=pl.BlockSpec((None, n_rows, LANES), per_b),
        scratch_shapes=[
            pltpu.VMEM((n_steps, n_rows, keys_per_step), F32),
            pltpu.VMEM((n_steps, gd, keys_per_step), BF16),
            pltpu.VMEM((n_cmp, LANES), BF16), pltpu.VMEM((n_cmp, LANES), BF16),
        ],
    )
    return pl.pallas_call(
        kern,
        grid_spec=grid_spec,
        out_shape=jax.ShapeDtypeStruct((bsz, n_rows, LANES), F32),
        compiler_params=pltpu.CompilerParams(
            dimension_semantics=("arbitrary", "arbitrary"), vmem_limit_bytes=VMEM_LIMIT),
        name="nsa_sample_attention",
    )(page_table, *([cache_t] * pages_per_step), qbd, new_t, winbuf_t, gates, wc, e_mat)


def _nsa_weights(w_in):
    c_g = D_ATT + N_ROWCOLS + N_WINCOLS
    c_z = c_g + N_GATES
    w_gates = jnp.pad(w_in[:, c_g:c_z], ((0, 0), (0, LANES - N_GATES)))
    return w_in[:, :c_g].astype(BF16), w_in[:, c_z:].astype(BF16), w_gates.astype(BF16)


def _cmp_weight_tile(w_cmp):
    return w_cmp.transpose(1, 0, 2, 3).reshape(CMP_BLOCK, 2 * N_KV * HEAD_DIM)


def _nsa_layer_prompt(x, w_all, wc, w_out, ln_g, ln_b):
    bsz, t_len, d = x.shape
    x2 = x.reshape(bsz * t_len, d)
    q, rows, win, sz, gates, rows_t, win_t = _nsa_proj(x2, w_all, row_tile=ROW_TILE, seq_len=t_len)
    o = _nsa_prompt_attention(q.reshape(bsz, t_len, D_ATT), rows.reshape(bsz, t_len, N_ROWCOLS),
                              win.reshape(bsz, t_len, N_WINCOLS), gates.reshape(bsz, t_len, LANES), wc)
    y = _nsa_out(x2, o.reshape(bsz * t_len, D_ATT), sz, w_out, ln_g, ln_b, row_tile=ROW_TILE)
    keep = min(WINDOW, t_len)
    rows_out = rows_t.reshape(bsz, 4, N_KV, HEAD_DIM, t_len).transpose(0, 4, 1, 2, 3)
    win_out = win_t[:, :, t_len - keep:].reshape(bsz, 2, N_KV, HEAD_DIM, keep).transpose(0, 4, 1, 2, 3)
    return y.reshape(bsz, t_len, d), rows_out, win_out


def _nsa_layer_sample(x, cache, win_buf, page_table, w_all, wc, w_out, ln_g, ln_b):
    bsz, n_new, d = x.shape
    x2 = x.reshape(bsz * n_new, d)
    q, rows, win, sz, gates = _nsa_proj(x2, w_all, row_tile=bsz * n_new)
    q5 = q.reshape(bsz, n_new, N_KV, HPG, HEAD_DIM).transpose(0, 2, 1, 3, 4)
    eye = jnp.eye(N_KV, dtype=q.dtype)
    qbd = (q5[:, :, :, :, None, :] * eye[None, :, None, None, :, None]).reshape(
        bsz, N_KV * n_new * HPG, N_KV * HEAD_DIM)
    g5 = gates[:, :N_GATES].reshape(bsz, n_new, 3, N_KV, HPG).transpose(0, 3, 1, 4, 2)
    g_rows = jnp.pad(g5.reshape(bsz, N_KV * n_new * HPG, 3), ((0, 0), (0, 0), (0, LANES - 3)))
    rows3 = rows.reshape(bsz, n_new, N_ROWCOLS)
    win3 = win.reshape(bsz, n_new, N_WINCOLS)
    n_wb = win_buf.shape[1]
    gd = N_KV * HEAD_DIM
    new_t = jnp.stack([rows3[:, :, 2 * gd:3 * gd], rows3[:, :, 3 * gd:4 * gd],
                       win3[:, :, 0:gd], win3[:, :, gd:2 * gd]], axis=1).transpose(0, 1, 3, 2)
    new_t = jnp.pad(new_t, ((0, 0), (0, 0), (0, 0), (0, LANES - n_new))).astype(BF16)
    o_rows = _nsa_sample_attention(
        page_table, cache.transpose(0, 2, 3, 4, 1), qbd, new_t, win_buf.transpose(0, 2, 3, 4, 1),
        g_rows, wc, n_new=n_new)
    o6 = o_rows.reshape(bsz, N_KV, n_new, HPG, N_KV, HEAD_DIM)
    o = jnp.stack([o6[:, g, :, :, g, :] for g in range(N_KV)], axis=2).reshape(bsz * n_new, D_ATT)
    y = _nsa_out(x2, o, sz, w_out, ln_g, ln_b, row_tile=bsz * n_new)
    all_win = jnp.concatenate([win_buf, win3.reshape(bsz, n_new, 2, N_KV, HEAD_DIM)], axis=1)
    return (y.reshape(bsz, n_new, d), rows3.reshape(bsz, n_new, 4, N_KV, HEAD_DIM),
            all_win[:, all_win.shape[1] - n_wb:])


def _time_major(a):
    bsz, k, d = a.shape
    return a.transpose(1, 0, 2).reshape(1, k * bsz, d)


def _batch_major(a, bsz):
    _, kb, d = a.shape
    return a.reshape(kb // bsz, bsz, d).transpose(1, 0, 2)


def kernel(x_prompt, x_sample, cache_nsa_kv, cache_nsa_win, state_conv_a, state_lru_h, state_lru_conv, page_table,
           ln_g, ln_b, a_w_in, a_conv_w, a_w_out, b_w_in, b_w_cmp, b_w_out,
           c_w_in, c_conv_w, c_conv_b, c_w_a, c_b_a, c_w_x, c_b_x, c_lam, c_w_out):
    bp, t_len, d = x_prompt.shape
    bs, n_new, _ = x_sample.shape
    yp = x_prompt
    ys = _time_major(x_sample)
    conv_a_p, conv_a_s = [], []
    rows_p, rows_s, win_p, win_s = [], [], [], []
    h_p, h_s, cc_p, cc_s = [], [], [], []
    row2 = lambda v: v.reshape(1, -1)
    a_w_in_b, a_w_out_b = a_w_in.astype(BF16), a_w_out.astype(BF16)
    for i in range(DEPTH):
        kind, j = i % 3, i // 3
        g, b = row2(ln_g[i]), row2(ln_b[i])
        if kind == 0:
            taps = a_conv_w.shape[1]
            yp, sp = _conv_layer(yp, jnp.zeros((bp, taps - 1, d), F32), a_w_in_b, a_conv_w, a_w_out_b, g, b,
                                 layer=j, stride=1, row_tile=2 * ROW_TILE)
            ys, ss = _conv_layer(ys, _time_major(state_conv_a[j]), a_w_in_b, a_conv_w, a_w_out_b, g, b,
                                 layer=j, stride=bs, row_tile=n_new * bs)
            conv_a_p.append(sp)
            conv_a_s.append(_batch_major(ss, bs))
        elif kind == 1:
            w_all = _nsa_weights(b_w_in[j])
            wc = _cmp_weight_tile(b_w_cmp[j])
            w_out = b_w_out[j].astype(BF16)
            yp, rp, wp = _nsa_layer_prompt(yp, w_all, wc, w_out, g, b)
            ys_b, rs, ws = _nsa_layer_sample(_batch_major(ys, bs), cache_nsa_kv[j], cache_nsa_win[j], page_table,
                                             w_all, wc, w_out, g, b)
            ys = _time_major(ys_b)
            rows_p.append(rp)
            rows_s.append(rs)
            win_p.append(wp)
            win_s.append(ws)
        else:
            w_in, w_out = c_w_in[j].astype(BF16), c_w_out[j].astype(BF16)
            w_a, w_x = c_w_a[j].astype(BF16), c_w_x[j].astype(BF16)
            taps = c_conv_w.shape[1]
            args = (w_in, c_conv_w[j], row2(c_conv_b[j]), w_a, row2(c_b_a[j]), w_x, row2(c_b_x[j]),
                    row2(c_lam[j]), w_out, g, b)
            yp, hp, cp = _lru_layer(yp, jnp.zeros((bp, 1, d), F32), jnp.zeros((bp, taps - 1, d), F32), *args,
                                    stride=1, row_tile=ROW_TILE)
            ys, hs, cs = _lru_layer(ys, state_lru_h[j].reshape(1, bs, d), _time_major(state_lru_conv[j]), *args,
                                    stride=bs, row_tile=n_new * bs)
            h_p.append(hp.reshape(bp, d))
            h_s.append(hs.reshape(bs, d))
            cc_p.append(cp)
            cc_s.append(_batch_major(cs, bs))
    return (yp, _batch_major(ys, bs), jnp.stack(conv_a_p), jnp.stack(conv_a_s), jnp.stack(rows_p), jnp.stack(rows_s),
            jnp.stack(win_p), jnp.stack(win_s), jnp.stack(h_p), jnp.stack(h_s), jnp.stack(cc_p), jnp.stack(cc_s))
```

```python
import functools

import jax
import jax.numpy as jnp
from jax import lax
from jax.experimental import pallas as pl
from jax.experimental.pallas import tpu as pltpu

F32 = jnp.float32
BF16 = jnp.bfloat16

DEPTH = 4
N_HEADS = 16
N_KV = 2
HPG = N_HEADS // N_KV
HEAD_DIM = 64
CMP_BLOCK = 32
SLC_BLOCK = 64
TOP_N = 16
N_LOCAL = 2
WINDOW = 512
PAGE_SIZE = 128
NEG_INF = -1e30
FORCE = 1e9
RG_C = 8.0
N_RG_BLOCKS = 4
ALPHA = (2 * DEPTH) ** 0.25
LN_EPS = 1e-5
LOG2E = 1.4426950408889634

LANES = 128
SUBLANES = 8
BF16_SUBLANES = 16
VMEM_LIMIT = 56 * 1024 * 1024

ROW_TILE = 512
NSA_BLOCK = 256
PAGES_PER_STEP = 64
ALIBI_RADIX = 16


def _dot(a, b):
    return jnp.dot(a, b, preferred_element_type=F32)


def _dot_nt(a, b):
    return lax.dot_general(a, b, (((1,), (1,)), ((), ())), preferred_element_type=F32)


def _silu(z):
    return z * jax.nn.sigmoid(z)


def _ln_residual(x, y, g, b):
    r = ALPHA * x + y
    mu = jnp.mean(r, axis=-1, keepdims=True)
    c = r - mu
    var = jnp.mean(c * c, axis=-1, keepdims=True)
    return c * lax.rsqrt(var + LN_EPS) * g + b


def _slope(head):
    return 2.0 ** (-8.0 * (head + 1) / N_HEADS)


def _prev_rows(u, carry, k):
    n_carry = carry.shape[0]
    rolled = pltpu.roll(u, k, 0)
    row = lax.broadcasted_iota(jnp.int32, (SUBLANES, 1), 0)
    head = rolled[0:SUBLANES, :]
    for i in range(k):
        head = jnp.where(row == i, carry[n_carry - k + i:n_carry - k + i + 1, :], head)
    return jnp.concatenate([head, rolled[SUBLANES:, :]], axis=0)


def _conv_layer_kernel(x_ref, buf_ref, win_ref, cw_ref, wout_ref, g_ref, b_ref,
                       y_ref, nbuf_ref, carry_ref, *, stride, col_chunk, n_sub):
    d = x_ref.shape[1]
    rows = x_ref.shape[0] // n_sub
    taps = cw_ref.shape[0]
    n_chunks = d // col_chunk
    if stride == 1:
        @pl.when(pl.program_id(1) == 0)
        def _():
            carry_ref[...] = buf_ref[...]

    def mixer_chunk(xb, acc, j):
        lo = j * col_chunk
        sl = slice(lo, lo + col_chunk)
        h, bg, cg, z = [_dot(xb, win_ref[:, part * d + lo:part * d + lo + col_chunk]) for part in range(4)]
        u = cg * h
        conv = u * cw_ref[taps - 1:taps, sl]
        if stride == 1:
            carry = carry_ref[:, sl]
            for k in range(1, taps):
                conv = conv + _prev_rows(u, carry, k) * cw_ref[taps - 1 - k:taps - k, sl]
            new_buf = u[rows - (taps - 1):rows, :]
            carry_ref[:, sl] = new_buf
        else:
            ext = jnp.concatenate([buf_ref[:, sl], u], axis=0)
            for k in range(1, taps):
                s0 = (taps - 1 - k) * stride
                conv = conv + ext[s0:s0 + rows, :] * cw_ref[taps - 1 - k:taps - k, sl]
            new_buf = ext[rows:rows + (taps - 1) * stride, :]
        nbuf_ref[:, sl] = new_buf
        m = _silu(z) * bg * conv
        return acc + _dot(m.astype(BF16), wout_ref[sl, :])

    pending = None
    for s in range(n_sub):
        x = x_ref[s * rows:(s + 1) * rows, :]
        xb = x.astype(BF16)
        acc = jnp.zeros((rows, d), F32)
        for j in range(n_chunks):
            acc = mixer_chunk(xb, acc, j)
            if j == 0 and pending is not None:
                ps, px, pacc = pending
                y_ref[ps * rows:(ps + 1) * rows, :] = _ln_residual(px, pacc, g_ref[...], b_ref[...])
        pending = (s, x, acc)
    ps, px, pacc = pending
    y_ref[ps * rows:(ps + 1) * rows, :] = _ln_residual(px, pacc, g_ref[...], b_ref[...])


def _conv_layer(x, buf, w_in, conv_w, w_out, ln_g, ln_b, *, layer, stride, row_tile):
    bsz, t_len, d = x.shape
    n_buf = buf.shape[1]
    taps = conv_w.shape[1]
    grid = (bsz, t_len // row_tile)
    n_sub = 2 if stride == 1 and row_tile % (2 * SUBLANES) == 0 else 1
    kern = functools.partial(_conv_layer_kernel, stride=stride, col_chunk=512, n_sub=n_sub)
    return pl.pallas_call(
        kern,
        grid=grid,
        in_specs=[
            pl.BlockSpec((None, row_tile, d), lambda b, t: (b, t, 0)),
            pl.BlockSpec((None, n_buf, d), lambda b, t: (b, 0, 0)),
            pl.BlockSpec((None, d, 4 * d), lambda b, t: (layer, 0, 0)),
            pl.BlockSpec((None, taps, d), lambda b, t: (layer, 0, 0)),
            pl.BlockSpec((None, d, d), lambda b, t: (layer, 0, 0)),
            pl.BlockSpec((1, d), lambda b, t: (0, 0)),
            pl.BlockSpec((1, d), lambda b, t: (0, 0)),
        ],
        out_specs=[
            pl.BlockSpec((None, row_tile, d), lambda b, t: (b, t, 0)),
            pl.BlockSpec((None, n_buf, d), lambda b, t: (b, 0, 0)),
        ],
        out_shape=[
            jax.ShapeDtypeStruct((bsz, t_len, d), F32),
            jax.ShapeDtypeStruct((bsz, n_buf, d), F32),
        ],
        scratch_shapes=[pltpu.VMEM((taps - 1, d), F32)],
        compiler_params=pltpu.CompilerParams(
            dimension_semantics=("arbitrary", "arbitrary"), vmem_limit_bytes=VMEM_LIMIT),
        name="conv_layer",
    )(x, buf, w_in, conv_w, w_out, ln_g, ln_b)


def _scan_rows(a_list, b_list, h_prev_list):
    rows, cols = a_list[0].shape
    groups = rows // SUBLANES
    a3 = [a.reshape(groups, SUBLANES, cols) for a in a_list]
    b3 = [b.reshape(groups, SUBLANES, cols) for b in b_list]
    sub = lax.broadcasted_iota(jnp.int32, (1, SUBLANES, 1), 1)
    s = 1
    while s < SUBLANES:
        live = sub >= s
        for n, (a, b) in enumerate(zip(a3, b3)):
            a_sh = jnp.where(live, pltpu.roll(a, s, 1), 1.0)
            b_sh = jnp.where(live, pltpu.roll(b, s, 1), 0.0)
            b3[n] = a * b_sh + b
            a3[n] = a * a_sh
        s *= 2
    carries = list(h_prev_list)
    outs = [[] for _ in a3]
    for j in range(groups):
        for n in range(len(a3)):
            h = a3[n][j] * carries[n] + b3[n][j]
            carries[n] = h[SUBLANES - 1:SUBLANES, :]
            outs[n].append(h)
    return [jnp.concatenate(o, axis=0) for o in outs]


def _lru_layer_kernel(x_ref, h0_ref, buf_ref, win_ref, cw_ref, cb_ref, wa_ref, ba_ref, wx_ref, bx_ref,
                      lam_ref, wout_ref, g_ref, b_ref,
                      y_ref, hlast_ref, nbuf_ref, carry_ref, hcarry_ref, *, stride, n_sub):
    d = x_ref.shape[1]
    rows = x_ref.shape[0] // n_sub
    taps = cw_ref.shape[0]
    n_blocks, blk, _ = wa_ref.shape
    slices = [slice(n * blk, (n + 1) * blk) for n in range(n_blocks)]
    if stride == 1:
        @pl.when(pl.program_id(1) == 0)
        def _():
            carry_ref[...] = buf_ref[...]
            hcarry_ref[...] = h0_ref[...]

    def conv_stage(s):
        x = x_ref[s * rows:(s + 1) * rows, :]
        xb = x.astype(BF16)
        ucs = []
        for sl in slices:
            u = _dot(xb, win_ref[:, sl])
            uc = u * cw_ref[taps - 1:taps, sl] + cb_ref[:, sl]
            if stride == 1:
                carry = carry_ref[:, sl]
                for k in range(1, taps):
                    uc = uc + _prev_rows(u, carry, k) * cw_ref[taps - 1 - k:taps - k, sl]
                new_buf = u[rows - (taps - 1):rows, :]
                carry_ref[:, sl] = new_buf
            else:
                ext = jnp.concatenate([buf_ref[:, sl], u], axis=0)
                for k in range(1, taps):
                    s0 = (taps - 1 - k) * stride
                    uc = uc + ext[s0:s0 + rows, :] * cw_ref[taps - 1 - k:taps - k, sl]
                new_buf = ext[rows:rows + (taps - 1) * stride, :]
            nbuf_ref[:, sl] = new_buf
            ucs.append(uc)
        return x, xb, ucs

    def recurrence_stage(ucs):
        a_list, b_list = [], []
        for n, sl in enumerate(slices):
            ucb = ucs[n].astype(BF16)
            r = jax.nn.sigmoid(_dot(ucb, wa_ref[n]) + ba_ref[:, sl])
            i = jax.nn.sigmoid(_dot(ucb, wx_ref[n]) + bx_ref[:, sl])
            nl = -lam_ref[:, sl]
            softplus = jnp.maximum(nl, 0.0) + jnp.log1p(jnp.exp(-jnp.abs(nl)))
            log_a = (-RG_C * softplus) * r
            a = jnp.exp(log_a)
            y = -jnp.tanh(log_a) * (a * a + 1.0)
            a_list.append(a)
            b_list.append(jnp.exp2(0.5 * jnp.log2(y)) * (i * ucs[n]))
        if stride == 1:
            hs = _scan_rows(a_list, b_list, [hcarry_ref[:, sl] for sl in slices])
            for n, sl in enumerate(slices):
                hcarry_ref[:, sl] = hs[n][rows - 1:rows, :]
                hlast_ref[:, sl] = hs[n][rows - 1:rows, :]
            return hs
        hs = []
        for n, sl in enumerate(slices):
            h_t = h0_ref[:, sl]
            steps = []
            for t in range(rows // stride):
                rs = slice(t * stride, (t + 1) * stride)
                h_t = a_list[n][rs, :] * h_t + b_list[n][rs, :]
                steps.append(h_t)
            hs.append(jnp.concatenate(steps, axis=0))
            hlast_ref[:, sl] = h_t
        return hs

    def output_stage(s, x, xb, hs):
        acc = jnp.zeros((rows, d), F32)
        for n, sl in enumerate(slices):
            z = _dot(xb, win_ref[:, d + n * blk:d + (n + 1) * blk])
            acc = acc + _dot((_silu(z) * hs[n]).astype(BF16), wout_ref[sl, :])
        y_ref[s * rows:(s + 1) * rows, :] = _ln_residual(x, acc, g_ref[...], b_ref[...])

    staged = conv_stage(0)
    for s in range(n_sub):
        x, xb, ucs = staged
        if s + 1 < n_sub:
            staged = conv_stage(s + 1)
        output_stage(s, x, xb, recurrence_stage(ucs))


def _lru_layer(x, h0, buf, w_in, conv_w, conv_b, w_a, b_a, w_x, b_x, lam, w_out, ln_g, ln_b, *, stride, row_tile):
    bsz, t_len, d = x.shape
    n_buf = buf.shape[1]
    n_h = h0.shape[1]
    taps = conv_w.shape[0]
    nb, blk, _ = w_a.shape
    grid = (bsz, t_len // row_tile)
    const2 = lambda b, t: (0, 0)
    n_sub = 2 if stride == 1 and row_tile % (2 * SUBLANES) == 0 else 1
    kern = functools.partial(_lru_layer_kernel, stride=stride, n_sub=n_sub)
    return pl.pallas_call(
        kern,
        grid=grid,
        in_specs=[
            pl.BlockSpec((None, row_tile, d), lambda b, t: (b, t, 0)),
            pl.BlockSpec((None, n_h, d), lambda b, t: (b, 0, 0)),
            pl.BlockSpec((None, n_buf, d), lambda b, t: (b, 0, 0)),
            pl.BlockSpec((d, 2 * d), const2),
            pl.BlockSpec((taps, d), const2),
            pl.BlockSpec((1, d), const2),
            pl.BlockSpec((nb, blk, blk), lambda b, t: (0, 0, 0)),
            pl.BlockSpec((1, d), const2),
            pl.BlockSpec((nb, blk, blk), lambda b, t: (0, 0, 0)),
            pl.BlockSpec((1, d), const2),
            pl.BlockSpec((1, d), const2),
            pl.BlockSpec((d, d), const2),
            pl.BlockSpec((1, d), const2),
            pl.BlockSpec((1, d), const2),
        ],
        out_specs=[
            pl.BlockSpec((None, row_tile, d), lambda b, t: (b, t, 0)),
            pl.BlockSpec((None, n_h, d), lambda b, t: (b, 0, 0)),
            pl.BlockSpec((None, n_buf, d), lambda b, t: (b, 0, 0)),
        ],
        out_shape=[
            jax.ShapeDtypeStruct((bsz, t_len, d), F32),
            jax.ShapeDtypeStruct((bsz, n_h, d), F32),
            jax.ShapeDtypeStruct((bsz, n_buf, d), F32),
        ],
        scratch_shapes=[pltpu.VMEM((taps - 1, d), F32), pltpu.VMEM((1, d), F32)],
        compiler_params=pltpu.CompilerParams(
            dimension_semantics=("arbitrary", "arbitrary"), vmem_limit_bytes=VMEM_LIMIT),
        name="lru_layer",
    )(x, h0, buf, w_in, conv_w, conv_b, w_a, b_a, w_x, b_x, lam, w_out, ln_g, ln_b)


D_ATT = N_HEADS * HEAD_DIM
N_ROWCOLS = 4 * N_KV * HEAD_DIM
N_WINCOLS = 2 * N_KV * HEAD_DIM
N_GATES = 3 * N_HEADS


def _nsa_proj_kernel(x_ref, wqkv_ref, wz_ref, wg_ref, q_ref, rows_ref, win_ref, sz_ref, gates_ref, *t_refs):
    xb = x_ref[...].astype(BF16)
    c0 = 0
    q_ref[...] = (_dot(xb, wqkv_ref[:, c0:c0 + D_ATT]) * (HEAD_DIM ** -0.5 * LOG2E)).astype(BF16)
    c0 += D_ATT
    rows = _dot(xb, wqkv_ref[:, c0:c0 + N_ROWCOLS])
    rows_ref[...] = rows.astype(rows_ref.dtype)
    c0 += N_ROWCOLS
    win = _dot(xb, wqkv_ref[:, c0:c0 + N_WINCOLS])
    win_ref[...] = win.astype(win_ref.dtype)
    sz_ref[...] = _silu(_dot(xb, wz_ref[...])).astype(sz_ref.dtype)
    gates_ref[...] = jax.nn.sigmoid(_dot(xb, wg_ref[...]))
    if t_refs:
        rows_t_ref, win_t_ref = t_refs
        rows_t_ref[...] = rows.T
        win_t_ref[...] = win.T


def _nsa_proj(x2, weights, *, row_tile, seq_len=None):
    n, d = x2.shape
    widths = (D_ATT, N_ROWCOLS, N_WINCOLS, D_ATT, LANES)
    kv_dtype = F32 if seq_len is None else BF16
    dtypes = (BF16, kv_dtype, kv_dtype, BF16, F32)
    out_specs = [pl.BlockSpec((row_tile, w), lambda i: (i, 0)) for w in widths]
    out_shape = [jax.ShapeDtypeStruct((n, w), dt) for w, dt in zip(widths, dtypes)]
    if seq_len is not None:
        tiles = seq_len // row_tile
        for w in (N_ROWCOLS, N_WINCOLS):
            out_specs.append(pl.BlockSpec((None, w, row_tile), lambda i: (i // tiles, 0, i % tiles)))
            out_shape.append(jax.ShapeDtypeStruct((n // seq_len, w, seq_len), F32))
    return pl.pallas_call(
        _nsa_proj_kernel,
        grid=(n // row_tile,),
        in_specs=[pl.BlockSpec((row_tile, d), lambda i: (i, 0))]
        + [pl.BlockSpec(w.shape, lambda i: (0, 0)) for w in weights],
        out_specs=out_specs,
        out_shape=out_shape,
        compiler_params=pltpu.CompilerParams(
            dimension_semantics=("arbitrary",), vmem_limit_bytes=VMEM_LIMIT),
        name="nsa_proj",
    )(x2, *weights)


def _nsa_out_kernel(x_ref, o_ref, sz_ref, w_ref, g_ref, b_ref, y_ref):
    m = (o_ref[...].astype(F32) * sz_ref[...].astype(F32)).astype(BF16)
    y_ref[...] = _ln_residual(x_ref[...], _dot(m, w_ref[...]), g_ref[...], b_ref[...])


def _nsa_out(x2, o2, sz2, w_out, ln_g, ln_b, *, row_tile):
    n, d = x2.shape
    row = lambda i: (i, 0)
    const = lambda i: (0, 0)
    return pl.pallas_call(
        _nsa_out_kernel,
        grid=(n // row_tile,),
        in_specs=[pl.BlockSpec((row_tile, d), row), pl.BlockSpec((row_tile, d), row),
                  pl.BlockSpec((row_tile, d), row), pl.BlockSpec((d, d), const),
                  pl.BlockSpec((1, d), const), pl.BlockSpec((1, d), const)],
        out_specs=pl.BlockSpec((row_tile, d), row),
        out_shape=jax.ShapeDtypeStruct((n, d), F32),
        compiler_params=pltpu.CompilerParams(
            dimension_semantics=("arbitrary",), vmem_limit_bytes=VMEM_LIMIT),
        name="nsa_out",
    )(x2, o2, sz2, w_out, ln_g, ln_b)


def _dup_half(slab, g, lo):
    rolled = pltpu.roll(slab, HEAD_DIM, 1)
    return jnp.where(lo, slab, rolled) if g == 0 else jnp.where(lo, rolled, slab)


def _select_blocks(score, nl, real, n_slc, n_sel, axis, stride):
    rank = jnp.zeros(score.shape, jnp.int32)
    for i in range(n_slc):
        j = stride * i
        cand = score[:, j:j + 1] if axis == 1 else score[j:j + 1, :]
        beats = (cand > score) | ((cand == score) & (nl > j))
        rank = rank + beats.astype(jnp.int32)
    return real & (rank < n_sel)


def _block_scores(imp2, nl, tq, n_slc, stride):
    blk = nl >> (stride - 1)
    real = ((nl & (stride - 1)) == 0) & (blk < n_slc)
    cur = tq >> (SLC_BLOCK.bit_length() - 1)
    forced = (blk == 0) | ((blk <= cur) & (blk > cur - N_LOCAL))
    score = jnp.where(forced, FORCE, jnp.where(blk <= cur, imp2, NEG_INF))
    return jnp.where(real, score, -jnp.inf), real


def _alibi_query_cols(head, n_rows):
    lane = lax.broadcasted_iota(jnp.int32, (1, LANES), 1)
    rem = jnp.full((1, LANES), _slope(head) * LOG2E, F32)
    row = jnp.zeros((1, LANES), F32)
    for i in range(3):
        piece = rem.astype(BF16).astype(F32)
        rem = rem - piece
        row = jnp.where(lane == i, float(ALIBI_RADIX) * piece, jnp.where(lane == 3 + i, piece, row))
    return jnp.broadcast_to(row, (n_rows, LANES)).astype(BF16)


def _alibi_key_cols(n_rows, period):
    r = lax.broadcasted_iota(jnp.int32, (n_rows, LANES), 0) & (period - 1)
    lane = lax.broadcasted_iota(jnp.int32, (n_rows, LANES), 1)
    hi = (r // ALIBI_RADIX).astype(F32)
    lo = (r % ALIBI_RADIX).astype(F32)
    return jnp.where(lane < 3, hi, jnp.where(lane < 6, lo, 0.0))


def _nsa_prompt_kernel(q_ref, rows_ref, win_ref, gates_ref, wc_ref, o_ref,
                       ka, vta, kwa, vwta, kcd, vct_lo, vct_hi, qa_ref, selb_ref, acc_ref):
    blk = q_ref.shape[0]
    t_len = rows_ref.shape[0]
    n_cmp = t_len // CMP_BLOCK
    n_slc = -(-t_len // SLC_BLOCK)
    n_sel = min(TOP_N, n_slc)
    n_blk = t_len // blk
    slc_per_blk = blk // SLC_BLOCK
    n_wblk = WINDOW // blk + 1
    v_rows = vta.shape[2]
    qi = pl.program_id(1)
    t0 = qi * blk
    lo = lax.broadcasted_iota(jnp.int32, (1, LANES), 1) < HEAD_DIM

    @pl.when(qi == 0)
    def _build():
        lo_t = lax.broadcasted_iota(jnp.int32, (t_len, LANES), 1) < HEAD_DIM
        lo_c = lax.broadcasted_iota(jnp.int32, (n_cmp, LANES), 1) < HEAD_DIM
        lo_r = lax.broadcasted_iota(jnp.int32, (LANES, 1), 0) < HEAD_DIM
        vrow = lax.broadcasted_iota(jnp.int32, (v_rows, 1), 0)
        w2 = jnp.concatenate([wc_ref[...]] * 2, axis=0)
        cw = rows_ref[:, 0:2 * LANES].astype(F32).reshape(n_slc, SLC_BLOCK, 2 * LANES) * w2[None]
        kvc = jnp.concatenate([jnp.sum(cw[:, 0:CMP_BLOCK, :], axis=1), jnp.sum(cw[:, CMP_BLOCK:, :], axis=1)], axis=0)
        key_cols = _alibi_key_cols(t_len, blk).astype(BF16)

        def values_t(v):
            vt = v.T[0:v_rows, :]
            return jnp.where(vrow < HEAD_DIM, vt, jnp.where(vrow == HEAD_DIM, 1.0, 0.0)).astype(BF16)

        for g in range(N_KV):
            kd = _dup_half(rows_ref[:, 2 * LANES:3 * LANES].astype(F32), g, lo_t).astype(BF16)
            ka[g] = jnp.concatenate([kd, key_cols], axis=1)
            vt = values_t(_dup_half(rows_ref[:, 3 * LANES:4 * LANES].astype(F32), g, lo_t))
            kd = _dup_half(win_ref[:, 0:LANES].astype(F32), g, lo_t).astype(BF16)
            kwa[g] = jnp.concatenate([kd, key_cols], axis=1)
            vwt = values_t(_dup_half(win_ref[:, LANES:2 * LANES].astype(F32), g, lo_t))
            for c in range(n_blk):
                vta[g, c] = vt[:, c * blk:(c + 1) * blk]
                vwta[g, c] = vwt[:, c * blk:(c + 1) * blk]
            kcd[g] = _dup_half(kvc[:, 0:LANES], g, lo_c).astype(BF16)
            vct = _dup_half(kvc[:, LANES:2 * LANES], g, lo_c).T
            vct_lo[g] = jnp.where(lo_r, vct, 0.0).astype(BF16)
            vct_hi[g] = jnp.where(lo_r, 0.0, vct).astype(BF16)

    tq = t0 + lax.broadcasted_iota(jnp.int32, (1, blk), 1)
    row_c = lax.broadcasted_iota(jnp.int32, (n_cmp, 1), 0)
    cmp_blk = jnp.where(row_c < n_slc, 2 * row_c, 2 * (row_c - n_slc) + 1)
    cmp_end = cmp_blk * CMP_BLOCK + (CMP_BLOCK - 1)
    valid_c = cmp_end <= tq
    dist_c = (tq - cmp_end).astype(F32)
    slc_r = lax.broadcasted_iota(jnp.int32, (n_slc, 1), 0)

    lr = lax.broadcasted_iota(jnp.int32, (1, blk), 1) - lax.broadcasted_iota(jnp.int32, (blk, 1), 0)
    causal_bias = jnp.where(lr >= 0, 0.0, NEG_INF)

    w_start = pl.multiple_of(jnp.maximum(t0 - WINDOW, 0), blk)
    w_blk = w_start // blk
    w_shift, w_bias = [], []
    for i in range(n_wblk):
        shift = t0 - w_start - i * blk
        dist = lr + shift
        w_shift.append(shift.astype(F32))
        w_bias.append(jnp.where((dist >= 0) & (dist < WINDOW), 0.0, NEG_INF))

    gates_t = gates_ref[...].T

    for hh in range(N_HEADS):
        q2 = q_ref[:, (hh // 2) * LANES:(hh // 2 + 1) * LANES]
        zero = jnp.zeros_like(q2)
        qm = jnp.where(lo, q2, zero) if hh % 2 == 0 else jnp.where(lo, zero, q2)
        qa_ref[hh] = jnp.concatenate([qm, _alibi_query_cols(hh, blk)], axis=1)

    o_cmp, importance = [], []
    for g in range(N_KV):
        heads = range(g * HPG, (g + 1) * HPG)
        scores = [_dot_nt(kcd[g], qa_ref[hh, :, 0:LANES]) for hh in heads]
        imp = jnp.zeros((n_cmp, blk), F32)
        probs = []
        for hh, s in zip(heads, scores):
            s = jnp.where(valid_c, s - (_slope(hh) * LOG2E) * dist_c, NEG_INF)
            e = jnp.exp2(s - jnp.max(s, axis=0, keepdims=True))
            p = jnp.where(valid_c, e, 0.0) / jnp.sum(e, axis=0, keepdims=True)
            imp = imp + p
            probs.append(p.astype(BF16))
        for j in range(HPG // 2):
            o_cmp.append(_dot(vct_lo[g], probs[2 * j]) + _dot(vct_hi[g], probs[2 * j + 1]))
        importance.append(imp[0:n_slc, :] + imp[n_slc:2 * n_slc, :])

    all_selected = t0 + blk <= n_sel * SLC_BLOCK

    @pl.when(all_selected)
    def _():
        selb_ref[...] = jnp.zeros(selb_ref.shape, F32)

    @pl.when(jnp.logical_not(all_selected))
    def _():
        for g in range(N_KV):
            score, real = _block_scores(importance[g], slc_r, tq, n_slc, stride=1)
            sel = _select_blocks(score, slc_r, real, n_slc, n_sel, axis=0, stride=1)
            sel_bias = jnp.where(sel, 0.0, NEG_INF)
            for c in range(n_blk):
                rows = sel_bias[c * slc_per_blk:(c + 1) * slc_per_blk, :]
                selb_ref[g, c] = jnp.concatenate([rows] * (SUBLANES // slc_per_blk), axis=0)

    acc_ref[...] = jnp.zeros(acc_ref.shape, F32)

    def key_block(c, ms, extra_bias):
        off = pl.multiple_of(c * blk, blk)
        shift = (t0 - off).astype(F32)
        keys, biases = [], []
        for g in range(N_KV):
            keys.append(ka[g, pl.ds(off, blk), :])
            sel_rows = selb_ref[g, c]
            bias = jnp.concatenate([jnp.broadcast_to(sel_rows[m:m + 1, :], (SLC_BLOCK, blk))
                                    for m in range(slc_per_blk)], axis=0)
            biases.append(bias if extra_bias is None else bias + extra_bias)
        scores, new_ms, probs = [], [], []
        for hh in range(N_HEADS):
            s = _dot_nt(keys[hh // HPG], qa_ref[hh]) + biases[hh // HPG]
            c_h = (_slope(hh) * LOG2E) * shift
            new_ms.append(jnp.maximum(ms[hh], jnp.max(s, axis=0, keepdims=True) - c_h))
            scores.append(s)
        for hh in range(N_HEADS):
            c_h = (_slope(hh) * LOG2E) * shift
            probs.append(jnp.exp2((scores[hh] - (new_ms[hh] + c_h)).astype(BF16)))
        for hh in range(N_HEADS):
            acc_ref[hh] = acc_ref[hh] * jnp.exp2(ms[hh] - new_ms[hh]) + _dot(vta[hh // HPG, c], probs[hh])
        return tuple(new_ms)

    ms = lax.fori_loop(0, qi, lambda c, ms: key_block(c, ms, None),
                       (jnp.full((1, blk), NEG_INF, F32),) * N_HEADS)
    key_block(qi, ms, causal_bias)

    k_win = [[kwa[g, pl.ds(pl.multiple_of(w_start + i * blk, blk), blk), :] for i in range(n_wblk)]
             for g in range(N_KV)]
    w_scores, w_max = [], []
    for hh in range(N_HEADS):
        slope = _slope(hh) * LOG2E
        tiles, m_w = [], None
        for i in range(n_wblk):
            s = _dot_nt(k_win[hh // HPG][i], qa_ref[hh]) + w_bias[i]
            m_i = jnp.max(s, axis=0, keepdims=True) - slope * w_shift[i]
            m_w = m_i if m_w is None else jnp.maximum(m_w, m_i)
            tiles.append(s)
        w_scores.append(tiles)
        w_max.append(m_w)
    o_wins = []
    for hh in range(N_HEADS):
        slope = _slope(hh) * LOG2E
        acc_w = jnp.zeros((v_rows, blk), F32)
        for i in range(n_wblk):
            p = jnp.exp2((w_scores[hh][i] - (w_max[hh] + slope * w_shift[i])).astype(BF16))
            acc_w = acc_w + _dot(vwta[hh // HPG, w_blk + i], p)
        o_wins.append(acc_w[0:HEAD_DIM, :] / acc_w[HEAD_DIM:HEAD_DIM + 1, :])

    for g in range(N_KV):
        for j in range(HPG // 2):
            outs = []
            for hh in (g * HPG + 2 * j, g * HPG + 2 * j + 1):
                o_win = o_wins[hh]
                acc = acc_ref[hh]
                o_sel = acc[0:HEAD_DIM, :] / acc[HEAD_DIM:HEAD_DIM + 1, :]
                half = (hh % 2) * HEAD_DIM
                gate = lambda c: gates_t[c * N_HEADS + hh:c * N_HEADS + hh + 1, :]
                o_c = o_cmp[g * (HPG // 2) + j][half:half + HEAD_DIM, :]
                outs.append(gate(0) * o_c + gate(1) * o_sel + gate(2) * o_win)
            c0 = g * HPG * HEAD_DIM + j * LANES
            o_ref[:, c0:c0 + LANES] = jnp.concatenate(outs, axis=0).T.astype(o_ref.dtype)


def _block_expand_matrix(n_rows, n_keys):
    n = jnp.arange(n_rows, dtype=jnp.int32)[:, None]
    p = jnp.arange(n_keys, dtype=jnp.int32)[None, :]
    return (n == 2 * (p // SLC_BLOCK)).astype(BF16)


def _nsa_prompt_attention(q, rows, win, gates, wc, *, q_tile=NSA_BLOCK):
    bsz, t_len, _ = q.shape
    assert WINDOW % q_tile == 0 and q_tile % (2 * LANES) == 0 and t_len >= WINDOW + q_tile
    n_cmp = t_len // CMP_BLOCK
    assert n_cmp % BF16_SUBLANES == 0 and t_len % SLC_BLOCK == 0 and SUBLANES % (q_tile // SLC_BLOCK) == 0
    v_rows = HEAD_DIM + BF16_SUBLANES
    k_scratch = pltpu.VMEM((N_KV, t_len, 2 * LANES), BF16)
    v_scratch = pltpu.VMEM((N_KV, t_len // q_tile, v_rows, q_tile), BF16)
    kc_scratch = pltpu.VMEM((N_KV, n_cmp, LANES), BF16)
    vc_scratch = pltpu.VMEM((N_KV, LANES, n_cmp), BF16)
    return pl.pallas_call(
        _nsa_prompt_kernel,
        grid=(bsz, t_len // q_tile),
        in_specs=[
            pl.BlockSpec((None, q_tile, D_ATT), lambda b, i: (b, i, 0)),
            pl.BlockSpec((None, t_len, N_ROWCOLS), lambda b, i: (b, 0, 0)),
            pl.BlockSpec((None, t_len, N_WINCOLS), lambda b, i: (b, 0, 0)),
            pl.BlockSpec((None, q_tile, LANES), lambda b, i: (b, i, 0)),
            pl.BlockSpec((CMP_BLOCK, 2 * LANES), lambda b, i: (0, 0)),
        ],
        out_specs=pl.BlockSpec((None, q_tile, D_ATT), lambda b, i: (b, i, 0)),
        out_shape=jax.ShapeDtypeStruct((bsz, t_len, D_ATT), BF16),
        scratch_shapes=[k_scratch, v_scratch, k_scratch, v_scratch,
                        kc_scratch, vc_scratch, vc_scratch,
                        pltpu.VMEM((N_HEADS, q_tile, 2 * LANES), BF16),
                        pltpu.VMEM((N_KV, t_len // q_tile, SUBLANES, q_tile), F32),
                        pltpu.VMEM((N_HEADS, v_rows, q_tile), F32)],
        compiler_params=pltpu.CompilerParams(
            dimension_semantics=("arbitrary", "arbitrary"), vmem_limit_bytes=VMEM_LIMIT),
        name="nsa_prompt_attention",
    )(q, rows, win, gates, wc)


def _nsa_sample_kernel(pt_ref, *refs, pages_per_step, n_steps, n_new):
    page_refs = refs[:pages_per_step]
    (qbd_ref, new_ref, winbuf_ref, gates_ref, wc_ref, e_ref, o_ref, s_ref, vst_ref, kcs, vcs) = refs[pages_per_step:]
    del pt_ref
    step = pl.program_id(1)
    keys_per_step = pages_per_step * PAGE_SIZE
    p_len = n_steps * keys_per_step
    n_keys = p_len + LANES
    n_cmp = p_len // CMP_BLOCK
    n_slc = -(-(p_len + n_new) // SLC_BLOCK)
    n_sel = min(TOP_N, n_slc)
    cmp_per_step = keys_per_step // CMP_BLOCK
    n_rows = N_KV * n_new * HPG
    gd = N_KV * HEAD_DIM

    def slab(r, kind):
        return r[kind].reshape(gd, r.shape[-1])

    cm = jnp.concatenate([jnp.concatenate([slab(r, 0).T, slab(r, 1).T], axis=1) for r in page_refs], axis=0)
    kvc = jnp.sum(cm.reshape(cmp_per_step, CMP_BLOCK, 2 * LANES) * wc_ref[...][None], axis=1)
    c_off = pl.multiple_of(step * cmp_per_step, cmp_per_step)
    kcs[pl.ds(c_off, cmp_per_step), :] = kvc[:, 0:LANES].astype(BF16)
    vcs[pl.ds(c_off, cmp_per_step), :] = kvc[:, LANES:2 * LANES].astype(BF16)
    s_ref[step] = _dot(qbd_ref[...], jnp.concatenate([slab(r, 2).astype(BF16) for r in page_refs], axis=1))
    vst_ref[step] = jnp.concatenate([slab(r, 3).astype(BF16) for r in page_refs], axis=1)

    @pl.when(step == n_steps - 1)
    def _attend():
        n_wb = winbuf_ref.shape[-1]
        w_keys = n_wb + LANES
        qbd = qbd_ref[...]
        r = lax.broadcasted_iota(jnp.int32, (n_rows, 1), 0)
        head = (r // (n_new * HPG)) * HPG + (r % HPG)
        slope = jnp.exp2((head + 1).astype(F32) * (-8.0 / N_HEADS)) * LOG2E
        tq = p_len + (r // HPG) % n_new

        def attend(s, values_t, dist, bias):
            s = s - slope * dist.astype(F32) + bias
            e = jnp.exp2(s - jnp.max(s, axis=-1, keepdims=True))
            eb = e.astype(BF16)
            o = sum(_dot_nt(eb[:, ks], vt) for ks, vt in values_t)
            return o / jnp.sum(e, axis=-1, keepdims=True)

        nl = lax.broadcasted_iota(jnp.int32, (n_rows, n_cmp), 1)
        cmp_end = nl * CMP_BLOCK + (CMP_BLOCK - 1)
        valid_c = cmp_end <= tq
        s = _dot_nt(qbd, kcs[...]) - slope * (tq - cmp_end).astype(F32)
        s = jnp.where(valid_c, s, NEG_INF)
        e = jnp.exp2(s - jnp.max(s, axis=-1, keepdims=True))
        p = jnp.where(valid_c, e, 0.0) / jnp.sum(e, axis=-1, keepdims=True)
        o_cmp = _dot(p.astype(BF16), vcs[...])

        n_gt = N_KV * n_new
        imp = jnp.sum(p.reshape(n_gt, HPG, n_cmp), axis=1)
        imp2 = imp + pltpu.roll(imp, n_cmp - 1, 1)
        w_sel = -(-2 * n_slc // LANES) * LANES
        imp2 = jnp.concatenate([imp2, jnp.zeros((n_gt, w_sel - n_cmp), F32)], axis=1)
        nl8 = lax.broadcasted_iota(jnp.int32, (n_gt, w_sel), 1)
        r8 = lax.broadcasted_iota(jnp.int32, (n_gt, 1), 0)
        score, real = _block_scores(imp2, nl8, p_len + r8 % n_new, n_slc, stride=2)
        sel = _select_blocks(score, nl8, real, n_slc, n_sel, axis=1, stride=2)
        selb = jnp.where(sel, 1.0, 0.0)
        selb = jnp.broadcast_to(selb[:, None, :], (n_gt, HPG, w_sel)).reshape(n_rows, w_sel).astype(BF16)
        keys_per_tile = e_ref.shape[1]
        hit = [_dot(selb[:, j * LANES:(j + 1) * LANES], e_ref[...]) for j in range(p_len // keys_per_tile)]
        hit.append(jnp.broadcast_to(selb[:, 2 * (p_len // SLC_BLOCK):2 * (p_len // SLC_BLOCK) + 1].astype(F32),
                                    (n_rows, LANES)))
        hit = jnp.concatenate(hit, axis=1)
        dist = tq - lax.broadcasted_iota(jnp.int32, (n_rows, n_keys), 1)
        new_k, new_v, new_kw, new_vw = (new_ref[i] for i in range(4))
        s_sel = jnp.concatenate([s_ref[i] for i in range(n_steps)] + [_dot(qbd, new_k)], axis=1)
        values_t = [(slice(i * keys_per_step, (i + 1) * keys_per_step), vst_ref[i]) for i in range(n_steps)]
        o_sel = attend(s_sel, values_t + [(slice(p_len, n_keys), new_v)], dist,
                       jnp.where((hit > 0.5) & (dist >= 0), 0.0, NEG_INF))

        dist_w = tq - (p_len - n_wb) - lax.broadcasted_iota(jnp.int32, (n_rows, w_keys), 1)
        s_win = jnp.concatenate([_dot(qbd, slab(winbuf_ref, 0).astype(BF16)), _dot(qbd, new_kw)], axis=1)
        o_win = attend(s_win, [(slice(0, n_wb), slab(winbuf_ref, 1).astype(BF16)), (slice(n_wb, w_keys), new_vw)],
                       dist_w, jnp.where((dist_w >= 0) & (dist_w < WINDOW), 0.0, NEG_INF))

        o_ref[...] = (gates_ref[:, 0:1] * o_cmp + gates_ref[:, 1:2] * o_sel + gates_ref[:, 2:3] * o_win)


def _nsa_sample_attention(page_table, cache_t, qbd, new_t, winbuf_t, gates, wc, *, n_new,
                          pages_per_step=PAGES_PER_STEP):
    bsz, n_pages = page_table.shape
    n_steps = n_pages // pages_per_step
    keys_per_step = pages_per_step * PAGE_SIZE
    p_len = n_pages * PAGE_SIZE
    n_rows = qbd.shape[1]
    n_wb = winbuf_t.shape[-1]
    n_cmp = p_len // CMP_BLOCK
    gd = N_KV * HEAD_DIM
    keys_per_tile = (LANES // 2) * SLC_BLOCK
    assert p_len % keys_per_tile == 0 and n_new <= SLC_BLOCK
    e_mat = _block_expand_matrix(LANES, keys_per_tile)
    kern = functools.partial(_nsa_sample_kernel, pages_per_step=pages_per_step, n_steps=n_steps, n_new=n_new)

    def page_spec(k):
        return pl.BlockSpec((None, 4, N_KV, HEAD_DIM, PAGE_SIZE),
                            lambda b, s, pt: (pt[b, s * pages_per_step + k], 0, 0, 0, 0))

    per_b = lambda b, s, pt: (b, 0, 0)
    const = lambda b, s, pt: (0, 0)
    grid_spec = pltpu.PrefetchScalarGridSpec(
        num_scalar_prefetch=1,
        grid=(bsz, n_steps),
        in_specs=[page_spec(k) for k in range(pages_per_step)] + [
            pl.BlockSpec((None, n_rows, LANES), per_b),
            pl.BlockSpec((None, 4, gd, LANES), lambda b, s, pt: (b, 0, 0, 0)),
            pl.BlockSpec((None, 2, N_KV, HEAD_DIM, n_wb), lambda b, s, pt: (b, 0, 0, 0, 0)),
            pl.BlockSpec((None, n_rows, LANES), per_b),
            pl.BlockSpec((CMP_BLOCK, 2 * LANES), const),
            pl.BlockSpec((LANES, keys_per_tile), const),
        ],
        out_specs=pl.BlockSpec((None, n_rows, LANES), per_b),
        scratch_shapes=[
            pltpu.VMEM((n_steps, n_rows, keys_per_step), F32),
            pltpu.VMEM((n_steps, gd, keys_per_step), BF16),
            pltpu.VMEM((n_cmp, LANES), BF16), pltpu.VMEM((n_cmp, LANES), BF16),
        ],
    )
    return pl.pallas_call(
        kern,
        grid_spec=grid_spec,
        out_shape=jax.ShapeDtypeStruct((bsz, n_rows, LANES), F32),
        compiler_params=pltpu.CompilerParams(
            dimension_semantics=("arbitrary", "arbitrary"), vmem_limit_bytes=VMEM_LIMIT),
        name="nsa_sample_attention",
    )(page_table, *([cache_t] * pages_per_step), qbd, new_t, winbuf_t, gates, wc, e_mat)


def _nsa_weights(w_in):
    c_g = D_ATT + N_ROWCOLS + N_WINCOLS
    c_z = c_g + N_GATES
    w_gates = jnp.pad(w_in[:, c_g:c_z], ((0, 0), (0, LANES - N_GATES)))
    return w_in[:, :c_g].astype(BF16), w_in[:, c_z:].astype(BF16), w_gates.astype(BF16)


def _cmp_weight_tile(w_cmp):
    return w_cmp.transpose(1, 0, 2, 3).reshape(CMP_BLOCK, 2 * N_KV * HEAD_DIM)


def _nsa_layer_prompt(x, w_all, wc, w_out, ln_g, ln_b):
    bsz, t_len, d = x.shape
    x2 = x.reshape(bsz * t_len, d)
    q, rows, win, sz, gates, rows_t, win_t = _nsa_proj(x2, w_all, row_tile=ROW_TILE, seq_len=t_len)
    o = _nsa_prompt_attention(q.reshape(bsz, t_len, D_ATT), rows.reshape(bsz, t_len, N_ROWCOLS),
                              win.reshape(bsz, t_len, N_WINCOLS), gates.reshape(bsz, t_len, LANES), wc)
    y = _nsa_out(x2, o.reshape(bsz * t_len, D_ATT), sz, w_out, ln_g, ln_b, row_tile=ROW_TILE)
    keep = min(WINDOW, t_len)
    rows_out = rows_t.reshape(bsz, 4, N_KV, HEAD_DIM, t_len).transpose(0, 4, 1, 2, 3)
    win_out = win_t[:, :, t_len - keep:].reshape(bsz, 2, N_KV, HEAD_DIM, keep).transpose(0, 4, 1, 2, 3)
    return y.reshape(bsz, t_len, d), rows_out, win_out


def _nsa_layer_sample(x, cache, win_buf, page_table, w_all, wc, w_out, ln_g, ln_b):
    bsz, n_new, d = x.shape
    x2 = x.reshape(bsz * n_new, d)
    q, rows, win, sz, gates = _nsa_proj(x2, w_all, row_tile=bsz * n_new)
    q5 = q.reshape(bsz, n_new, N_KV, HPG, HEAD_DIM).transpose(0, 2, 1, 3, 4)
    eye = jnp.eye(N_KV, dtype=q.dtype)
    qbd = (q5[:, :, :, :, None, :] * eye[None, :, None, None, :, None]).reshape(
        bsz, N_KV * n_new * HPG, N_KV * HEAD_DIM)
    g5 = gates[:, :N_GATES].reshape(bsz, n_new, 3, N_KV, HPG).transpose(0, 3, 1, 4, 2)
    g_rows = jnp.pad(g5.reshape(bsz, N_KV * n_new * HPG, 3), ((0, 0), (0, 0), (0, LANES - 3)))
    rows3 = rows.reshape(bsz, n_new, N_ROWCOLS)
    win3 = win.reshape(bsz, n_new, N_WINCOLS)
    n_wb = win_buf.shape[1]
    gd = N_KV * HEAD_DIM
    new_t = jnp.stack([rows3[:, :, 2 * gd:3 * gd], rows3[:, :, 3 * gd:4 * gd],
                       win3[:, :, 0:gd], win3[:, :, gd:2 * gd]], axis=1).transpose(0, 1, 3, 2)
    new_t = jnp.pad(new_t, ((0, 0), (0, 0), (0, 0), (0, LANES - n_new))).astype(BF16)
    o_rows = _nsa_sample_attention(
        page_table, cache.transpose(0, 2, 3, 4, 1), qbd, new_t, win_buf.transpose(0, 2, 3, 4, 1),
        g_rows, wc, n_new=n_new)
    o6 = o_rows.reshape(bsz, N_KV, n_new, HPG, N_KV, HEAD_DIM)
    o = jnp.stack([o6[:, g, :, :, g, :] for g in range(N_KV)], axis=2).reshape(bsz * n_new, D_ATT)
    y = _nsa_out(x2, o, sz, w_out, ln_g, ln_b, row_tile=bsz * n_new)
    all_win = jnp.concatenate([win_buf, win3.reshape(bsz, n_new, 2, N_KV, HEAD_DIM)], axis=1)
    return (y.reshape(bsz, n_new, d), rows3.reshape(bsz, n_new, 4, N_KV, HEAD_DIM),
            all_win[:, all_win.shape[1] - n_wb:])


def _time_major(a):
    bsz, k, d = a.shape
    return a.transpose(1, 0, 2).reshape(1, k * bsz, d)


def _batch_major(a, bsz):
    _, kb, d = a.shape
    return a.reshape(kb // bsz, bsz, d).transpose(1, 0, 2)


def kernel(x_prompt, x_sample, cache_nsa_kv, cache_nsa_win, state_conv_a, state_lru_h, state_lru_conv, page_table,
           ln_g, ln_b, a_w_in, a_conv_w, a_w_out, b_w_in, b_w_cmp, b_w_out,
           c_w_in, c_conv_w, c_conv_b, c_w_a, c_b_a, c_w_x, c_b_x, c_lam, c_w_out):
    bp, t_len, d = x_prompt.shape
    bs, n_new, _ = x_sample.shape
    yp = x_prompt
    ys = _time_major(x_sample)
    conv_a_p, conv_a_s = [], []
    rows_p, rows_s, win_p, win_s = [], [], [], []
    h_p, h_s, cc_p, cc_s = [], [], [], []
    row2 = lambda v: v.reshape(1, -1)
    a_w_in_b, a_w_out_b = a_w_in.astype(BF16), a_w_out.astype(BF16)
    for i in range(DEPTH):
        kind, j = i % 3, i // 3
        g, b = row2(ln_g[i]), row2(ln_b[i])
        if kind == 0:
            taps = a_conv_w.shape[1]
            yp, sp = _conv_layer(yp, jnp.zeros((bp, taps - 1, d), F32), a_w_in_b, a_conv_w, a_w_out_b, g, b,
                                 layer=j, stride=1, row_tile=2 * ROW_TILE)
            ys, ss = _conv_layer(ys, _time_major(state_conv_a[j]), a_w_in_b, a_conv_w, a_w_out_b, g, b,
                                 layer=j, stride=bs, row_tile=n_new * bs)
            conv_a_p.append(sp)
            conv_a_s.append(_batch_major(ss, bs))
        elif kind == 1:
            w_all = _nsa_weights(b_w_in[j])
            wc = _cmp_weight_tile(b_w_cmp[j])
            w_out = b_w_out[j].astype(BF16)
            yp, rp, wp = _nsa_layer_prompt(yp, w_all, wc, w_out, g, b)
            ys_b, rs, ws = _nsa_layer_sample(_batch_major(ys, bs), cache_nsa_kv[j], cache_nsa_win[j], page_table,
                                             w_all, wc, w_out, g, b)
            ys = _time_major(ys_b)
            rows_p.append(rp)
            rows_s.append(rs)
            win_p.append(wp)
            win_s.append(ws)
        else:
            w_in, w_out = c_w_in[j].astype(BF16), c_w_out[j].astype(BF16)
            w_a, w_x = c_w_a[j].astype(BF16), c_w_x[j].astype(BF16)
            taps = c_conv_w.shape[1]
            args = (w_in, c_conv_w[j], row2(c_conv_b[j]), w_a, row2(c_b_a[j]), w_x, row2(c_b_x[j]),
                    row2(c_lam[j]), w_out, g, b)
            yp, hp, cp = _lru_layer(yp, jnp.zeros((bp, 1, d), F32), jnp.zeros((bp, taps - 1, d), F32), *args,
                                    stride=1, row_tile=ROW_TILE)
            ys, hs, cs = _lru_layer(ys, state_lru_h[j].reshape(1, bs, d), _time_major(state_lru_conv[j]), *args,
                                    stride=bs, row_tile=n_new * bs)
            h_p.append(hp.reshape(bp, d))
            h_s.append(hs.reshape(bs, d))
            cc_p.append(cp)
            cc_s.append(_batch_major(cs, bs))
    return (yp, _batch_major(ys, bs), jnp.stack(conv_a_p), jnp.stack(conv_a_s), jnp.stack(rows_p), jnp.stack(rows_s),
            jnp.stack(win_p), jnp.stack(win_s), jnp.stack(h_p), jnp.stack(h_s), jnp.stack(cc_p), jnp.stack(cc_s))
```

```python
import functools

import jax
import jax.numpy as jnp
from jax import lax
from jax.experimental import pallas as pl
from jax.experimental.pallas import tpu as pltpu

F32 = jnp.float32
BF16 = jnp.bfloat16

DEPTH = 4
N_HEADS = 16
N_KV = 2
HPG = N_HEADS // N_KV
HEAD_DIM = 64
CMP_BLOCK = 32
SLC_BLOCK = 64
TOP_N = 16
N_LOCAL = 2
WINDOW = 512
PAGE_SIZE = 128
NEG_INF = -1e30
FORCE = 1e9
RG_C = 8.0
N_RG_BLOCKS = 4
ALPHA = (2 * DEPTH) ** 0.25
LN_EPS = 1e-5
LOG2E = 1.4426950408889634

LANES = 128
SUBLANES = 8
BF16_SUBLANES = 16
VMEM_LIMIT = 56 * 1024 * 1024

ROW_TILE = 512
NSA_BLOCK = 256
PAGES_PER_STEP = 64
ALIBI_RADIX = 16


def _dot(a, b):
    return jnp.dot(a, b, preferred_element_type=F32)


def _dot_nt(a, b):
    return lax.dot_general(a, b, (((1,), (1,)), ((), ())), preferred_element_type=F32)


def _silu(z):
    return z * jax.nn.sigmoid(z)


def _ln_residual(x, y, g, b):
    r = ALPHA * x + y
    mu = jnp.mean(r, axis=-1, keepdims=True)
    c = r - mu
    var = jnp.mean(c * c, axis=-1, keepdims=True)
    return c * lax.rsqrt(var + LN_EPS) * g + b


def _slope(head):
    return 2.0 ** (-8.0 * (head + 1) / N_HEADS)


def _prev_rows(u, carry, k):
    n_carry = carry.shape[0]
    rolled = pltpu.roll(u, k, 0)
    row = lax.broadcasted_iota(jnp.int32, (SUBLANES, 1), 0)
    head = rolled[0:SUBLANES, :]
    for i in range(k):
        head = jnp.where(row == i, carry[n_carry - k + i:n_carry - k + i + 1, :], head)
    return jnp.concatenate([head, rolled[SUBLANES:, :]], axis=0)


def _conv_layer_kernel(x_ref, buf_ref, win_ref, cw_ref, wout_ref, g_ref, b_ref,
                       y_ref, nbuf_ref, carry_ref, *, stride, col_chunk, n_sub):
    d = x_ref.shape[1]
    rows = x_ref.shape[0] // n_sub
    taps = cw_ref.shape[0]
    n_chunks = d // col_chunk
    if stride == 1:
        @pl.when(pl.program_id(1) == 0)
        def _():
            carry_ref[...] = buf_ref[...]

    def mixer_chunk(xb, acc, j):
        lo = j * col_chunk
        sl = slice(lo, lo + col_chunk)
        h, bg, cg, z = [_dot(xb, win_ref[:, part * d + lo:part * d + lo + col_chunk]) for part in range(4)]
        u = cg * h
        conv = u * cw_ref[taps - 1:taps, sl]
        if stride == 1:
            carry = carry_ref[:, sl]
            for k in range(1, taps):
                conv = conv + _prev_rows(u, carry, k) * cw_ref[taps - 1 - k:taps - k, sl]
            new_buf = u[rows - (taps - 1):rows, :]
            carry_ref[:, sl] = new_buf
        else:
            ext = jnp.concatenate([buf_ref[:, sl], u], axis=0)
            for k in range(1, taps):
                s0 = (taps - 1 - k) * stride
                conv = conv + ext[s0:s0 + rows, :] * cw_ref[taps - 1 - k:taps - k, sl]
            new_buf = ext[rows:rows + (taps - 1) * stride, :]
        nbuf_ref[:, sl] = new_buf
        m = _silu(z) * bg * conv
        return acc + _dot(m.astype(BF16), wout_ref[sl, :])

    pending = None
    for s in range(n_sub):
        x = x_ref[s * rows:(s + 1) * rows, :]
        xb = x.astype(BF16)
        acc = jnp.zeros((rows, d), F32)
        for j in range(n_chunks):
            acc = mixer_chunk(xb, acc, j)
            if j == 0 and pending is not None:
                ps, px, pacc = pending
                y_ref[ps * rows:(ps + 1) * rows, :] = _ln_residual(px, pacc, g_ref[...], b_ref[...])
        pending = (s, x, acc)
    ps, px, pacc = pending
    y_ref[ps * rows:(ps + 1) * rows, :] = _ln_residual(px, pacc, g_ref[...], b_ref[...])


def _conv_layer(x, buf, w_in, conv_w, w_out, ln_g, ln_b, *, layer, stride, row_tile):
    bsz, t_len, d = x.shape
    n_buf = buf.shape[1]
    taps = conv_w.shape[1]
    grid = (bsz, t_len // row_tile)
    n_sub = 2 if stride == 1 and row_tile % (2 * SUBLANES) == 0 else 1
    kern = functools.partial(_conv_layer_kernel, stride=stride, col_chunk=512, n_sub=n_sub)
    return pl.pallas_call(
        kern,
        grid=grid,
        in_specs=[
            pl.BlockSpec((None, row_tile, d), lambda b, t: (b, t, 0)),
            pl.BlockSpec((None, n_buf, d), lambda b, t: (b, 0, 0)),
            pl.BlockSpec((None, d, 4 * d), lambda b, t: (layer, 0, 0)),
            pl.BlockSpec((None, taps, d), lambda b, t: (layer, 0, 0)),
            pl.BlockSpec((None, d, d), lambda b, t: (layer, 0, 0)),
            pl.BlockSpec((1, d), lambda b, t: (0, 0)),
            pl.BlockSpec((1, d), lambda b, t: (0, 0)),
        ],
        out_specs=[
            pl.BlockSpec((None, row_tile, d), lambda b, t: (b, t, 0)),
            pl.BlockSpec((None, n_buf, d), lambda b, t: (b, 0, 0)),
        ],
        out_shape=[
            jax.ShapeDtypeStruct((bsz, t_len, d), F32),
            jax.ShapeDtypeStruct((bsz, n_buf, d), F32),
        ],
        scratch_shapes=[pltpu.VMEM((taps - 1, d), F32)],
        compiler_params=pltpu.CompilerParams(
            dimension_semantics=("arbitrary", "arbitrary"), vmem_limit_bytes=VMEM_LIMIT),
        name="conv_layer",
    )(x, buf, w_in, conv_w, w_out, ln_g, ln_b)


def _scan_rows(a_list, b_list, h_prev_list):
    rows, cols = a_list[0].shape
    groups = rows // SUBLANES
    a3 = [a.reshape(groups, SUBLANES, cols) for a in a_list]
    b3 = [b.reshape(groups, SUBLANES, cols) for b in b_list]
    sub = lax.broadcasted_iota(jnp.int32, (1, SUBLANES, 1), 1)
    s = 1
    while s < SUBLANES:
        live = sub >= s
        for n, (a, b) in enumerate(zip(a3, b3)):
            a_sh = jnp.where(live, pltpu.roll(a, s, 1), 1.0)
            b_sh = jnp.where(live, pltpu.roll(b, s, 1), 0.0)
            b3[n] = a * b_sh + b
            a3[n] = a * a_sh
        s *= 2
    carries = list(h_prev_list)
    outs = [[] for _ in a3]
    for j in range(groups):
        for n in range(len(a3)):
            h = a3[n][j] * carries[n] + b3[n][j]
            carries[n] = h[SUBLANES - 1:SUBLANES, :]
            outs[n].append(h)
    return [jnp.concatenate(o, axis=0) for o in outs]


def _lru_layer_kernel(x_ref, h0_ref, buf_ref, win_ref, cw_ref, cb_ref, wa_ref, ba_ref, wx_ref, bx_ref,
                      lam_ref, wout_ref, g_ref, b_ref,
                      y_ref, hlast_ref, nbuf_ref, carry_ref, hcarry_ref, *, stride, n_sub):
    d = x_ref.shape[1]
    rows = x_ref.shape[0] // n_sub
    taps = cw_ref.shape[0]
    n_blocks, blk, _ = wa_ref.shape
    slices = [slice(n * blk, (n + 1) * blk) for n in range(n_blocks)]
    if stride == 1:
        @pl.when(pl.program_id(1) == 0)
        def _():
            carry_ref[...] = buf_ref[...]
            hcarry_ref[...] = h0_ref[...]

    def conv_stage(s):
        x = x_ref[s * rows:(s + 1) * rows, :]
        xb = x.astype(BF16)
        ucs = []
        for sl in slices:
            u = _dot(xb, win_ref[:, sl])
            uc = u * cw_ref[taps - 1:taps, sl] + cb_ref[:, sl]
            if stride == 1:
                carry = carry_ref[:, sl]
                for k in range(1, taps):
                    uc = uc + _prev_rows(u, carry, k) * cw_ref[taps - 1 - k:taps - k, sl]
                new_buf = u[rows - (taps - 1):rows, :]
                carry_ref[:, sl] = new_buf
            else:
                ext = jnp.concatenate([buf_ref[:, sl], u], axis=0)
                for k in range(1, taps):
                    s0 = (taps - 1 - k) * stride
                    uc = uc + ext[s0:s0 + rows, :] * cw_ref[taps - 1 - k:taps - k, sl]
                new_buf = ext[rows:rows + (taps - 1) * stride, :]
            nbuf_ref[:, sl] = new_buf
            ucs.append(uc)
        return x, xb, ucs

    def recurrence_stage(ucs):
        a_list, b_list = [], []
        for n, sl in enumerate(slices):
            ucb = ucs[n].astype(BF16)
            r = jax.nn.sigmoid(_dot(ucb, wa_ref[n]) + ba_ref[:, sl])
            i = jax.nn.sigmoid(_dot(ucb, wx_ref[n]) + bx_ref[:, sl])
            nl = -lam_ref[:, sl]
            softplus = jnp.maximum(nl, 0.0) + jnp.log1p(jnp.exp(-jnp.abs(nl)))
            log_a = (-RG_C * softplus) * r
            a = jnp.exp(log_a)
            y = -jnp.tanh(log_a) * (a * a + 1.0)
            a_list.append(a)
            b_list.append(jnp.exp2(0.5 * jnp.log2(y)) * (i * ucs[n]))
        if stride == 1:
            hs = _scan_rows(a_list, b_list, [hcarry_ref[:, sl] for sl in slices])
            for n, sl in enumerate(slices):
                hcarry_ref[:, sl] = hs[n][rows - 1:rows, :]
                hlast_ref[:, sl] = hs[n][rows - 1:rows, :]
            return hs
        hs = []
        for n, sl in enumerate(slices):
            h_t = h0_ref[:, sl]
            steps = []
            for t in range(rows // stride):
                rs = slice(t * stride, (t + 1) * stride)
                h_t = a_list[n][rs, :] * h_t + b_list[n][rs, :]
                steps.append(h_t)
            hs.append(jnp.concatenate(steps, axis=0))
            hlast_ref[:, sl] = h_t
        return hs

    def output_stage(s, x, xb, hs):
        acc = jnp.zeros((rows, d), F32)
        for n, sl in enumerate(slices):
            z = _dot(xb, win_ref[:, d + n * blk:d + (n + 1) * blk])
            acc = acc + _dot((_silu(z) * hs[n]).astype(BF16), wout_ref[sl, :])
        y_ref[s * rows:(s + 1) * rows, :] = _ln_residual(x, acc, g_ref[...], b_ref[...])

    staged = conv_stage(0)
    for s in range(n_sub):
        x, xb, ucs = staged
        if s + 1 < n_sub:
            staged = conv_stage(s + 1)
        output_stage(s, x, xb, recurrence_stage(ucs))


def _lru_layer(x, h0, buf, w_in, conv_w, conv_b, w_a, b_a, w_x, b_x, lam, w_out, ln_g, ln_b, *, stride, row_tile):
    bsz, t_len, d = x.shape
    n_buf = buf.shape[1]
    n_h = h0.shape[1]
    taps = conv_w.shape[0]
    nb, blk, _ = w_a.shape
    grid = (bsz, t_len // row_tile)
    const2 = lambda b, t: (0, 0)
    n_sub = 2 if stride == 1 and row_tile % (2 * SUBLANES) == 0 else 1
    kern = functools.partial(_lru_layer_kernel, stride=stride, n_sub=n_sub)
    return pl.pallas_call(
        kern,
        grid=grid,
        in_specs=[
            pl.BlockSpec((None, row_tile, d), lambda b, t: (b, t, 0)),
            pl.BlockSpec((None, n_h, d), lambda b, t: (b, 0, 0)),
            pl.BlockSpec((None, n_buf, d), lambda b, t: (b, 0, 0)),
            pl.BlockSpec((d, 2 * d), const2),
            pl.BlockSpec((taps, d), const2),
            pl.BlockSpec((1, d), const2),
            pl.BlockSpec((nb, blk, blk), lambda b, t: (0, 0, 0)),
            pl.BlockSpec((1, d), const2),
            pl.BlockSpec((nb, blk, blk), lambda b, t: (0, 0, 0)),
            pl.BlockSpec((1, d), const2),
            pl.BlockSpec((1, d), const2),
            pl.BlockSpec((d, d), const2),
            pl.BlockSpec((1, d), const2),
            pl.BlockSpec((1, d), const2),
        ],
        out_specs=[
            pl.BlockSpec((None, row_tile, d), lambda b, t: (b, t, 0)),
            pl.BlockSpec((None, n_h, d), lambda b, t: (b, 0, 0)),
            pl.BlockSpec((None, n_buf, d), lambda b, t: (b, 0, 0)),
        ],
        out_shape=[
            jax.ShapeDtypeStruct((bsz, t_len, d), F32),
            jax.ShapeDtypeStruct((bsz, n_h, d), F32),
            jax.ShapeDtypeStruct((bsz, n_buf, d), F32),
        ],
        scratch_shapes=[pltpu.VMEM((taps - 1, d), F32), pltpu.VMEM((1, d), F32)],
        compiler_params=pltpu.CompilerParams(
            dimension_semantics=("arbitrary", "arbitrary"), vmem_limit_bytes=VMEM_LIMIT),
        name="lru_layer",
    )(x, h0, buf, w_in, conv_w, conv_b, w_a, b_a, w_x, b_x, lam, w_out, ln_g, ln_b)


D_ATT = N_HEADS * HEAD_DIM
N_ROWCOLS = 4 * N_KV * HEAD_DIM
N_WINCOLS = 2 * N_KV * HEAD_DIM
N_GATES = 3 * N_HEADS


def _nsa_proj_kernel(x_ref, wqkv_ref, wz_ref, wg_ref, q_ref, rows_ref, win_ref, sz_ref, gates_ref, *t_refs):
    xb = x_ref[...].astype(BF16)
    c0 = 0
    q_ref[...] = (_dot(xb, wqkv_ref[:, c0:c0 + D_ATT]) * (HEAD_DIM ** -0.5 * LOG2E)).astype(BF16)
    c0 += D_ATT
    rows = _dot(xb, wqkv_ref[:, c0:c0 + N_ROWCOLS])
    rows_ref[...] = rows.astype(rows_ref.dtype)
    c0 += N_ROWCOLS
    win = _dot(xb, wqkv_ref[:, c0:c0 + N_WINCOLS])
    win_ref[...] = win.astype(win_ref.dtype)
    sz_ref[...] = _silu(_dot(xb, wz_ref[...])).astype(sz_ref.dtype)
    gates_ref[...] = jax.nn.sigmoid(_dot(xb, wg_ref[...]))
    if t_refs:
        rows_t_ref, win_t_ref = t_refs
        rows_t_ref[...] = rows.T
        win_t_ref[...] = win.T


def _nsa_proj(x2, weights, *, row_tile, seq_len=None):
    n, d = x2.shape
    widths = (D_ATT, N_ROWCOLS, N_WINCOLS, D_ATT, LANES)
    kv_dtype = F32 if seq_len is None else BF16
    dtypes = (BF16, kv_dtype, kv_dtype, BF16, F32)
    out_specs = [pl.BlockSpec((row_tile, w), lambda i: (i, 0)) for w in widths]
    out_shape = [jax.ShapeDtypeStruct((n, w), dt) for w, dt in zip(widths, dtypes)]
    if seq_len is not None:
        tiles = seq_len // row_tile
        for w in (N_ROWCOLS, N_WINCOLS):
            out_specs.append(pl.BlockSpec((None, w, row_tile), lambda i: (i // tiles, 0, i % tiles)))
            out_shape.append(jax.ShapeDtypeStruct((n // seq_len, w, seq_len), F32))
    return pl.pallas_call(
        _nsa_proj_kernel,
        grid=(n // row_tile,),
        in_specs=[pl.BlockSpec((row_tile, d), lambda i: (i, 0))]
        + [pl.BlockSpec(w.shape, lambda i: (0, 0)) for w in weights],
        out_specs=out_specs,
        out_shape=out_shape,
        compiler_params=pltpu.CompilerParams(
            dimension_semantics=("arbitrary",), vmem_limit_bytes=VMEM_LIMIT),
        name="nsa_proj",
    )(x2, *weights)


def _nsa_out_kernel(x_ref, o_ref, sz_ref, w_ref, g_ref, b_ref, y_ref):
    m = (o_ref[...].astype(F32) * sz_ref[...].astype(F32)).astype(BF16)
    y_ref[...] = _ln_residual(x_ref[...], _dot(m, w_ref[...]), g_ref[...], b_ref[...])


def _nsa_out(x2, o2, sz2, w_out, ln_g, ln_b, *, row_tile):
    n, d = x2.shape
    row = lambda i: (i, 0)
    const = lambda i: (0, 0)
    return pl.pallas_call(
        _nsa_out_kernel,
        grid=(n // row_tile,),
        in_specs=[pl.BlockSpec((row_tile, d), row), pl.BlockSpec((row_tile, d), row),
                  pl.BlockSpec((row_tile, d), row), pl.BlockSpec((d, d), const),
                  pl.BlockSpec((1, d), const), pl.BlockSpec((1, d), const)],
        out_specs=pl.BlockSpec((row_tile, d), row),
        out_shape=jax.ShapeDtypeStruct((n, d), F32),
        compiler_params=pltpu.CompilerParams(
            dimension_semantics=("arbitrary",), vmem_limit_bytes=VMEM_LIMIT),
        name="nsa_out",
    )(x2, o2, sz2, w_out, ln_g, ln_b)


def _dup_half(slab, g, lo):
    rolled = pltpu.roll(slab, HEAD_DIM, 1)
    return jnp.where(lo, slab, rolled) if g == 0 else jnp.where(lo, rolled, slab)


def _select_blocks(score, nl, real, n_slc, n_sel, axis, stride):
    rank = jnp.zeros(score.shape, jnp.int32)
    for i in range(n_slc):
        j = stride * i
        cand = score[:, j:j + 1] if axis == 1 else score[j:j + 1, :]
        beats = (cand > score) | ((cand == score) & (nl > j))
        rank = rank + beats.astype(jnp.int32)
    return real & (rank < n_sel)


def _block_scores(imp2, nl, tq, n_slc, stride):
    blk = nl >> (stride - 1)
    real = ((nl & (stride - 1)) == 0) & (blk < n_slc)
    cur = tq >> (SLC_BLOCK.bit_length() - 1)
    forced = (blk == 0) | ((blk <= cur) & (blk > cur - N_LOCAL))
    score = jnp.where(forced, FORCE, jnp.where(blk <= cur, imp2, NEG_INF))
    return jnp.where(real, score, -jnp.inf), real


def _alibi_query_cols(head, n_rows):
    lane = lax.broadcasted_iota(jnp.int32, (1, LANES), 1)
    rem = jnp.full((1, LANES), _slope(head) * LOG2E, F32)
    row = jnp.zeros((1, LANES), F32)
    for i in range(3):
        piece = rem.astype(BF16).astype(F32)
        rem = rem - piece
        row = jnp.where(lane == i, float(ALIBI_RADIX) * piece, jnp.where(lane == 3 + i, piece, row))
    return jnp.broadcast_to(row, (n_rows, LANES)).astype(BF16)


def _alibi_key_cols(n_rows, period):
    r = lax.broadcasted_iota(jnp.int32, (n_rows, LANES), 0) & (period - 1)
    lane = lax.broadcasted_iota(jnp.int32, (n_rows, LANES), 1)
    hi = (r // ALIBI_RADIX).astype(F32)
    lo = (r % ALIBI_RADIX).astype(F32)
    return jnp.where(lane < 3, hi, jnp.where(lane < 6, lo, 0.0))


def _nsa_prompt_kernel(q_ref, rows_ref, win_ref, gates_ref, wc_ref, o_ref,
                       ka, vta, kwa, vwta, kcd, vct_lo, vct_hi, qa_ref, selb_ref, acc_ref):
    blk = q_ref.shape[0]
    t_len = rows_ref.shape[0]
    n_cmp = t_len // CMP_BLOCK
    n_slc = -(-t_len // SLC_BLOCK)
    n_sel = min(TOP_N, n_slc)
    n_blk = t_len // blk
    slc_per_blk = blk // SLC_BLOCK
    n_wblk = WINDOW // blk + 1
    v_rows = vta.shape[2]
    qi = pl.program_id(1)
    t0 = qi * blk
    lo = lax.broadcasted_iota(jnp.int32, (1, LANES), 1) < HEAD_DIM

    @pl.when(qi == 0)
    def _build():
        lo_t = lax.broadcasted_iota(jnp.int32, (t_len, LANES), 1) < HEAD_DIM
        lo_c = lax.broadcasted_iota(jnp.int32, (n_cmp, LANES), 1) < HEAD_DIM
        lo_r = lax.broadcasted_iota(jnp.int32, (LANES, 1), 0) < HEAD_DIM
        vrow = lax.broadcasted_iota(jnp.int32, (v_rows, 1), 0)
        w2 = jnp.concatenate([wc_ref[...]] * 2, axis=0)
        cw = rows_ref[:, 0:2 * LANES].astype(F32).reshape(n_slc, SLC_BLOCK, 2 * LANES) * w2[None]
        kvc = jnp.concatenate([jnp.sum(cw[:, 0:CMP_BLOCK, :], axis=1), jnp.sum(cw[:, CMP_BLOCK:, :], axis=1)], axis=0)
        key_cols = _alibi_key_cols(t_len, blk).astype(BF16)

        def values_t(v):
            vt = v.T[0:v_rows, :]
            return jnp.where(vrow < HEAD_DIM, vt, jnp.where(vrow == HEAD_DIM, 1.0, 0.0)).astype(BF16)

        for g in range(N_KV):
            kd = _dup_half(rows_ref[:, 2 * LANES:3 * LANES].astype(F32), g, lo_t).astype(BF16)
            ka[g] = jnp.concatenate([kd, key_cols], axis=1)
            vt = values_t(_dup_half(rows_ref[:, 3 * LANES:4 * LANES].astype(F32), g, lo_t))
            kd = _dup_half(win_ref[:, 0:LANES].astype(F32), g, lo_t).astype(BF16)
            kwa[g] = jnp.concatenate([kd, key_cols], axis=1)
            vwt = values_t(_dup_half(win_ref[:, LANES:2 * LANES].astype(F32), g, lo_t))
            for c in range(n_blk):
                vta[g, c] = vt[:, c * blk:(c + 1) * blk]
                vwta[g, c] = vwt[:, c * blk:(c + 1) * blk]
            kcd[g] = _dup_half(kvc[:, 0:LANES], g, lo_c).astype(BF16)
            vct = _dup_half(kvc[:, LANES:2 * LANES], g, lo_c).T
            vct_lo[g] = jnp.where(lo_r, vct, 0.0).astype(BF16)
            vct_hi[g] = jnp.where(lo_r, 0.0, vct).astype(BF16)

    tq = t0 + lax.broadcasted_iota(jnp.int32, (1, blk), 1)
    row_c = lax.broadcasted_iota(jnp.int32, (n_cmp, 1), 0)
    cmp_blk = jnp.where(row_c < n_slc, 2 * row_c, 2 * (row_c - n_slc) + 1)
    cmp_end = cmp_blk * CMP_BLOCK + (CMP_BLOCK - 1)
    valid_c = cmp_end <= tq
    dist_c = (tq - cmp_end).astype(F32)
    slc_r = lax.broadcasted_iota(jnp.int32, (n_slc, 1), 0)

    lr = lax.broadcasted_iota(jnp.int32, (1, blk), 1) - lax.broadcasted_iota(jnp.int32, (blk, 1), 0)
    causal_bias = jnp.where(lr >= 0, 0.0, NEG_INF)

    w_start = pl.multiple_of(jnp.maximum(t0 - WINDOW, 0), blk)
    w_blk = w_start // blk
    w_shift, w_bias = [], []
    for i in range(n_wblk):
        shift = t0 - w_start - i * blk
        dist = lr + shift
        w_shift.append(shift.astype(F32))
        w_bias.append(jnp.where((dist >= 0) & (dist < WINDOW), 0.0, NEG_INF))

    gates_t = gates_ref[...].T

    for hh in range(N_HEADS):
        q2 = q_ref[:, (hh // 2) * LANES:(hh // 2 + 1) * LANES]
        zero = jnp.zeros_like(q2)
        qm = jnp.where(lo, q2, zero) if hh % 2 == 0 else jnp.where(lo, zero, q2)
        qa_ref[hh] = jnp.concatenate([qm, _alibi_query_cols(hh, blk)], axis=1)

    o_cmp = []
    for g in range(N_KV):
        heads = range(g * HPG, (g + 1) * HPG)
        scores = [_dot_nt(kcd[g], qa_ref[hh, :, 0:LANES]) for hh in heads]
        imp = jnp.zeros((n_cmp, blk), F32)
        probs = []
        for hh, s in zip(heads, scores):
            s = jnp.where(valid_c, s - (_slope(hh) * LOG2E) * dist_c, NEG_INF)
            e = jnp.exp2(s - jnp.max(s, axis=0, keepdims=True))
            p = jnp.where(valid_c, e, 0.0) / jnp.sum(e, axis=0, keepdims=True)
            imp = imp + p
            probs.append(p.astype(BF16))
        for j in range(HPG // 2):
            o_cmp.append(_dot(vct_lo[g], probs[2 * j]) + _dot(vct_hi[g], probs[2 * j + 1]))

        imp2 = imp[0:n_slc, :] + imp[n_slc:2 * n_slc, :]
        score, real = _block_scores(imp2, slc_r, tq, n_slc, stride=1)
        sel = _select_blocks(score, slc_r, real, n_slc, n_sel, axis=0, stride=1)
        sel_bias = jnp.where(sel, 0.0, NEG_INF)
        for c in range(n_blk):
            rows = sel_bias[c * slc_per_blk:(c + 1) * slc_per_blk, :]
            selb_ref[g, c] = jnp.concatenate([rows] * (SUBLANES // slc_per_blk), axis=0)

    acc_ref[...] = jnp.zeros(acc_ref.shape, F32)

    def key_block(c, ms, extra_bias):
        off = pl.multiple_of(c * blk, blk)
        shift = (t0 - off).astype(F32)
        keys, biases = [], []
        for g in range(N_KV):
            keys.append(ka[g, pl.ds(off, blk), :])
            sel_rows = selb_ref[g, c]
            bias = jnp.concatenate([jnp.broadcast_to(sel_rows[m:m + 1, :], (SLC_BLOCK, blk))
                                    for m in range(slc_per_blk)], axis=0)
            biases.append(bias if extra_bias is None else bias + extra_bias)
        scores, new_ms, probs = [], [], []
        for hh in range(N_HEADS):
            s = _dot_nt(keys[hh // HPG], qa_ref[hh]) + biases[hh // HPG]
            c_h = (_slope(hh) * LOG2E) * shift
            new_ms.append(jnp.maximum(ms[hh], jnp.max(s, axis=0, keepdims=True) - c_h))
            scores.append(s)
        for hh in range(N_HEADS):
            c_h = (_slope(hh) * LOG2E) * shift
            probs.append(jnp.exp2((scores[hh] - (new_ms[hh] + c_h)).astype(BF16)))
        for hh in range(N_HEADS):
            acc_ref[hh] = acc_ref[hh] * jnp.exp2(ms[hh] - new_ms[hh]) + _dot(vta[hh // HPG, c], probs[hh])
        return tuple(new_ms)

    ms = lax.fori_loop(0, qi, lambda c, ms: key_block(c, ms, None),
                       (jnp.full((1, blk), NEG_INF, F32),) * N_HEADS)
    key_block(qi, ms, causal_bias)

    k_win = [[kwa[g, pl.ds(pl.multiple_of(w_start + i * blk, blk), blk), :] for i in range(n_wblk)]
             for g in range(N_KV)]
    w_scores, w_max = [], []
    for hh in range(N_HEADS):
        slope = _slope(hh) * LOG2E
        tiles, m_w = [], None
        for i in range(n_wblk):
            s = _dot_nt(k_win[hh // HPG][i], qa_ref[hh]) + w_bias[i]
            m_i = jnp.max(s, axis=0, keepdims=True) - slope * w_shift[i]
            m_w = m_i if m_w is None else jnp.maximum(m_w, m_i)
            tiles.append(s)
        w_scores.append(tiles)
        w_max.append(m_w)
    o_wins = []
    for hh in range(N_HEADS):
        slope = _slope(hh) * LOG2E
        acc_w = jnp.zeros((v_rows, blk), F32)
        for i in range(n_wblk):
            p = jnp.exp2((w_scores[hh][i] - (w_max[hh] + slope * w_shift[i])).astype(BF16))
            acc_w = acc_w + _dot(vwta[hh // HPG, w_blk + i], p)
        o_wins.append(acc_w[0:HEAD_DIM, :] / acc_w[HEAD_DIM:HEAD_DIM + 1, :])

    for g in range(N_KV):
        for j in range(HPG // 2):
            outs = []
            for hh in (g * HPG + 2 * j, g * HPG + 2 * j + 1):
                o_win = o_wins[hh]
                acc = acc_ref[hh]
                o_sel = acc[0:HEAD_DIM, :] / acc[HEAD_DIM:HEAD_DIM + 1, :]
                half = (hh % 2) * HEAD_DIM
                gate = lambda c: gates_t[c * N_HEADS + hh:c * N_HEADS + hh + 1, :]
                o_c = o_cmp[g * (HPG // 2) + j][half:half + HEAD_DIM, :]
                outs.append(gate(0) * o_c + gate(1) * o_sel + gate(2) * o_win)
            c0 = g * HPG * HEAD_DIM + j * LANES
            o_ref[:, c0:c0 + LANES] = jnp.concatenate(outs, axis=0).T.astype(o_ref.dtype)


def _block_expand_matrix(n_rows, n_keys):
    n = jnp.arange(n_rows, dtype=jnp.int32)[:, None]
    p = jnp.arange(n_keys, dtype=jnp.int32)[None, :]
    return (n == 2 * (p // SLC_BLOCK)).astype(BF16)


def _nsa_prompt_attention(q, rows, win, gates, wc, *, q_tile=NSA_BLOCK):
    bsz, t_len, _ = q.shape
    assert WINDOW % q_tile == 0 and q_tile % (2 * LANES) == 0 and t_len >= WINDOW + q_tile
    n_cmp = t_len // CMP_BLOCK
    assert n_cmp % BF16_SUBLANES == 0 and t_len % SLC_BLOCK == 0 and SUBLANES % (q_tile // SLC_BLOCK) == 0
    v_rows = HEAD_DIM + BF16_SUBLANES
    k_scratch = pltpu.VMEM((N_KV, t_len, 2 * LANES), BF16)
    v_scratch = pltpu.VMEM((N_KV, t_len // q_tile, v_rows, q_tile), BF16)
    kc_scratch = pltpu.VMEM((N_KV, n_cmp, LANES), BF16)
    vc_scratch = pltpu.VMEM((N_KV, LANES, n_cmp), BF16)
    return pl.pallas_call(
        _nsa_prompt_kernel,
        grid=(bsz, t_len // q_tile),
        in_specs=[
            pl.BlockSpec((None, q_tile, D_ATT), lambda b, i: (b, i, 0)),
            pl.BlockSpec((None, t_len, N_ROWCOLS), lambda b, i: (b, 0, 0)),
            pl.BlockSpec((None, t_len, N_WINCOLS), lambda b, i: (b, 0, 0)),
            pl.BlockSpec((None, q_tile, LANES), lambda b, i: (b, i, 0)),
            pl.BlockSpec((CMP_BLOCK, 2 * LANES), lambda b, i: (0, 0)),
        ],
        out_specs=pl.BlockSpec((None, q_tile, D_ATT), lambda b, i: (b, i, 0)),
        out_shape=jax.ShapeDtypeStruct((bsz, t_len, D_ATT), BF16),
        scratch_shapes=[k_scratch, v_scratch, k_scratch, v_scratch,
                        kc_scratch, vc_scratch, vc_scratch,
                        pltpu.VMEM((N_HEADS, q_tile, 2 * LANES), BF16),
                        pltpu.VMEM((N_KV, t_len // q_tile, SUBLANES, q_tile), F32),
                        pltpu.VMEM((N_HEADS, v_rows, q_tile), F32)],
        compiler_params=pltpu.CompilerParams(
            dimension_semantics=("arbitrary", "arbitrary"), vmem_limit_bytes=VMEM_LIMIT),
        name="nsa_prompt_attention",
    )(q, rows, win, gates, wc)


def _nsa_sample_kernel(pt_ref, *refs, pages_per_step, n_steps, n_new):
    page_refs = refs[:pages_per_step]
    (qbd_ref, new_ref, winbuf_ref, gates_ref, wc_ref, e_ref, o_ref, s_ref, vst_ref, kcs, vcs) = refs[pages_per_step:]
    del pt_ref
    step = pl.program_id(1)
    keys_per_step = pages_per_step * PAGE_SIZE
    p_len = n_steps * keys_per_step
    n_keys = p_len + LANES
    n_cmp = p_len // CMP_BLOCK
    n_slc = -(-(p_len + n_new) // SLC_BLOCK)
    n_sel = min(TOP_N, n_slc)
    cmp_per_step = keys_per_step // CMP_BLOCK
    n_rows = N_KV * n_new * HPG
    gd = N_KV * HEAD_DIM

    def slab(r, kind):
        return r[kind].reshape(gd, r.shape[-1])

    cm = jnp.concatenate([jnp.concatenate([slab(r, 0).T, slab(r, 1).T], axis=1) for r in page_refs], axis=0)
    kvc = jnp.sum(cm.reshape(cmp_per_step, CMP_BLOCK, 2 * LANES) * wc_ref[...][None], axis=1)
    c_off = pl.multiple_of(step * cmp_per_step, cmp_per_step)
    kcs[pl.ds(c_off, cmp_per_step), :] = kvc[:, 0:LANES].astype(BF16)
    vcs[pl.ds(c_off, cmp_per_step), :] = kvc[:, LANES:2 * LANES].astype(BF16)
    s_ref[step] = _dot(qbd_ref[...], jnp.concatenate([slab(r, 2).astype(BF16) for r in page_refs], axis=1))
    vst_ref[step] = jnp.concatenate([slab(r, 3).astype(BF16) for r in page_refs], axis=1)

    @pl.when(step == n_steps - 1)
    def _attend():
        n_wb = winbuf_ref.shape[-1]
        w_keys = n_wb + LANES
        qbd = qbd_ref[...]
        r = lax.broadcasted_iota(jnp.int32, (n_rows, 1), 0)
        head = (r // (n_new * HPG)) * HPG + (r % HPG)
        slope = jnp.exp2((head + 1).astype(F32) * (-8.0 / N_HEADS)) * LOG2E
        tq = p_len + (r // HPG) % n_new

        def attend(s, values_t, dist, bias):
            s = s - slope * dist.astype(F32) + bias
            e = jnp.exp2(s - jnp.max(s, axis=-1, keepdims=True))
            eb = e.astype(BF16)
            o = sum(_dot_nt(eb[:, ks], vt) for ks, vt in values_t)
            return o / jnp.sum(e, axis=-1, keepdims=True)

        nl = lax.broadcasted_iota(jnp.int32, (n_rows, n_cmp), 1)
        cmp_end = nl * CMP_BLOCK + (CMP_BLOCK - 1)
        valid_c = cmp_end <= tq
        s = _dot_nt(qbd, kcs[...]) - slope * (tq - cmp_end).astype(F32)
        s = jnp.where(valid_c, s, NEG_INF)
        e = jnp.exp2(s - jnp.max(s, axis=-1, keepdims=True))
        p = jnp.where(valid_c, e, 0.0) / jnp.sum(e, axis=-1, keepdims=True)
        o_cmp = _dot(p.astype(BF16), vcs[...])

        n_gt = N_KV * n_new
        imp = jnp.sum(p.reshape(n_gt, HPG, n_cmp), axis=1)
        imp2 = imp + pltpu.roll(imp, n_cmp - 1, 1)
        w_sel = -(-2 * n_slc // LANES) * LANES
        imp2 = jnp.concatenate([imp2, jnp.zeros((n_gt, w_sel - n_cmp), F32)], axis=1)
        nl8 = lax.broadcasted_iota(jnp.int32, (n_gt, w_sel), 1)
        r8 = lax.broadcasted_iota(jnp.int32, (n_gt, 1), 0)
        score, real = _block_scores(imp2, nl8, p_len + r8 % n_new, n_slc, stride=2)
        sel = _select_blocks(score, nl8, real, n_slc, n_sel, axis=1, stride=2)
        selb = jnp.where(sel, 1.0, 0.0)
        selb = jnp.broadcast_to(selb[:, None, :], (n_gt, HPG, w_sel)).reshape(n_rows, w_sel).astype(BF16)
        keys_per_tile = e_ref.shape[1]
        hit = [_dot(selb[:, j * LANES:(j + 1) * LANES], e_ref[...]) for j in range(p_len // keys_per_tile)]
        hit.append(jnp.broadcast_to(selb[:, 2 * (p_len // SLC_BLOCK):2 * (p_len // SLC_BLOCK) + 1].astype(F32),
                                    (n_rows, LANES)))
        hit = jnp.concatenate(hit, axis=1)
        dist = tq - lax.broadcasted_iota(jnp.int32, (n_rows, n_keys), 1)
        new_k, new_v, new_kw, new_vw = (new_ref[i] for i in range(4))
        s_sel = jnp.concatenate([s_ref[i] for i in range(n_steps)] + [_dot(qbd, new_k)], axis=1)
        values_t = [(slice(i * keys_per_step, (i + 1) * keys_per_step), vst_ref[i]) for i in range(n_steps)]
        o_sel = attend(s_sel, values_t + [(slice(p_len, n_keys), new_v)], dist,
                       jnp.where((hit > 0.5) & (dist >= 0), 0.0, NEG_INF))

        dist_w = tq - (p_len - n_wb) - lax.broadcasted_iota(jnp.int32, (n_rows, w_keys), 1)
        s_win = jnp.concatenate([_dot(qbd, slab(winbuf_ref, 0).astype(BF16)), _dot(qbd, new_kw)], axis=1)
        o_win = attend(s_win, [(slice(0, n_wb), slab(winbuf_ref, 1).astype(BF16)), (slice(n_wb, w_keys), new_vw)],
                       dist_w, jnp.where((dist_w >= 0) & (dist_w < WINDOW), 0.0, NEG_INF))

        o_ref[...] = (gates_ref[:, 0:1] * o_cmp + gates_ref[:, 1:2] * o_sel + gates_ref[:, 2:3] * o_win)


def _nsa_sample_attention(page_table, cache_t, qbd, new_t, winbuf_t, gates, wc, *, n_new,
                          pages_per_step=PAGES_PER_STEP):
    bsz, n_pages = page_table.shape
    n_steps = n_pages // pages_per_step
    keys_per_step = pages_per_step * PAGE_SIZE
    p_len = n_pages * PAGE_SIZE
    n_rows = qbd.shape[1]
    n_wb = winbuf_t.shape[-1]
    n_cmp = p_len // CMP_BLOCK
    gd = N_KV * HEAD_DIM
    keys_per_tile = (LANES // 2) * SLC_BLOCK
    assert p_len % keys_per_tile == 0 and n_new <= SLC_BLOCK
    e_mat = _block_expand_matrix(LANES, keys_per_tile)
    kern = functools.partial(_nsa_sample_kernel, pages_per_step=pages_per_step, n_steps=n_steps, n_new=n_new)

    def page_spec(k):
        return pl.BlockSpec((None, 4, N_KV, HEAD_DIM, PAGE_SIZE),
                            lambda b, s, pt: (pt[b, s * pages_per_step + k], 0, 0, 0, 0))

    per_b = lambda b, s, pt: (b, 0, 0)
    const = lambda b, s, pt: (0, 0)
    grid_spec = pltpu.PrefetchScalarGridSpec(
        num_scalar_prefetch=1,
        grid=(bsz, n_steps),
        in_specs=[page_spec(k) for k in range(pages_per_step)] + [
            pl.BlockSpec((None, n_rows, LANES), per_b),
            pl.BlockSpec((None, 4, gd, LANES), lambda b, s, pt: (b, 0, 0, 0)),
            pl.BlockSpec((None, 2, N_KV, HEAD_DIM, n_wb), lambda b, s, pt: (b, 0, 0, 0, 0)),
            pl.BlockSpec((None, n_rows, LANES), per_b),
            pl.BlockSpec((CMP_BLOCK, 2 * LANES), const),
            pl.BlockSpec((LANES, keys_per_tile), const),
        ],
        out_specs=pl.BlockSpec((None, n_rows, LANES), per_b),
        scratch_shapes=[
            pltpu.VMEM((n_steps, n_rows, keys_per_step), F32),
            pltpu.VMEM((n_steps, gd, keys_per_step), BF16),
            pltpu.VMEM((n_cmp, LANES), BF16), pltpu.VMEM((n_cmp, LANES), BF16),
        ],
    )
    return pl.pallas_call(
        kern,
        grid_spec=grid_spec,
        out_shape=jax.ShapeDtypeStruct((bsz, n_rows, LANES), F32),
        compiler_params=pltpu.CompilerParams(
            dimension_semantics=("arbitrary", "arbitrary"), vmem_limit_bytes=VMEM_LIMIT),
        name="nsa_sample_attention",
    )(page_table, *([cache_t] * pages_per_step), qbd, new_t, winbuf_t, gates, wc, e_mat)


def _nsa_weights(w_in):
    c_g = D_ATT + N_ROWCOLS + N_WINCOLS
    c_z = c_g + N_GATES
    w_gates = jnp.pad(w_in[:, c_g:c_z], ((0, 0), (0, LANES - N_GATES)))
    return w_in[:, :c_g].astype(BF16), w_in[:, c_z:].astype(BF16), w_gates.astype(BF16)


def _cmp_weight_tile(w_cmp):
    return w_cmp.transpose(1, 0, 2, 3).reshape(CMP_BLOCK, 2 * N_KV * HEAD_DIM)


def _nsa_layer_prompt(x, w_all, wc, w_out, ln_g, ln_b):
    bsz, t_len, d = x.shape
    x2 = x.reshape(bsz * t_len, d)
    q, rows, win, sz, gates, rows_t, win_t = _nsa_proj(x2, w_all, row_tile=ROW_TILE, seq_len=t_len)
    o = _nsa_prompt_attention(q.reshape(bsz, t_len, D_ATT), rows.reshape(bsz, t_len, N_ROWCOLS),
                              win.reshape(bsz, t_len, N_WINCOLS), gates.reshape(bsz, t_len, LANES), wc)
    y = _nsa_out(x2, o.reshape(bsz * t_len, D_ATT), sz, w_out, ln_g, ln_b, row_tile=ROW_TILE)
    keep = min(WINDOW, t_len)
    rows_out = rows_t.reshape(bsz, 4, N_KV, HEAD_DIM, t_len).transpose(0, 4, 1, 2, 3)
    win_out = win_t[:, :, t_len - keep:].reshape(bsz, 2, N_KV, HEAD_DIM, keep).transpose(0, 4, 1, 2, 3)
    return y.reshape(bsz, t_len, d), rows_out, win_out


def _nsa_layer_sample(x, cache, win_buf, page_table, w_all, wc, w_out, ln_g, ln_b):
    bsz, n_new, d = x.shape
    x2 = x.reshape(bsz * n_new, d)
    q, rows, win, sz, gates = _nsa_proj(x2, w_all, row_tile=bsz * n_new)
    q5 = q.reshape(bsz, n_new, N_KV, HPG, HEAD_DIM).transpose(0, 2, 1, 3, 4)
    eye = jnp.eye(N_KV, dtype=q.dtype)
    qbd = (q5[:, :, :, :, None, :] * eye[None, :, None, None, :, None]).reshape(
        bsz, N_KV * n_new * HPG, N_KV * HEAD_DIM)
    g5 = gates[:, :N_GATES].reshape(bsz, n_new, 3, N_KV, HPG).transpose(0, 3, 1, 4, 2)
    g_rows = jnp.pad(g5.reshape(bsz, N_KV * n_new * HPG, 3), ((0, 0), (0, 0), (0, LANES - 3)))
    rows3 = rows.reshape(bsz, n_new, N_ROWCOLS)
    win3 = win.reshape(bsz, n_new, N_WINCOLS)
    n_wb = win_buf.shape[1]
    gd = N_KV * HEAD_DIM
    new_t = jnp.stack([rows3[:, :, 2 * gd:3 * gd], rows3[:, :, 3 * gd:4 * gd],
                       win3[:, :, 0:gd], win3[:, :, gd:2 * gd]], axis=1).transpose(0, 1, 3, 2)
    new_t = jnp.pad(new_t, ((0, 0), (0, 0), (0, 0), (0, LANES - n_new))).astype(BF16)
    o_rows = _nsa_sample_attention(
        page_table, cache.transpose(0, 2, 3, 4, 1), qbd, new_t, win_buf.transpose(0, 2, 3, 4, 1),
        g_rows, wc, n_new=n_new)
    o6 = o_rows.reshape(bsz, N_KV, n_new, HPG, N_KV, HEAD_DIM)
    o = jnp.stack([o6[:, g, :, :, g, :] for g in range(N_KV)], axis=2).reshape(bsz * n_new, D_ATT)
    y = _nsa_out(x2, o, sz, w_out, ln_g, ln_b, row_tile=bsz * n_new)
    all_win = jnp.concatenate([win_buf, win3.reshape(bsz, n_new, 2, N_KV, HEAD_DIM)], axis=1)
    return (y.reshape(bsz, n_new, d), rows3.reshape(bsz, n_new, 4, N_KV, HEAD_DIM),
            all_win[:, all_win.shape[1] - n_wb:])


def _time_major(a):
    bsz, k, d = a.shape
    return a.transpose(1, 0, 2).reshape(1, k * bsz, d)


def _batch_major(a, bsz):
    _, kb, d = a.shape
    return a.reshape(kb // bsz, bsz, d).transpose(1, 0, 2)


def kernel(x_prompt, x_sample, cache_nsa_kv, cache_nsa_win, state_conv_a, state_lru_h, state_lru_conv, page_table,
           ln_g, ln_b, a_w_in, a_conv_w, a_w_out, b_w_in, b_w_cmp, b_w_out,
           c_w_in, c_conv_w, c_conv_b, c_w_a, c_b_a, c_w_x, c_b_x, c_lam, c_w_out):
    bp, t_len, d = x_prompt.shape
    bs, n_new, _ = x_sample.shape
    yp = x_prompt
    ys = _time_major(x_sample)
    conv_a_p, conv_a_s = [], []
    rows_p, rows_s, win_p, win_s = [], [], [], []
    h_p, h_s, cc_p, cc_s = [], [], [], []
    row2 = lambda v: v.reshape(1, -1)
    a_w_in_b, a_w_out_b = a_w_in.astype(BF16), a_w_out.astype(BF16)
    for i in range(DEPTH):
        kind, j = i % 3, i // 3
        g, b = row2(ln_g[i]), row2(ln_b[i])
        if kind == 0:
            taps = a_conv_w.shape[1]
            yp, sp = _conv_layer(yp, jnp.zeros((bp, taps - 1, d), F32), a_w_in_b, a_conv_w, a_w_out_b, g, b,
                                 layer=j, stride=1, row_tile=2 * ROW_TILE)
            ys, ss = _conv_layer(ys, _time_major(state_conv_a[j]), a_w_in_b, a_conv_w, a_w_out_b, g, b,
                                 layer=j, stride=bs, row_tile=n_new * bs)
            conv_a_p.append(sp)
            conv_a_s.append(_batch_major(ss, bs))
        elif kind == 1:
            w_all = _nsa_weights(b_w_in[j])
            wc = _cmp_weight_tile(b_w_cmp[j])
            w_out = b_w_out[j].astype(BF16)
            yp, rp, wp = _nsa_layer_prompt(yp, w_all, wc, w_out, g, b)
            ys_b, rs, ws = _nsa_layer_sample(_batch_major(ys, bs), cache_nsa_kv[j], cache_nsa_win[j], page_table,
                                             w_all, wc, w_out, g, b)
            ys = _time_major(ys_b)
            rows_p.append(rp)
            rows_s.append(rs)
            win_p.append(wp)
            win_s.append(ws)
        else:
            w_in, w_out = c_w_in[j].astype(BF16), c_w_out[j].astype(BF16)
            w_a, w_x = c_w_a[j].astype(BF16), c_w_x[j].astype(BF16)
            taps = c_conv_w.shape[1]
            args = (w_in, c_conv_w[j], row2(c_conv_b[j]), w_a, row2(c_b_a[j]), w_x, row2(c_b_x[j]),
                    row2(c_lam[j]), w_out, g, b)
            yp, hp, cp = _lru_layer(yp, jnp.zeros((bp, 1, d), F32), jnp.zeros((bp, taps - 1, d), F32), *args,
                                    stride=1, row_tile=2 * ROW_TILE)
            ys, hs, cs = _lru_layer(ys, state_lru_h[j].reshape(1, bs, d), _time_major(state_lru_conv[j]), *args,
                                    stride=bs, row_tile=n_new * bs)
            h_p.append(hp.reshape(bp, d))
            h_s.append(hs.reshape(bs, d))
            cc_p.append(cp)
            cc_s.append(_batch_major(cs, bs))
    return (yp, _batch_major(ys, bs), jnp.stack(conv_a_p), jnp.stack(conv_a_s), jnp.stack(rows_p), jnp.stack(rows_s),
            jnp.stack(win_p), jnp.stack(win_s), jnp.stack(h_p), jnp.stack(h_s), jnp.stack(cc_p), jnp.stack(cc_s))
```
